```python
import jax, jax.numpy as jnp
from jax import lax
import numpy as np

D_MODEL = 1024
BATCH = 32
SEQ = 256
DEPTH = 2
DEC_BATCH = 4
DEC_SEQ = 4096
PAST_LEN = 256

GRID_W = 64
HEAD_DIM = 64
HALF = D_MODEL // 2
NA_HEADS = HALF // HEAD_DIM
NA_WIN_R = 8
NA_WIN_C = 16
FT_GROUPS = 4
FT_GROUP_CH = HALF // FT_GROUPS
RET_HEADS = HALF // HEAD_DIM
RET_CHUNK = 128
Q_BLOCK = 128
D_FF = 256 * ((8 * D_MODEL // 3 + 255) // 256)
N_EXPERTS = 8
TOP_K = 2
D_FF_EXPERT = 7 * D_MODEL // 2
MOE_BLOCK = 256
N_EVEN = (DEPTH + 1) // 2
N_ODD = DEPTH // 2
EPS = 1e-6
NEG_INF = -1e30
ATT_SCALE = HEAD_DIM ** -0.5

kernel_name = 'hybrid_diffusion_prefix_trunk_step'


def rmsnorm(x, g):
    xf = x.astype(jnp.float32)
    y = xf * lax.rsqrt(jnp.mean(xf * xf, axis=-1, keepdims=True) + EPS)
    return (y * g.astype(jnp.float32)).astype(x.dtype)


def swiglu(x, wg, wu, wd):
    return (jax.nn.silu(x @ wg) * (x @ wu)) @ wd


def short_conv(u, w):
    up = jnp.pad(u, ((0, 0), (1, 1), (0, 0)))
    return up[:, :-2] * w[:, 0] + up[:, 1:-1] * w[:, 1] + up[:, 2:] * w[:, 2]


def context_attention(q, k, v):
    B, S, H, Dh = q.shape
    nq = S // Q_BLOCK
    qb = q.reshape(B, nq, Q_BLOCK, H, Dh).transpose(1, 0, 2, 3, 4)

    def attend(qi):
        s = jnp.einsum('bqhd,bkhd->bhqk', qi, k).astype(jnp.float32) * ATT_SCALE
        p = jax.nn.softmax(s, axis=-1).astype(v.dtype)
        return jnp.einsum('bhqk,bkhd->bqhd', p, v)

    return lax.map(attend, qb).transpose(1, 0, 2, 3, 4).reshape(B, S, H, Dh)


def neighborhood_attention(q, k, v, k_ctx, v_ctx, rpb):
    B, N, H, Dh = q.shape
    rows = N // GRID_W
    wr = min(NA_WIN_R, rows)
    qg = q.reshape(B, rows, GRID_W, H, Dh)
    r = jnp.arange(rows)
    key_rows = jnp.clip(r - wr // 2, 0, rows - wr)[:, None] + jnp.arange(wr)[None, :]
    kb = k.reshape(B, rows, GRID_W, H, Dh)[:, key_rows]
    vb = v.reshape(B, rows, GRID_W, H, Dh)[:, key_rows].reshape(B, rows, wr * GRID_W, H, Dh)
    cols = jnp.arange(GRID_W)
    col_start = jnp.clip(cols - NA_WIN_C // 2, 0, GRID_W - NA_WIN_C)
    col_in = (cols[None, :] >= col_start[:, None]) & (cols[None, :] < col_start[:, None] + NA_WIN_C)
    row_off = key_rows - r[:, None] + (NA_WIN_R - 1)
    col_off = jnp.clip(cols[None, :] - cols[:, None], 1 - NA_WIN_C, NA_WIN_C - 1) + (NA_WIN_C - 1)
    bias = rpb.astype(jnp.float32)[:, row_off[:, :, None, None], col_off[None, None]]
    bias = jnp.where(col_in[:, None, :], bias.transpose(1, 0, 3, 2, 4), NEG_INF)
    s_loc = jnp.einsum('brqhd,brwkhd->brhqwk', qg, kb).astype(jnp.float32) * ATT_SCALE + bias
    s_ctx = jnp.einsum('brqhd,bchd->brhqc', qg, k_ctx).astype(jnp.float32) * ATT_SCALE
    n_loc = wr * GRID_W
    s = jnp.concatenate([s_loc.reshape(B, rows, H, GRID_W, n_loc), s_ctx], axis=-1)
    p = jax.nn.softmax(s, axis=-1).astype(v.dtype)
    o = (jnp.einsum('brhqn,brnhd->brqhd', p[..., :n_loc], vb)
         + jnp.einsum('brhqc,bchd->brqhd', p[..., n_loc:], v_ctx))
    return o.reshape(B, N, H, Dh)


def fourier_mix(u):
    B, S, _ = u.shape
    ug = u.reshape(B, S, FT_GROUPS, FT_GROUP_CH).astype(jnp.float32)
    f = jnp.fft.fftn(ug, axes=(1, 3), norm='ortho').real
    return f.reshape(B, S, HALF).astype(u.dtype)


def retention_chunkwise(q, k, v, log_gamma, s0):
    B, N, H, _ = q.shape
    Dv = v.shape[-1]
    L = RET_CHUNK
    nc = N // L

    def to_chunks(t):
        return t.astype(jnp.float32).reshape(B, nc, L, H, t.shape[-1]).transpose(1, 0, 3, 2, 4)

    idx = jnp.arange(L, dtype=jnp.float32)
    diff = idx[:, None] - idx[None, :]
    decay = jnp.where(diff >= 0, jnp.exp(log_gamma[:, None, None] * jnp.maximum(diff, 0.0)), 0.0)
    xi = jnp.exp(log_gamma[:, None] * (idx + 1.0))[None, :, :, None]
    zeta = jnp.exp(log_gamma[:, None] * (L - 1.0 - idx))[None, :, :, None]
    g_chunk = jnp.exp(log_gamma * L)[None, :, None, None]

    def step(s, chunk):
        qc, kc, vc = chunk
        inner = jnp.einsum('bhld,bhmd->bhlm', qc, kc) * decay
        o = jnp.einsum('bhlm,bhmv->bhlv', inner, vc) + jnp.einsum('bhld,bhdv->bhlv', qc, s) * xi
        s = s * g_chunk + jnp.einsum('bhld,bhlv->bhdv', kc * zeta, vc)
        return s, o

    s_fin, o = lax.scan(step, s0, (to_chunks(q), to_chunks(k), to_chunks(v)))
    return o.transpose(1, 0, 3, 2, 4).reshape(B, N, H, Dv), s_fin


def mixer_even(h, w_in, w_conv, rpb, w_out, ctx_kv):
    B, S, _ = h.shape
    b_g, c_g, u, q, k, v = jnp.split(h @ w_in, 6, axis=-1)
    y_a = b_g * short_conv(c_g * u, w_conv)
    q, k, v = [t.reshape(B, S, NA_HEADS, HEAD_DIM) for t in (q, k, v)]
    if ctx_kv is None:
        y_b = context_attention(q, k, v)
    else:
        y_b = neighborhood_attention(q, k, v, ctx_kv[0], ctx_kv[1], rpb)
    y = jnp.concatenate([y_a, y_b.reshape(B, S, HALF)], axis=-1) @ w_out
    return y, k, v


def mixer_odd(h, w_in, decay_logit, w_out, s0):
    B, S, _ = h.shape
    u, q, k, v, g = jnp.split(h @ w_in, 5, axis=-1)
    y_c = fourier_mix(u)
    q, k, v = [t.reshape(B, S, RET_HEADS, HEAD_DIM) for t in (q, k, v)]
    k = k * ATT_SCALE
    lg = jax.nn.log_sigmoid(decay_logit.astype(jnp.float32))
    if s0 is None:
        s0 = jnp.zeros((B, 2, RET_HEADS, HEAD_DIM, HEAD_DIM), jnp.float32)
    s0 = s0.astype(jnp.float32)
    o_f, s_f = retention_chunkwise(q, k, v, lg[0], s0[:, 0])
    o_b, s_b = retention_chunkwise(jnp.flip(q, 1), jnp.flip(k, 1), jnp.flip(v, 1), lg[1], s0[:, 1])
    o = o_f + jnp.flip(o_b, 1)
    mu = jnp.mean(o, axis=-1, keepdims=True)
    var = jnp.mean(jnp.square(o - mu), axis=-1, keepdims=True)
    o = ((o - mu) * lax.rsqrt(var + EPS)).astype(h.dtype).reshape(B, S, HALF)
    y_d = jax.nn.silu(g) * o
    y = jnp.concatenate([y_c, y_d], axis=-1) @ w_out
    return y, jnp.stack([s_f, s_b], axis=1)


def moe_swiglu(x2d, router, w_g, w_u, w_d):
    T, D = x2d.shape
    logits = (x2d @ router).astype(jnp.float32)
    top_v, top_i = lax.top_k(logits, TOP_K)
    gates = jax.nn.softmax(top_v, axis=-1)
    A = T * TOP_K
    flat_e = top_i.reshape(A)
    flat_t = jnp.repeat(jnp.arange(T, dtype=jnp.int32), TOP_K)
    flat_g = gates.reshape(A)
    order = jnp.argsort(flat_e)
    se = flat_e[order]
    counts = jnp.bincount(flat_e, length=N_EXPERTS)
    padded = (counts + MOE_BLOCK - 1) // MOE_BLOCK * MOE_BLOCK
    ends = jnp.cumsum(padded)
    pos = (ends - padded)[se] + jnp.arange(A) - (jnp.cumsum(counts) - counts)[se]
    nb = -(-A // MOE_BLOCK) + N_EXPERTS
    P = nb * MOE_BLOCK
    slot_t = jnp.full((P,), T, jnp.int32).at[pos].set(flat_t[order])
    slot_g = jnp.zeros((P,), jnp.float32).at[pos].set(flat_g[order])
    block_e = jnp.minimum(jnp.searchsorted(ends, jnp.arange(nb) * MOE_BLOCK, side='right'), N_EXPERTS - 1)
    xs = jnp.concatenate([x2d, jnp.zeros((1, D), x2d.dtype)], axis=0)[slot_t].reshape(nb, MOE_BLOCK, D)

    def expert_block(args):
        xb, e = args
        return swiglu(xb, w_g[e], w_u[e], w_d[e])

    ys = lax.map(expert_block, (xs, block_e)).reshape(P, D)
    ys = ys * slot_g[:, None].astype(ys.dtype)
    return jax.ops.segment_sum(ys, slot_t, num_segments=T + 1)[:T]


def setup_inputs(seed: int = 0) -> dict:
    key = jax.random.key(seed)
    ks = jax.random.split(key, 32)
    f32 = jnp.float32
    D = D_MODEL

    def nrm(k, shape, scale=1.0):
        return jax.random.normal(k, shape, f32) * scale

    base_logit = jnp.asarray(np.log(2.0 ** (5 + np.arange(RET_HEADS)) - 1.0), f32)
    return {
        'x_prompt': nrm(ks[0], (BATCH, SEQ, D)),
        'x_sample': nrm(ks[1], (DEC_BATCH, DEC_SEQ, D)),
        'cache_na_k': nrm(ks[2], (DEC_BATCH, N_EVEN, PAST_LEN, NA_HEADS, HEAD_DIM)),
        'cache_na_v': nrm(ks[3], (DEC_BATCH, N_EVEN, PAST_LEN, NA_HEADS, HEAD_DIM)),
        'state_ret': nrm(ks[4], (DEC_BATCH, N_ODD, 2, RET_HEADS, HEAD_DIM, HEAD_DIM)),
        'c': nrm(ks[5], (DEC_BATCH, D)),
        'c_ctx': nrm(ks[6], (D,)),
        'w_ada': nrm(ks[7], (DEPTH, D, 6 * D), 0.5 * D ** -0.5),
        'b_ada': nrm(ks[8], (DEPTH, 6 * D), 0.02),
        'norm_g': 1.0 + nrm(ks[9], (DEPTH, 2, D), 0.02),
        'final_g': 1.0 + nrm(ks[10], (D,), 0.02),
        'w_in_e': nrm(ks[11], (N_EVEN, D, 6 * HALF), D ** -0.5),
        'w_conv_e': nrm(ks[12], (N_EVEN, HALF, 3), 3 ** -0.5),
        'na_rpb_e': nrm(ks[13], (N_EVEN, NA_HEADS, 2 * NA_WIN_R - 1, 2 * NA_WIN_C - 1), 0.1),
        'w_out_e': nrm(ks[14], (N_EVEN, D, D), D ** -0.5),
        'ff_gate_e': nrm(ks[15], (N_EVEN, D, D_FF), D ** -0.5),
        'ff_up_e': nrm(ks[16], (N_EVEN, D, D_FF), D ** -0.5),
        'ff_down_e': nrm(ks[17], (N_EVEN, D_FF, D), D_FF ** -0.5),
        'w_in_o': nrm(ks[18], (N_ODD, D, 5 * HALF), D ** -0.5),
        'ret_decay_o': base_logit + nrm(ks[19], (N_ODD, 2, RET_HEADS), 0.1),
        'w_out_o': nrm(ks[20], (N_ODD, D, D), D ** -0.5),
        'router_o': nrm(ks[21], (N_ODD, D, N_EXPERTS), D ** -0.5),
        'ex_gate_o': nrm(ks[22], (N_ODD, N_EXPERTS, D, D_FF_EXPERT), D ** -0.5),
        'ex_up_o': nrm(ks[23], (N_ODD, N_EXPERTS, D, D_FF_EXPERT), D ** -0.5),
        'ex_down_o': nrm(ks[24], (N_ODD, N_EXPERTS, D_FF_EXPERT, D), D_FF_EXPERT ** -0.5),
    }


def reference(x_prompt, x_sample, cache_na_k, cache_na_v, state_ret, c, c_ctx,
              w_ada, b_ada, norm_g, final_g,
              w_in_e, w_conv_e, na_rpb_e, w_out_e, ff_gate_e, ff_up_e, ff_down_e,
              w_in_o, ret_decay_o, w_out_o, router_o, ex_gate_o, ex_up_o, ex_down_o):

    def trunk(x, cond, cache):
        B, S, _ = x.shape
        new_k, new_v, new_s = [], [], []
        for l in range(DEPTH):
            i = l // 2
            mod = (jax.nn.silu(cond) @ w_ada[l] + b_ada[l])[:, None, :].astype(x.dtype)
            sh1, sc1, g1, sh2, sc2, g2 = jnp.split(mod, 6, axis=-1)
            h = rmsnorm(x, norm_g[l, 0]) * (1 + sc1) + sh1
            if l % 2 == 0:
                ctx_kv = None if cache is None else (cache[0][:, i], cache[1][:, i])
                y, k, v = mixer_even(h, w_in_e[i], w_conv_e[i], na_rpb_e[i], w_out_e[i], ctx_kv)
                new_k.append(k)
                new_v.append(v)
            else:
                s0 = None if cache is None else cache[2][:, i]
                y, s = mixer_odd(h, w_in_o[i], ret_decay_o[i], w_out_o[i], s0)
                new_s.append(s)
            x = x + g1 * y
            h = rmsnorm(x, norm_g[l, 1]) * (1 + sc2) + sh2
            if l % 2 == 0:
                f = swiglu(h, ff_gate_e[i], ff_up_e[i], ff_down_e[i])
            else:
                f = moe_swiglu(h.reshape(B * S, D_MODEL), router_o[i], ex_gate_o[i], ex_up_o[i],
                               ex_down_o[i]).reshape(B, S, D_MODEL)
            x = x + g2 * f
        return rmsnorm(x, final_g), new_k, new_v, new_s

    y_prompt, ks, vs, ss = trunk(x_prompt, c_ctx[None, :], None)
    y_sample = trunk(x_sample, c, (cache_na_k, cache_na_v, state_ret))[0]
    new_na_k = jnp.stack(ks, axis=1)
    new_na_v = jnp.stack(vs, axis=1)
    new_state_ret = jnp.stack(ss, axis=1)
    return (y_prompt, y_sample, new_na_k, new_na_v, new_state_ret)
```

```python
import functools

import numpy as np
import jax
import jax.numpy as jnp
from jax import lax
from jax.experimental import pallas as pl
from jax.experimental.pallas import tpu as pltpu

D_MODEL = 1024
BATCH = 32
SEQ = 256
DEC_BATCH = 4
DEC_SEQ = 4096
PAST_LEN = 256
GRID_W = 64
HEAD_DIM = 64
HALF = D_MODEL // 2
N_HEADS = HALF // HEAD_DIM
NA_WIN_R = 8
NA_WIN_C = 16
FT_GROUPS = 4
FT_GROUP_CH = HALF // FT_GROUPS
RET_CHUNK = 128
D_FF = 2816
N_EXPERTS = 8
D_FF_EXPERT = 3584
EPS = 1e-6
NEG_INF = -1e30
ATT_SCALE = HEAD_DIM ** -0.5

T_CTX = BATCH * SEQ
T_SMP = DEC_BATCH * DEC_SEQ
T_ALL = T_CTX + T_SMP
N_COND = 8
N_MOD = 6 * D_MODEL
GRID_H = DEC_SEQ // GRID_W
FT_N1 = 64

F32 = jnp.float32
BF16 = jnp.bfloat16
VMEM_LIMIT = 56 * 1024 * 1024

MOE_TM = 512
MOE_TF = 1792
MOE_A = 2 * T_ALL
MOE_NB = MOE_A // MOE_TM + N_EXPERTS
MOE_P = MOE_NB * MOE_TM
GATHER_TM = 256


def _cparams(*sem):
    return pltpu.CompilerParams(dimension_semantics=sem, vmem_limit_bytes=VMEM_LIMIT)


def _cond_index(tm):
    n_ctx = T_CTX // tm
    per_b = DEC_SEQ // tm

    def f(i):
        return jnp.where(i < n_ctx, 0, 1 + (i - n_ctx) // per_b)

    return f


def _silu(x):
    return x * (1.0 / (1.0 + jnp.exp(-x)))


def _norm_mod(x, g, scale, shift):
    y = x * lax.rsqrt(jnp.mean(x * x, axis=-1, keepdims=True) + EPS)
    return (y * g) * (1.0 + scale) + shift


def _split(a):
    hi = a.astype(BF16)
    lo = (a - hi.astype(F32)).astype(BF16)
    return hi, lo


def _dot(a, b):
    return jnp.dot(a, b, preferred_element_type=F32)


def _dot_nt(a, b):
    return lax.dot_general(a, b, (((1,), (1,)), ((), ())), preferred_element_type=F32)


def _dot3(a_hi, a_lo, b_hi, b_lo):
    return _dot(a_hi, b_hi) + (_dot(a_hi, b_lo) + _dot(a_lo, b_hi))


def _ada_kernel(c_ref, w_ref, b_ref, o_ref):
    s = _silu(c_ref[...]).astype(BF16)
    o_ref[...] = _dot(s, w_ref[...].astype(BF16)) + b_ref[...]


def _ada(cond, w_ada, b_ada):
    depth = w_ada.shape[0]
    tn = 1536
    return pl.pallas_call(
        _ada_kernel,
        grid=(depth, N_MOD // tn),
        in_specs=[
            pl.BlockSpec((N_COND, D_MODEL), lambda l, j: (0, 0)),
            pl.BlockSpec((None, D_MODEL, tn), lambda l, j: (l, 0, j)),
            pl.BlockSpec((None, 1, tn), lambda l, j: (l, 0, j)),
        ],
        out_specs=pl.BlockSpec((None, N_COND, tn), lambda l, j: (l, 0, j)),
        out_shape=jax.ShapeDtypeStruct((depth, N_COND, N_MOD), F32),
        compiler_params=_cparams("parallel", "parallel"),
        name="ada_mod",
    )(cond, w_ada, b_ada.reshape(depth, 1, N_MOD))


def _inproj_e_kernel(x_ref, mod_ref, g_ref, w_ref, bcu_ref, q_ref, k_ref, v_ref, kb_ref, vb_ref):
    m = mod_ref[...]
    h = _norm_mod(x_ref[...], g_ref[...], m[:, D_MODEL:2 * D_MODEL], m[:, 0:D_MODEL]).astype(BF16)
    bcu_ref[...] = _dot(h, w_ref[:, 0:3 * HALF])
    q_ref[...] = _dot(h, w_ref[:, 3 * HALF:4 * HALF]).astype(BF16)
    k = _dot(h, w_ref[:, 4 * HALF:5 * HALF])
    k_ref[...] = k
    kb_ref[...] = k.astype(BF16)
    v = _dot(h, w_ref[:, 5 * HALF:6 * HALF])
    v_ref[...] = v
    vb_ref[...] = v.astype(BF16)


def _in_proj_e(x, mod, g, w):
    tm = 512
    cidx = _cond_index(tm)
    row = lambda i: (i, 0)
    sds = jax.ShapeDtypeStruct
    return pl.pallas_call(
        _inproj_e_kernel,
        grid=(T_ALL // tm,),
        in_specs=[
            pl.BlockSpec((tm, D_MODEL), row),
            pl.BlockSpec((None, 1, N_MOD), lambda i: (cidx(i), 0, 0)),
            pl.BlockSpec((1, D_MODEL), lambda i: (0, 0)),
            pl.BlockSpec((D_MODEL, 6 * HALF), lambda i: (0, 0)),
        ],
        out_specs=[
            pl.BlockSpec((tm, 3 * HALF), row),
            pl.BlockSpec((tm, HALF), row),
            pl.BlockSpec((tm, HALF), row),
            pl.BlockSpec((tm, HALF), row),
            pl.BlockSpec((tm, HALF), row),
            pl.BlockSpec((tm, HALF), row),
        ],
        out_shape=[
            sds((T_ALL, 3 * HALF), F32),
            sds((T_ALL, HALF), BF16),
            sds((T_ALL, HALF), F32),
            sds((T_ALL, HALF), F32),
            sds((T_ALL, HALF), BF16),
            sds((T_ALL, HALF), BF16),
        ],
        compiler_params=_cparams("parallel"),
        name="in_proj_even",
    )(x, mod, g, w)


def _softmax_parts(parts):
    m = parts[0].max(axis=-1, keepdims=True)
    for s in parts[1:]:
        m = jnp.maximum(m, s.max(axis=-1, keepdims=True))
    es = [jnp.exp(s - m) for s in parts]
    l = es[0].sum(axis=-1, keepdims=True)
    for e in es[1:]:
        l = l + e.sum(axis=-1, keepdims=True)
    inv = 1.0 / l
    return [e * inv for e in es]


def _attn_ctx_kernel(q_ref, k_ref, v_ref, o_ref):
    for h in range(N_HEADS):
        sl = slice(h * HEAD_DIM, (h + 1) * HEAD_DIM)
        s = _dot_nt(q_ref[:, sl], k_ref[:, sl]) * ATT_SCALE
        (p,) = _softmax_parts([s])
        o_ref[:, sl] = _dot(p.astype(BF16), v_ref[:, sl]).astype(BF16)


def _attn_ctx(q, kb, vb):
    blk = pl.BlockSpec((SEQ, HALF), lambda b: (b, 0))
    return pl.pallas_call(
        _attn_ctx_kernel,
        grid=(BATCH,),
        in_specs=[blk, blk, blk],
        out_specs=blk,
        out_shape=jax.ShapeDtypeStruct((T_CTX, HALF), BF16),
        compiler_params=_cparams("parallel"),
        name="attn_context",
    )(q, kb, vb)


def _na_key_row0(r):
    return jnp.clip(r - NA_WIN_R // 2, 0, GRID_H - NA_WIN_R)


def _attn_na_kernel(q_ref, k_ref, v_ref, kc_ref, vc_ref, bias_ref, o_ref):
    r = pl.program_id(1)
    start = pl.multiple_of(_na_key_row0(r) * GRID_W, GRID_W)
    n_loc = NA_WIN_R * GRID_W
    for h in range(N_HEADS):
        sl = slice(h * HEAD_DIM, (h + 1) * HEAD_DIM)
        q = q_ref[:, sl]
        s_loc = _dot_nt(q, k_ref[pl.ds(start, n_loc), sl]) * ATT_SCALE + bias_ref[h]
        s_ctx = _dot_nt(q, kc_ref[:, sl].astype(BF16)) * ATT_SCALE
        p_loc, p_ctx = _softmax_parts([s_loc, s_ctx])
        o = _dot(p_loc.astype(BF16), v_ref[pl.ds(start, n_loc), sl])
        o = o + _dot(p_ctx.astype(BF16), vc_ref[:, sl].astype(BF16))
        o_ref[:, sl] = o.astype(BF16)


def _na_bias_table(rpb):
    cols = np.arange(GRID_W)
    col_start = np.clip(cols - NA_WIN_C // 2, 0, GRID_W - NA_WIN_C)
    col_in = (cols[None, :] >= col_start[:, None]) & (cols[None, :] < col_start[:, None] + NA_WIN_C)
    col_off = np.clip(cols[None, :] - cols[:, None], 1 - NA_WIN_C, NA_WIN_C - 1) + (NA_WIN_C - 1)
    row_off = np.arange(NA_WIN_R)[:, None] + np.arange(NA_WIN_R)[None, :]
    b = rpb.astype(F32)[:, row_off[:, :, None, None], col_off[None, None]]
    b = jnp.where(col_in[None, None, None], b, NEG_INF)
    return b.transpose(1, 0, 3, 2, 4).reshape(NA_WIN_R, N_HEADS, GRID_W, NA_WIN_R * GRID_W)


def _attn_na(q, kb, vb, cache_k, cache_v, bias):
    ctx_q_tiles = T_CTX // GRID_W
    ctx_b_tiles = T_CTX // DEC_SEQ

    def bias_idx(b, r):
        return (_na_key_row0(r) - r + (NA_WIN_R - 1), 0, 0, 0)

    full = pl.BlockSpec((DEC_SEQ, HALF), lambda b, r: (ctx_b_tiles + b, 0))
    cache = pl.BlockSpec((None, PAST_LEN, HALF), lambda b, r: (b, 0, 0))
    return pl.pallas_call(
        _attn_na_kernel,
        grid=(DEC_BATCH, GRID_H),
        in_specs=[
            pl.BlockSpec((GRID_W, HALF), lambda b, r: (ctx_q_tiles + b * GRID_H + r, 0)),
            full, full, cache, cache,
            pl.BlockSpec((None, N_HEADS, GRID_W, NA_WIN_R * GRID_W), bias_idx),
        ],
        out_specs=pl.BlockSpec((GRID_W, HALF), lambda b, r: (b * GRID_H + r, 0)),
        out_shape=jax.ShapeDtypeStruct((T_SMP, HALF), BF16),
        compiler_params=_cparams("parallel", "arbitrary"),
        name="attn_neighbourhood",
    )(q, kb, vb, cache_k, cache_v, bias)


def _seq_edges(i, tm):
    n_ctx = T_CTX // tm
    ctx_per = SEQ // tm
    smp_per = DEC_SEQ // tm
    j = i - n_ctx
    first = jnp.where(i < n_ctx, i % ctx_per == 0, j % smp_per == 0)
    last = jnp.where(i < n_ctx, i % ctx_per == ctx_per - 1, j % smp_per == smp_per - 1)
    return first, last


def _outproj_e_kernel(bcu_ref, prev_ref, next_ref, oc_ref, os_ref, x_ref, mod_ref, wc_ref, w_ref, o_ref, *, tm):
    i = pl.program_id(0)
    first, last = _seq_edges(i, tm)
    bcu = bcu_ref[...]
    cu = bcu[:, HALF:2 * HALF] * bcu[:, 2 * HALF:3 * HALF]
    pv = prev_ref[7:8, :]
    nx = next_ref[0:1, :]
    cu_prev_row = jnp.where(first, 0.0, pv[:, HALF:2 * HALF] * pv[:, 2 * HALF:3 * HALF])
    cu_next_row = jnp.where(last, 0.0, nx[:, HALF:2 * HALF] * nx[:, 2 * HALF:3 * HALF])
    rows = lax.broadcasted_iota(jnp.int32, (tm, HALF), 0)
    cu_prev = jnp.where(rows == 0, cu_prev_row, pltpu.roll(cu, 1, axis=0))
    cu_next = jnp.where(rows == tm - 1, cu_next_row, pltpu.roll(cu, tm - 1, axis=0))
    wc = wc_ref[...]
    y_a = bcu[:, 0:HALF] * (cu_prev * wc[0:1] + cu * wc[1:2] + cu_next * wc[2:3])
    y_b = jnp.where(i < T_CTX // tm, oc_ref[...], os_ref[...])
    y = _dot(y_a.astype(BF16), w_ref[0:HALF, :]) + _dot(y_b, w_ref[HALF:D_MODEL, :])
    o_ref[...] = x_ref[...] + mod_ref[:, 2 * D_MODEL:3 * D_MODEL] * y


def _out_proj_e(bcu, o_ctx, o_smp, x, mod, w_conv_t, w):
    tm = 256
    n_ctx = T_CTX // tm
    cidx = _cond_index(tm)
    row = lambda i: (i, 0)
    nb8 = T_ALL // 8
    return pl.pallas_call(
        functools.partial(_outproj_e_kernel, tm=tm),
        grid=(T_ALL // tm,),
        in_specs=[
            pl.BlockSpec((tm, 3 * HALF), row),
            pl.BlockSpec((8, 3 * HALF), lambda i: (jnp.maximum(i * (tm // 8) - 1, 0), 0)),
            pl.BlockSpec((8, 3 * HALF), lambda i: (jnp.minimum((i + 1) * (tm // 8), nb8 - 1), 0)),
            pl.BlockSpec((tm, HALF), lambda i: (jnp.minimum(i, n_ctx - 1), 0)),
            pl.BlockSpec((tm, HALF), lambda i: (jnp.maximum(i - n_ctx, 0), 0)),
            pl.BlockSpec((tm, D_MODEL), row),
            pl.BlockSpec((None, 1, N_MOD), lambda i: (cidx(i), 0, 0)),
            pl.BlockSpec((3, HALF), lambda i: (0, 0)),
            pl.BlockSpec((D_MODEL, D_MODEL), lambda i: (0, 0)),
        ],
        out_specs=pl.BlockSpec((tm, D_MODEL), row),
        out_shape=jax.ShapeDtypeStruct((T_ALL, D_MODEL), F32),
        compiler_params=_cparams("parallel"),
        name="out_proj_even",
    )(bcu, bcu, bcu, o_ctx, o_smp, x, mod, w_conv_t, w)


def _ffn_kernel(x_ref, mod_ref, g_ref, wg_ref, wu_ref, wd_ref, o_ref, h_ref, acc_ref):
    f = pl.program_id(1)

    @pl.when(f == 0)
    def _():
        m = mod_ref[...]
        h = _norm_mod(x_ref[...], g_ref[...], m[:, 4 * D_MODEL:5 * D_MODEL], m[:, 3 * D_MODEL:4 * D_MODEL])
        h_ref[...] = h.astype(BF16)
        acc_ref[...] = jnp.zeros_like(acc_ref)

    h = h_ref[...]
    a = _silu(_dot(h, wg_ref[...])) * _dot(h, wu_ref[...])
    acc_ref[...] += _dot(a.astype(BF16), wd_ref[...])

    @pl.when(f == pl.num_programs(1) - 1)
    def _():
        o_ref[...] = x_ref[...] + mod_ref[:, 5 * D_MODEL:6 * D_MODEL] * acc_ref[...]


def _ffn(x, mod, g, wg, wu, wd):
    tm = 512
    tf = D_FF // 2
    cidx = _cond_index(tm)
    return pl.pallas_call(
        _ffn_kernel,
        grid=(T_ALL // tm, D_FF // tf),
        in_specs=[
            pl.BlockSpec((tm, D_MODEL), lambda i, f: (i, 0)),
            pl.BlockSpec((None, 1, N_MOD), lambda i, f: (cidx(i), 0, 0)),
            pl.BlockSpec((1, D_MODEL), lambda i, f: (0, 0)),
            pl.BlockSpec((D_MODEL, tf), lambda i, f: (0, f)),
            pl.BlockSpec((D_MODEL, tf), lambda i, f: (0, f)),
            pl.BlockSpec((tf, D_MODEL), lambda i, f: (f, 0)),
        ],
        out_specs=pl.BlockSpec((tm, D_MODEL), lambda i, f: (i, 0)),
        out_shape=jax.ShapeDtypeStruct((T_ALL, D_MODEL), F32),
        scratch_shapes=[pltpu.VMEM((tm, D_MODEL), BF16), pltpu.VMEM((tm, D_MODEL), F32)],
        compiler_params=_cparams("parallel", "arbitrary"),
        name="ffn_dense",
    )(x, mod, g, wg, wu, wd)


def _inproj_o_kernel(x_ref, mod_ref, g_ref, w_ref, mch_ref, mcl_ref, ar_ref, ai_ref, q_ref, k_ref, v_ref, gt_ref):
    m = mod_ref[...]
    h = _norm_mod(x_ref[...], g_ref[...], m[:, D_MODEL:2 * D_MODEL], m[:, 0:D_MODEL]).astype(BF16)
    u_hi, u_lo = _split(_dot(h, w_ref[:, 0:HALF]))
    a = _dot3(u_hi, u_lo, mch_ref[...], mcl_ref[...])
    ar_ref[...] = a[:, 0:HALF]
    ai_ref[...] = a[:, HALF:2 * HALF]
    q_ref[...] = _dot(h, w_ref[:, HALF:2 * HALF]).astype(BF16)
    k_ref[...] = _dot(h, w_ref[:, 2 * HALF:3 * HALF]) * ATT_SCALE
    v_ref[...] = _dot(h, w_ref[:, 3 * HALF:4 * HALF]).astype(BF16)
    gt_ref[...] = _dot(h, w_ref[:, 4 * HALF:5 * HALF])


def _in_proj_o(x, mod, g, w, mc_hi, mc_lo):
    tm = 512
    cidx = _cond_index(tm)
    row = lambda i: (i, 0)
    half_out = pl.BlockSpec((tm, HALF), row)
    sds = jax.ShapeDtypeStruct
    return pl.pallas_call(
        _inproj_o_kernel,
        grid=(T_ALL // tm,),
        in_specs=[
            pl.BlockSpec((tm, D_MODEL), row),
            pl.BlockSpec((None, 1, N_MOD), lambda i: (cidx(i), 0, 0)),
            pl.BlockSpec((1, D_MODEL), lambda i: (0, 0)),
            pl.BlockSpec((D_MODEL, 5 * HALF), lambda i: (0, 0)),
            pl.BlockSpec((HALF, 2 * HALF), lambda i: (0, 0)),
            pl.BlockSpec((HALF, 2 * HALF), lambda i: (0, 0)),
        ],
        out_specs=[half_out] * 6,
        out_shape=[
            sds((T_ALL, HALF), F32),
            sds((T_ALL, HALF), F32),
            sds((T_ALL, HALF), BF16),
            sds((T_ALL, HALF), F32),
            sds((T_ALL, HALF), BF16),
            sds((T_ALL, HALF), F32),
        ],
        compiler_params=_cparams("parallel"),
        name="in_proj_odd",
    )(x, mod, g, w, mc_hi, mc_lo)


def _dft_cos_sin(n):
    k = np.arange(n)
    ang = 2.0 * np.pi * ((k[:, None] * k[None, :]) % n) / n
    return np.cos(ang), np.sin(ang)


def _hi_lo(a):
    a = jnp.asarray(a, F32)
    hi = a.astype(BF16)
    return hi, (a - hi.astype(F32)).astype(BF16)


def _channel_dft_table():
    c, s = _dft_cos_sin(FT_GROUP_CH)
    scale = FT_GROUP_CH ** -0.5
    eye = np.eye(FT_GROUPS)
    return np.concatenate([np.kron(eye, c * scale), np.kron(eye, -s * scale)], axis=1)


def _fourier_ctx_kernel(ar_ref, ai_ref, th_ref, tl_ref, o_ref):
    a_hi, a_lo = _split(jnp.concatenate([ar_ref[...], ai_ref[...]], axis=0))
    o_ref[...] = _dot3(th_ref[...], tl_ref[...], a_hi, a_lo).astype(BF16)


def _fourier_ctx(ar, ai):
    c, s = _dft_cos_sin(SEQ)
    t_hi, t_lo = _hi_lo(np.concatenate([c, s], axis=1) * SEQ ** -0.5)
    blk = pl.BlockSpec((SEQ, HALF), lambda b: (b, 0))
    tab = pl.BlockSpec((SEQ, 2 * SEQ), lambda b: (0, 0))
    return pl.pallas_call(
        _fourier_ctx_kernel,
        grid=(BATCH,),
        in_specs=[blk, blk, tab, tab],
        out_specs=blk,
        out_shape=jax.ShapeDtypeStruct((T_CTX, HALF), BF16),
        compiler_params=_cparams("parallel"),
        name="fourier_context",
    )(ar, ai, t_hi, t_lo)


def _fourier_s1_kernel(ar_ref, ai_ref, mh_ref, ml_ref, yr_ref, yi_ref):
    a_hi, a_lo = _split(jnp.concatenate([ar_ref[...], ai_ref[...]], axis=0))
    y = _dot3(mh_ref[...], ml_ref[...], a_hi, a_lo)
    yr_ref[...] = y[0:FT_N1]
    yi_ref[...] = y[FT_N1:2 * FT_N1]


def _fourier_s3_kernel(yr_ref, yi_ref, tc_ref, ts_ref, mh_ref, ml_ref, o_ref, *, nk):
    lanes = HALF // tc_ref.shape[-1]
    for j in range(nk):
        tc = jnp.concatenate([tc_ref[j]] * lanes, axis=-1)
        ts = jnp.concatenate([ts_ref[j]] * lanes, axis=-1)
        yr = yr_ref[j]
        yi = yi_ref[j]
        z = jnp.concatenate([yr * tc + yi * ts, yi * tc - yr * ts], axis=0)
        z_hi, z_lo = _split(z)
        o_ref[:, j * HALF:(j + 1) * HALF] = _dot3(mh_ref[...], ml_ref[...], z_hi, z_lo).astype(BF16)


def _fourier_smp(ar, ai):
    n1 = FT_N1
    wide = n1 * HALF
    c, s = _dft_cos_sin(n1)
    m1_hi, m1_lo = _hi_lo(np.block([[c, s], [-s, c]]))
    m3_hi, m3_lo = _hi_lo(np.concatenate([c, s], axis=1) / n1)
    kk = np.arange(n1)
    ang = 2.0 * np.pi * (kk[:, None] * kk[None, :]) / DEC_SEQ
    tw_c = jnp.broadcast_to(jnp.asarray(np.cos(ang), F32)[:, :, None], (n1, n1, 128))
    tw_s = jnp.broadcast_to(jnp.asarray(np.sin(ang), F32)[:, :, None], (n1, n1, 128))

    ncol = 4096
    ctx_tiles = T_CTX // n1 // n1
    a_blk = pl.BlockSpec((n1, ncol), lambda b, j: (ctx_tiles + b, j))
    y_blk = pl.BlockSpec((n1, ncol), lambda b, j: (b, j))
    m1_blk = pl.BlockSpec((2 * n1, 2 * n1), lambda b, j: (0, 0))
    y_sds = jax.ShapeDtypeStruct((DEC_BATCH * n1, wide), F32)
    yr, yi = pl.pallas_call(
        _fourier_s1_kernel,
        grid=(DEC_BATCH, wide // ncol),
        in_specs=[a_blk, a_blk, m1_blk, m1_blk],
        out_specs=[y_blk, y_blk],
        out_shape=[y_sds, y_sds],
        compiler_params=_cparams("parallel", "parallel"),
        name="fourier_latent_stage1",
    )(ar.reshape(T_ALL // n1, wide), ai.reshape(T_ALL // n1, wide), m1_hi, m1_lo)

    nk = 8
    z_blk = pl.BlockSpec((nk, n1, HALF), lambda b, k: (b * (n1 // nk) + k, 0, 0))
    tw_blk = pl.BlockSpec((nk, n1, 128), lambda b, k: (k, 0, 0))
    m3_blk = pl.BlockSpec((n1, 2 * n1), lambda b, k: (0, 0))
    out = pl.pallas_call(
        functools.partial(_fourier_s3_kernel, nk=nk),
        grid=(DEC_BATCH, n1 // nk),
        in_specs=[z_blk, z_blk, tw_blk, tw_blk, m3_blk, m3_blk],
        out_specs=pl.BlockSpec((n1, nk * HALF), lambda b, k: (b, k)),
        out_shape=jax.ShapeDtypeStruct((DEC_BATCH * n1, wide), BF16),
        compiler_params=_cparams("parallel", "parallel"),
        name="fourier_latent_stage3",
    )(yr.reshape(DEC_BATCH * n1, n1, HALF), yi.reshape(DEC_BATCH * n1, n1, HALF), tw_c, tw_s, m3_hi, m3_lo)
    return out.reshape(T_SMP, HALF)


def _retention_kernel(lg_ref, q_ref, k_ref, v_ref, g_ref, *rest, seq, has_s0):
    if has_s0:
        s0_ref, y_ref, of_ref = rest
        st_ref = None
    else:
        y_ref, st_ref, of_ref = rest
        s0_ref = None
    L = RET_CHUNK
    nc = seq // L
    hp = pl.program_id(1)
    diff = (lax.broadcasted_iota(jnp.int32, (L, L), 0) - lax.broadcasted_iota(jnp.int32, (L, L), 1)).astype(F32)
    li = lax.broadcasted_iota(jnp.int32, (L, 1), 0).astype(F32)
    one = jnp.ones((1, 1), F32)

    for hh in range(2):
        sl = slice(hh * HEAD_DIM, (hh + 1) * HEAD_DIM)
        lgf = lg_ref[0, hp * 2 + hh]
        lgb = lg_ref[1, hp * 2 + hh]
        dec_f = jnp.where(diff >= 0, jnp.exp(lgf * jnp.maximum(diff, 0.0)), 0.0)
        dec_b = jnp.where(diff <= 0, jnp.exp(lgb * jnp.maximum(-diff, 0.0)), 0.0)
        xi_f = jnp.exp(lgf * (li + 1.0))
        zeta_f = jnp.exp(lgf * (L - 1.0 - li))
        xi_b = jnp.exp(lgb * (L - li))
        zeta_b = jnp.exp(lgb * li)
        gc_f = jnp.exp(one * (lgf * L))
        gc_b = jnp.exp(one * (lgb * L))

        def chunk(c, s, dec, xi, zeta, gc):
            r0 = pl.multiple_of(c * L, L)
            qc = q_ref[pl.ds(r0, L), sl]
            kc = k_ref[pl.ds(r0, L), sl]
            vc = v_ref[pl.ds(r0, L), sl]
            inner = _dot_nt(qc, kc.astype(BF16)) * dec
            o = _dot(inner.astype(BF16), vc) + _dot(qc, s.astype(BF16)) * xi
            s = s * gc + _dot((kc * zeta).T.astype(BF16), vc)
            return r0, o, s

        def fwd(c, s):
            r0, o, s = chunk(c, s, dec_f, xi_f, zeta_f, gc_f)
            of_ref[pl.ds(r0, L), sl] = o
            return s

        def bwd(i, s):
            r0, o, s = chunk(nc - 1 - i, s, dec_b, xi_b, zeta_b, gc_b)
            o = o + of_ref[pl.ds(r0, L), sl]
            mu = jnp.mean(o, axis=-1, keepdims=True)
            var = jnp.mean(jnp.square(o - mu), axis=-1, keepdims=True)
            on = (o - mu) * lax.rsqrt(var + EPS)
            y_ref[pl.ds(r0, L), sl] = (_silu(g_ref[pl.ds(r0, L), sl]) * on).astype(BF16)
            return s

        if has_s0:
            s_f0 = s0_ref[0, hh]
            s_b0 = s0_ref[1, hh]
        else:
            s_f0 = jnp.zeros((HEAD_DIM, HEAD_DIM), F32)
            s_b0 = s_f0
        s_f = lax.fori_loop(0, nc, fwd, s_f0)
        s_b = lax.fori_loop(0, nc, bwd, s_b0)
        if st_ref is not None:
            st_ref[0, hh] = s_f
            st_ref[1, hh] = s_b


def _retention(lg, q, k, v, g, s0, *, seq, nbatch, row0):
    has_s0 = s0 is not None
    tile0 = row0 // seq
    blk = pl.BlockSpec((seq, 2 * HEAD_DIM), lambda b, hp: (tile0 + b, hp))
    st_blk = pl.BlockSpec((None, 2, 2, HEAD_DIM, HEAD_DIM), lambda b, hp: (b, 0, hp, 0, 0))
    in_specs = [pl.BlockSpec(memory_space=pltpu.SMEM), blk, blk, blk, blk]
    args = [lg, q, k, v, g]
    y_spec = pl.BlockSpec((seq, 2 * HEAD_DIM), lambda b, hp: (b, hp))
    y_sds = jax.ShapeDtypeStruct((nbatch * seq, HALF), BF16)
    if has_s0:
        in_specs.append(st_blk)
        args.append(s0)
        out_specs, out_shape = y_spec, y_sds
    else:
        out_specs = [y_spec, st_blk]
        out_shape = [y_sds, jax.ShapeDtypeStruct((nbatch, 2, N_HEADS, HEAD_DIM, HEAD_DIM), F32)]
    return pl.pallas_call(
        functools.partial(_retention_kernel, seq=seq, has_s0=has_s0),
        grid=(nbatch, N_HEADS // 2),
        in_specs=in_specs,
        out_specs=out_specs,
        out_shape=out_shape,
        scratch_shapes=[pltpu.VMEM((seq, 2 * HEAD_DIM), F32)],
        compiler_params=_cparams("parallel", "parallel"),
        name="retention_%d" % seq,
    )(*args)


def _outproj_o_kernel(ycc_ref, ycs_ref, ydc_ref, yds_ref, x_ref, mod_ref, w_ref, g_ref, rh_ref, rl_ref,
                      x3_ref, h_ref, e1_ref, e2_ref, ga_ref, gb_ref, *, tm):
    is_ctx = pl.program_id(0) < T_CTX // tm
    y_c = jnp.where(is_ctx, ycc_ref[...], ycs_ref[...])
    y_d = jnp.where(is_ctx, ydc_ref[...], yds_ref[...])
    y = _dot(y_c, w_ref[0:HALF, :]) + _dot(y_d, w_ref[HALF:D_MODEL, :])
    m = mod_ref[...]
    x3 = x_ref[...] + m[:, 2 * D_MODEL:3 * D_MODEL] * y
    x3_ref[...] = x3
    h = _norm_mod(x3, g_ref[...], m[:, 4 * D_MODEL:5 * D_MODEL], m[:, 3 * D_MODEL:4 * D_MODEL])
    h_ref[...] = h
    h_hi, h_lo = _split(h)
    logits = _dot3(h_hi, h_lo, rh_ref[...], rl_ref[...])
    idx = lax.broadcasted_iota(jnp.int32, logits.shape, 1).astype(F32)
    logits = jnp.where(idx < float(N_EXPERTS), logits, -jnp.inf)
    m1 = logits.max(axis=-1, keepdims=True)
    e1 = jnp.where(logits == m1, idx, float(N_EXPERTS)).min(axis=-1, keepdims=True)
    rest = jnp.where(idx == e1, -jnp.inf, logits)
    m2 = rest.max(axis=-1, keepdims=True)
    e2 = jnp.where(rest == m2, idx, float(N_EXPERTS)).min(axis=-1, keepdims=True)
    ex = jnp.exp(m2 - m1)
    den = 1.0 + ex
    wide = (tm, 128)
    e1_ref[...] = jnp.broadcast_to(e1, wide).astype(jnp.int32)
    e2_ref[...] = jnp.broadcast_to(e2, wide).astype(jnp.int32)
    ga_ref[...] = jnp.broadcast_to(1.0 / den, wide)
    gb_ref[...] = jnp.broadcast_to(ex / den, wide)


def _out_proj_o(yc_ctx, yc_smp, yd_ctx, yd_smp, x, mod, w, g, r_hi, r_lo):
    tm = 256
    n_ctx = T_CTX // tm
    cidx = _cond_index(tm)
    row = lambda i: (i, 0)
    ctx_blk = pl.BlockSpec((tm, HALF), lambda i: (jnp.minimum(i, n_ctx - 1), 0))
    smp_blk = pl.BlockSpec((tm, HALF), lambda i: (jnp.maximum(i - n_ctx, 0), 0))
    sds = jax.ShapeDtypeStruct
    rep = pl.BlockSpec((tm, 128), row)
    return pl.pallas_call(
        functools.partial(_outproj_o_kernel, tm=tm),
        grid=(T_ALL // tm,),
        in_specs=[
            ctx_blk, smp_blk, ctx_blk, smp_blk,
            pl.BlockSpec((tm, D_MODEL), row),
            pl.BlockSpec((None, 1, N_MOD), lambda i: (cidx(i), 0, 0)),
            pl.BlockSpec((D_MODEL, D_MODEL), lambda i: (0, 0)),
            pl.BlockSpec((1, D_MODEL), lambda i: (0, 0)),
            pl.BlockSpec((D_MODEL, 128), lambda i: (0, 0)),
            pl.BlockSpec((D_MODEL, 128), lambda i: (0, 0)),
        ],
        out_specs=[pl.BlockSpec((tm, D_MODEL), row), pl.BlockSpec((tm, D_MODEL), row), rep, rep, rep, rep],
        out_shape=[
            sds((T_ALL, D_MODEL), F32), sds((T_ALL, D_MODEL), F32),
            sds((T_ALL, 128), jnp.int32), sds((T_ALL, 128), jnp.int32),
            sds((T_ALL, 128), F32), sds((T_ALL, 128), F32),
        ],
        compiler_params=_cparams("parallel"),
        name="out_proj_odd_route",
    )(yc_ctx, yc_smp, yd_ctx, yd_smp, x, mod, w, g, r_hi, r_lo)


def _routing_tables(e1, e2):
    flat_e = jnp.concatenate([e1, e2])
    onehot = (flat_e[:, None] == jnp.arange(N_EXPERTS, dtype=jnp.int32)[None, :]).astype(jnp.int32)
    csum = jnp.cumsum(onehot, axis=0)
    rank = jnp.sum(csum * onehot, axis=1) - 1
    counts = csum[-1]
    padded = (counts + MOE_TM - 1) // MOE_TM * MOE_TM
    ends = jnp.cumsum(padded)
    pos = (ends - padded)[flat_e] + rank
    tok = jnp.tile(jnp.arange(T_ALL, dtype=jnp.int32), 2)
    slot_t = jnp.zeros((MOE_P,), jnp.int32).at[pos].set(tok)
    block_e = jnp.minimum(
        jnp.searchsorted(ends, jnp.arange(MOE_NB, dtype=jnp.int32) * MOE_TM, side="right"), N_EXPERTS - 1
    ).astype(jnp.int32)
    n_valid = (ends[-1] // MOE_TM).astype(jnp.int32).reshape(1)
    return slot_t, pos[:T_ALL].astype(jnp.int32), pos[T_ALL:].astype(jnp.int32), block_e, n_valid


def _row_copy(src_ref, dst_ref, sem, src_row, dst_row):
    return pltpu.make_async_copy(src_ref.at[pl.ds(src_row, 1), :], dst_ref.at[pl.ds(dst_row, 1), :], sem)


def _gather_kernel(idx_ref, src_ref, o_ref, sem, *, tm):
    def start(r, c):
        _row_copy(src_ref, o_ref, sem, idx_ref[0, r], r).start()
        return c

    def wait(r, c):
        _row_copy(src_ref, o_ref, sem, 0, r).wait()
        return c

    lax.fori_loop(0, tm, start, 0)
    lax.fori_loop(0, tm, wait, 0)


def _gather_rows(src, slot_t):
    tm = GATHER_TM
    nblk = MOE_P // tm
    return pl.pallas_call(
        functools.partial(_gather_kernel, tm=tm),
        grid=(nblk,),
        in_specs=[
            pl.BlockSpec((None, 1, tm), lambda i: (i, 0, 0), memory_space=pltpu.SMEM),
            pl.BlockSpec(memory_space=pl.ANY),
        ],
        out_specs=pl.BlockSpec((tm, D_MODEL), lambda i: (i, 0)),
        out_shape=jax.ShapeDtypeStruct((MOE_P, D_MODEL), F32),
        scratch_shapes=[pltpu.SemaphoreType.DMA(())],
        compiler_params=_cparams("arbitrary"),
        name="moe_gather",
    )(slot_t.reshape(nblk, 1, tm), src)


def _experts_kernel(be_ref, nv_ref, x_ref, wg_ref, wu_ref, wd_ref, o_ref, xb_ref, acc_ref):
    i = pl.program_id(0)
    f = pl.program_id(1)
    valid = i < nv_ref[0]

    @pl.when(jnp.logical_and(valid, f == 0))
    def _():
        xb_ref[...] = x_ref[...].astype(BF16)
        acc_ref[...] = jnp.zeros_like(acc_ref)

    @pl.when(valid)
    def _():
        x = xb_ref[...]
        a = _silu(_dot(x, wg_ref[...])) * _dot(x, wu_ref[...])
        acc_ref[...] += _dot(a.astype(BF16), wd_ref[...])

    last = f == pl.num_programs(1) - 1

    @pl.when(jnp.logical_and(valid, last))
    def _():
        o_ref[...] = acc_ref[...]

    @pl.when(jnp.logical_and(jnp.logical_not(valid), last))
    def _():
        o_ref[...] = jnp.zeros_like(o_ref)


def _experts(xs, block_e, n_valid, wg, wu, wd):
    nf = D_FF_EXPERT // MOE_TF

    def f_eff(i, f, nv):
        return jnp.where(i < nv[0], f, nf - 1)

    return pl.pallas_call(
        _experts_kernel,
        grid_spec=pltpu.PrefetchScalarGridSpec(
            num_scalar_prefetch=2,
            grid=(MOE_NB, nf),
            in_specs=[
                pl.BlockSpec((MOE_TM, D_MODEL), lambda i, f, be, nv: (i, 0)),
                pl.BlockSpec((None, D_MODEL, MOE_TF), lambda i, f, be, nv: (be[i], 0, f_eff(i, f, nv))),
                pl.BlockSpec((None, D_MODEL, MOE_TF), lambda i, f, be, nv: (be[i], 0, f_eff(i, f, nv))),
                pl.BlockSpec((None, MOE_TF, D_MODEL), lambda i, f, be, nv: (be[i], f_eff(i, f, nv), 0)),
            ],
            out_specs=pl.BlockSpec((MOE_TM, D_MODEL), lambda i, f, be, nv: (i, 0)),
            scratch_shapes=[pltpu.VMEM((MOE_TM, D_MODEL), BF16), pltpu.VMEM((MOE_TM, D_MODEL), F32)],
        ),
        out_shape=jax.ShapeDtypeStruct((MOE_P, D_MODEL), F32),
        compiler_params=_cparams("arbitrary", "arbitrary"),
        name="moe_experts",
    )(block_e, n_valid, xs, wg, wu, wd)


def _combine_kernel(p1_ref, p2_ref, ys_ref, x_ref, ga_ref, gb_ref, mod_ref, g_ref, o_ref, a_ref, b_ref, sems, *, tm):
    def start(r, c):
        _row_copy(ys_ref, a_ref, sems.at[0], p1_ref[0, r], r).start()
        _row_copy(ys_ref, b_ref, sems.at[1], p2_ref[0, r], r).start()
        return c

    def wait(r, c):
        _row_copy(ys_ref, a_ref, sems.at[0], 0, r).wait()
        _row_copy(ys_ref, b_ref, sems.at[1], 0, r).wait()
        return c

    lax.fori_loop(0, tm, start, 0)
    lax.fori_loop(0, tm, wait, 0)
    reps = D_MODEL // 128
    ga = jnp.concatenate([ga_ref[...]] * reps, axis=-1)
    gb = jnp.concatenate([gb_ref[...]] * reps, axis=-1)
    f = a_ref[...] * ga + b_ref[...] * gb
    x = x_ref[...] + mod_ref[:, 5 * D_MODEL:6 * D_MODEL] * f
    o_ref[...] = (x * lax.rsqrt(jnp.mean(x * x, axis=-1, keepdims=True) + EPS)) * g_ref[...]


def _combine(ys, pos1, pos2, x, ga, gb, mod, final_g):
    tm = GATHER_TM
    nblk = T_ALL // tm
    cidx = _cond_index(tm)
    row = lambda i: (i, 0)
    idx_blk = pl.BlockSpec((None, 1, tm), lambda i: (i, 0, 0), memory_space=pltpu.SMEM)
    return pl.pallas_call(
        functools.partial(_combine_kernel, tm=tm),
        grid=(nblk,),
        in_specs=[
            idx_blk, idx_blk,
            pl.BlockSpec(memory_space=pl.ANY),
            pl.BlockSpec((tm, D_MODEL), row),
            pl.BlockSpec((tm, 128), row),
            pl.BlockSpec((tm, 128), row),
            pl.BlockSpec((None, 1, N_MOD), lambda i: (cidx(i), 0, 0)),
            pl.BlockSpec((1, D_MODEL), lambda i: (0, 0)),
        ],
        out_specs=pl.BlockSpec((tm, D_MODEL), row),
        out_shape=jax.ShapeDtypeStruct((T_ALL, D_MODEL), F32),
        scratch_shapes=[
            pltpu.VMEM((tm, D_MODEL), F32),
            pltpu.VMEM((tm, D_MODEL), F32),
            pltpu.SemaphoreType.DMA((2,)),
        ],
        compiler_params=_cparams("arbitrary"),
        name="moe_combine_final_norm",
    )(pos1.reshape(nblk, 1, tm), pos2.reshape(nblk, 1, tm), ys, x, ga, gb, mod, final_g)


def kernel(x_prompt, x_sample, cache_na_k, cache_na_v, state_ret, c, c_ctx, w_ada, b_ada, norm_g, final_g, w_in_e, w_conv_e, na_rpb_e, w_out_e, ff_gate_e, ff_up_e, ff_down_e, w_in_o, ret_decay_o, w_out_o, router_o, ex_gate_o, ex_up_o, ex_down_o):
    x0 = jnp.concatenate([x_prompt.reshape(T_CTX, D_MODEL), x_sample.reshape(T_SMP, D_MODEL)], axis=0)
    cond = jnp.concatenate([c_ctx[None, :], c, jnp.zeros((N_COND - 1 - DEC_BATCH, D_MODEL), F32)], axis=0)
    mod = _ada(cond, w_ada, b_ada)
    mod0 = mod[0].reshape(N_COND, 1, N_MOD)
    mod1 = mod[1].reshape(N_COND, 1, N_MOD)

    bcu, q, k, v, kb, vb = _in_proj_e(x0, mod0, norm_g[0, 0][None, :], w_in_e[0].astype(BF16))
    o_ctx = _attn_ctx(q, kb, vb)
    o_smp = _attn_na(q, kb, vb,
                     cache_na_k[:, 0].reshape(DEC_BATCH, PAST_LEN, HALF),
                     cache_na_v[:, 0].reshape(DEC_BATCH, PAST_LEN, HALF),
                     _na_bias_table(na_rpb_e[0]))
    x1 = _out_proj_e(bcu, o_ctx, o_smp, x0, mod0, w_conv_e[0].T, w_out_e[0].astype(BF16))
    x2 = _ffn(x1, mod0, norm_g[0, 1][None, :], ff_gate_e[0].astype(BF16), ff_up_e[0].astype(BF16),
              ff_down_e[0].astype(BF16))

    mc_hi, mc_lo = _hi_lo(_channel_dft_table())
    ar, ai, q1, k1, v1, g1 = _in_proj_o(x2, mod1, norm_g[1, 0][None, :], w_in_o[0].astype(BF16), mc_hi, mc_lo)
    yc_ctx = _fourier_ctx(ar, ai)
    yc_smp = _fourier_smp(ar, ai)
    lg = jax.nn.log_sigmoid(ret_decay_o[0].astype(F32))
    yd_ctx, new_state = _retention(lg, q1, k1, v1, g1, None, seq=SEQ, nbatch=BATCH, row0=0)
    yd_smp = _retention(lg, q1, k1, v1, g1, state_ret[:, 0], seq=DEC_SEQ, nbatch=DEC_BATCH, row0=T_CTX)
    r_hi, r_lo = _hi_lo(jnp.pad(router_o[0], ((0, 0), (0, 128 - N_EXPERTS))))
    x3, hm, e1, e2, ga, gb = _out_proj_o(yc_ctx, yc_smp, yd_ctx, yd_smp, x2, mod1, w_out_o[0].astype(BF16),
                                         norm_g[1, 1][None, :], r_hi, r_lo)
    slot_t, pos1, pos2, block_e, n_valid = _routing_tables(e1[:, 0], e2[:, 0])
    xs = _gather_rows(hm, slot_t)
    ys = _experts(xs, block_e, n_valid, ex_gate_o[0].astype(BF16), ex_up_o[0].astype(BF16),
                  ex_down_o[0].astype(BF16))
    y = _combine(ys, pos1, pos2, x3, ga, gb, mod1, final_g[None, :])

    y_prompt = y[:T_CTX].reshape(BATCH, SEQ, D_MODEL)
    y_sample = y[T_CTX:].reshape(DEC_BATCH, DEC_SEQ, D_MODEL)
    new_na_k = k[:T_CTX].reshape(BATCH, 1, SEQ, N_HEADS, HEAD_DIM)
    new_na_v = v[:T_CTX].reshape(BATCH, 1, SEQ, N_HEADS, HEAD_DIM)
    new_state_ret = new_state.reshape(BATCH, 1, 2, N_HEADS, HEAD_DIM, HEAD_DIM)
    return (y_prompt, y_sample, new_na_k, new_na_v, new_state_ret)
```

```python
import functools

import numpy as np
import jax
import jax.numpy as jnp
from jax import lax
from jax.experimental import pallas as pl
from jax.experimental.pallas import tpu as pltpu

D_MODEL = 1024
BATCH = 32
SEQ = 256
DEC_BATCH = 4
DEC_SEQ = 4096
PAST_LEN = 256
GRID_W = 64
HEAD_DIM = 64
HALF = D_MODEL // 2
N_HEADS = HALF // HEAD_DIM
NA_WIN_R = 8
NA_WIN_C = 16
FT_GROUPS = 4
FT_GROUP_CH = HALF // FT_GROUPS
RET_CHUNK = 128
D_FF = 2816
N_EXPERTS = 8
D_FF_EXPERT = 3584
EPS = 1e-6
NEG_INF = -1e30
ATT_SCALE = HEAD_DIM ** -0.5

T_CTX = BATCH * SEQ
T_SMP = DEC_BATCH * DEC_SEQ
T_ALL = T_CTX + T_SMP
N_COND = 8
N_MOD = 6 * D_MODEL
GRID_H = DEC_SEQ // GRID_W
FT_N1 = 64

F32 = jnp.float32
BF16 = jnp.bfloat16
VMEM_LIMIT = 56 * 1024 * 1024

MOE_TM = 512
MOE_TF = 1792
MOE_A = 2 * T_ALL
MOE_NB = MOE_A // MOE_TM + N_EXPERTS
MOE_P = MOE_NB * MOE_TM
GATHER_TM = 256


def _cparams(*sem):
    return pltpu.CompilerParams(dimension_semantics=sem, vmem_limit_bytes=VMEM_LIMIT)


def _cond_index(tm):
    n_ctx = T_CTX // tm
    per_b = DEC_SEQ // tm

    def f(i):
        return jnp.where(i < n_ctx, 0, 1 + (i - n_ctx) // per_b)

    return f


def _silu(x):
    return x * (1.0 / (1.0 + jnp.exp(-x)))


def _norm_mod(x, g, scale, shift):
    y = x * lax.rsqrt(jnp.mean(x * x, axis=-1, keepdims=True) + EPS)
    return (y * g) * (1.0 + scale) + shift


def _split(a):
    hi = a.astype(BF16)
    lo = (a - hi.astype(F32)).astype(BF16)
    return hi, lo


def _dot(a, b):
    return jnp.dot(a, b, preferred_element_type=F32)


def _dot_nt(a, b):
    return lax.dot_general(a, b, (((1,), (1,)), ((), ())), preferred_element_type=F32)


def _dot3(a_hi, a_lo, b_hi, b_lo):
    return _dot(a_hi, b_hi) + (_dot(a_hi, b_lo) + _dot(a_lo, b_hi))


def _ada_kernel(c_ref, w_ref, b_ref, o_ref):
    s = _silu(c_ref[...]).astype(BF16)
    o_ref[...] = _dot(s, w_ref[...].astype(BF16)) + b_ref[...]


def _ada(cond, w_ada, b_ada):
    depth = w_ada.shape[0]
    tn = 1536
    return pl.pallas_call(
        _ada_kernel,
        grid=(depth, N_MOD // tn),
        in_specs=[
            pl.BlockSpec((N_COND, D_MODEL), lambda l, j: (0, 0)),
            pl.BlockSpec((None, D_MODEL, tn), lambda l, j: (l, 0, j)),
            pl.BlockSpec((None, 1, tn), lambda l, j: (l, 0, j)),
        ],
        out_specs=pl.BlockSpec((None, N_COND, tn), lambda l, j: (l, 0, j)),
        out_shape=jax.ShapeDtypeStruct((depth, N_COND, N_MOD), F32),
        compiler_params=_cparams("parallel", "parallel"),
        name="ada_mod",
    )(cond, w_ada, b_ada.reshape(depth, 1, N_MOD))


def _inproj_e_kernel(x_ref, mod_ref, g_ref, w_ref, bcu_ref, q_ref, k_ref, v_ref, kb_ref, vb_ref):
    m = mod_ref[...]
    h = _norm_mod(x_ref[...], g_ref[...], m[:, D_MODEL:2 * D_MODEL], m[:, 0:D_MODEL]).astype(BF16)
    bcu_ref[...] = _dot(h, w_ref[:, 0:3 * HALF])
    q_ref[...] = _dot(h, w_ref[:, 3 * HALF:4 * HALF]).astype(BF16)
    k = _dot(h, w_ref[:, 4 * HALF:5 * HALF])
    k_ref[...] = k
    kb_ref[...] = k.astype(BF16)
    v = _dot(h, w_ref[:, 5 * HALF:6 * HALF])
    v_ref[...] = v
    vb_ref[...] = v.astype(BF16)


def _in_proj_e(x, mod, g, w):
    tm = 512
    cidx = _cond_index(tm)
    row = lambda i: (i, 0)
    sds = jax.ShapeDtypeStruct
    return pl.pallas_call(
        _inproj_e_kernel,
        grid=(T_ALL // tm,),
        in_specs=[
            pl.BlockSpec((tm, D_MODEL), row),
            pl.BlockSpec((None, 1, N_MOD), lambda i: (cidx(i), 0, 0)),
            pl.BlockSpec((1, D_MODEL), lambda i: (0, 0)),
            pl.BlockSpec((D_MODEL, 6 * HALF), lambda i: (0, 0)),
        ],
        out_specs=[
            pl.BlockSpec((tm, 3 * HALF), row),
            pl.BlockSpec((tm, HALF), row),
            pl.BlockSpec((tm, HALF), row),
            pl.BlockSpec((tm, HALF), row),
            pl.BlockSpec((tm, HALF), row),
            pl.BlockSpec((tm, HALF), row),
        ],
        out_shape=[
            sds((T_ALL, 3 * HALF), F32),
            sds((T_ALL, HALF), BF16),
            sds((T_ALL, HALF), F32),
            sds((T_ALL, HALF), F32),
            sds((T_ALL, HALF), BF16),
            sds((T_ALL, HALF), BF16),
        ],
        compiler_params=_cparams("parallel"),
        name="in_proj_even",
    )(x, mod, g, w)


def _softmax_parts(parts):
    m = parts[0].max(axis=-1, keepdims=True)
    for s in parts[1:]:
        m = jnp.maximum(m, s.max(axis=-1, keepdims=True))
    es = [jnp.exp(s - m) for s in parts]
    l = es[0].sum(axis=-1, keepdims=True)
    for e in es[1:]:
        l = l + e.sum(axis=-1, keepdims=True)
    inv = 1.0 / l
    return [e * inv for e in es]


def _attn_ctx_kernel(q_ref, k_ref, v_ref, o_ref):
    for h in range(N_HEADS):
        sl = slice(h * HEAD_DIM, (h + 1) * HEAD_DIM)
        s = _dot_nt(q_ref[:, sl], k_ref[:, sl]) * ATT_SCALE
        (p,) = _softmax_parts([s])
        o_ref[:, sl] = _dot(p.astype(BF16), v_ref[:, sl]).astype(BF16)


def _attn_ctx(q, kb, vb):
    blk = pl.BlockSpec((SEQ, HALF), lambda b: (b, 0))
    return pl.pallas_call(
        _attn_ctx_kernel,
        grid=(BATCH,),
        in_specs=[blk, blk, blk],
        out_specs=blk,
        out_shape=jax.ShapeDtypeStruct((T_CTX, HALF), BF16),
        compiler_params=_cparams("parallel"),
        name="attn_context",
    )(q, kb, vb)


NA_Q_ROWS = 4
NA_K_ROWS = NA_WIN_R + NA_Q_ROWS


def _na_key_row0(rb):
    return jnp.clip(rb * NA_Q_ROWS - NA_WIN_R // 2, 0, GRID_H - NA_K_ROWS)


def _attn_na_kernel(q_ref, k_ref, v_ref, kc_ref, vc_ref, bias_ref, o_ref):
    rb = pl.program_id(1)
    start = pl.multiple_of(_na_key_row0(rb) * GRID_W, GRID_W)
    n_loc = NA_K_ROWS * GRID_W
    for h in range(N_HEADS):
        sl = slice(h * HEAD_DIM, (h + 1) * HEAD_DIM)
        q = q_ref[:, sl]
        s_loc = _dot_nt(q, k_ref[pl.ds(start, n_loc), sl]) * ATT_SCALE + bias_ref[h]
        s_ctx = _dot_nt(q, kc_ref[:, sl].astype(BF16)) * ATT_SCALE
        p_loc, p_ctx = _softmax_parts([s_loc, s_ctx])
        o = _dot(p_loc.astype(BF16), v_ref[pl.ds(start, n_loc), sl])
        o = o + _dot(p_ctx.astype(BF16), vc_ref[:, sl].astype(BF16))
        o_ref[:, sl] = o.astype(BF16)


def _na_bias_table(rpb):
    cols = np.arange(GRID_W)
    col_start = np.clip(cols - NA_WIN_C // 2, 0, GRID_W - NA_WIN_C)
    col_in = (cols[None, :] >= col_start[:, None]) & (cols[None, :] < col_start[:, None] + NA_WIN_C)
    col_off = np.clip(cols[None, :] - cols[:, None], 1 - NA_WIN_C, NA_WIN_C - 1) + (NA_WIN_C - 1)
    rpb = rpb.astype(F32)
    n_blocks = GRID_H // NA_Q_ROWS
    variants = []
    for rb in (0, 1, n_blocks - 1):
        q_row = rb * NA_Q_ROWS + np.arange(NA_Q_ROWS)
        k_row = int(np.clip(rb * NA_Q_ROWS - NA_WIN_R // 2, 0, GRID_H - NA_K_ROWS)) + np.arange(NA_K_ROWS)
        win0 = np.clip(q_row - NA_WIN_R // 2, 0, GRID_H - NA_WIN_R)
        row_in = (k_row[None, :] >= win0[:, None]) & (k_row[None, :] < win0[:, None] + NA_WIN_R)
        row_off = np.clip(k_row[None, :] - q_row[:, None] + (NA_WIN_R - 1), 0, 2 * NA_WIN_R - 2)
        b = rpb[:, row_off[:, :, None, None], col_off[None, None]]
        keep = row_in[:, :, None, None] & col_in[None, None]
        b = jnp.where(keep[None], b, NEG_INF)
        variants.append(b.transpose(0, 1, 3, 2, 4).reshape(N_HEADS, NA_Q_ROWS * GRID_W, NA_K_ROWS * GRID_W))
    return jnp.stack(variants)


def _attn_na(q, kb, vb, cache_k, cache_v, bias):
    tq = NA_Q_ROWS * GRID_W
    n_blocks = GRID_H // NA_Q_ROWS
    ctx_q_tiles = T_CTX // tq
    ctx_b_tiles = T_CTX // DEC_SEQ

    def bias_idx(b, rb):
        return (jnp.where(rb == 0, 0, jnp.where(rb == n_blocks - 1, 2, 1)), 0, 0, 0)

    full = pl.BlockSpec((DEC_SEQ, HALF), lambda b, r: (ctx_b_tiles + b, 0))
    cache = pl.BlockSpec((None, PAST_LEN, HALF), lambda b, r: (b, 0, 0))
    return pl.pallas_call(
        _attn_na_kernel,
        grid=(DEC_BATCH, n_blocks),
        in_specs=[
            pl.BlockSpec((tq, HALF), lambda b, r: (ctx_q_tiles + b * n_blocks + r, 0)),
            full, full, cache, cache,
            pl.BlockSpec((None, N_HEADS, tq, NA_K_ROWS * GRID_W), bias_idx),
        ],
        out_specs=pl.BlockSpec((tq, HALF), lambda b, r: (b * n_blocks + r, 0)),
        out_shape=jax.ShapeDtypeStruct((T_SMP, HALF), BF16),
        compiler_params=_cparams("parallel", "arbitrary"),
        name="attn_neighbourhood",
    )(q, kb, vb, cache_k, cache_v, bias)


def _seq_edges(i, tm):
    n_ctx = T_CTX // tm
    ctx_per = SEQ // tm
    smp_per = DEC_SEQ // tm
    j = i - n_ctx
    first = jnp.where(i < n_ctx, i % ctx_per == 0, j % smp_per == 0)
    last = jnp.where(i < n_ctx, i % ctx_per == ctx_per - 1, j % smp_per == smp_per - 1)
    return first, last


def _outproj_e_kernel(bcu_ref, prev_ref, next_ref, oc_ref, os_ref, x_ref, mod_ref, wc_ref, w_ref, o_ref, *, tm):
    i = pl.program_id(0)
    first, last = _seq_edges(i, tm)
    bcu = bcu_ref[...]
    cu = bcu[:, HALF:2 * HALF] * bcu[:, 2 * HALF:3 * HALF]
    pv = prev_ref[7:8, :]
    nx = next_ref[0:1, :]
    cu_prev_row = jnp.where(first, 0.0, pv[:, HALF:2 * HALF] * pv[:, 2 * HALF:3 * HALF])
    cu_next_row = jnp.where(last, 0.0, nx[:, HALF:2 * HALF] * nx[:, 2 * HALF:3 * HALF])
    rows = lax.broadcasted_iota(jnp.int32, (tm, HALF), 0)
    cu_prev = jnp.where(rows == 0, cu_prev_row, pltpu.roll(cu, 1, axis=0))
    cu_next = jnp.where(rows == tm - 1, cu_next_row, pltpu.roll(cu, tm - 1, axis=0))
    wc = wc_ref[...]
    y_a = bcu[:, 0:HALF] * (cu_prev * wc[0:1] + cu * wc[1:2] + cu_next * wc[2:3])
    y_b = jnp.where(i < T_CTX // tm, oc_ref[...], os_ref[...])
    y = _dot(y_a.astype(BF16), w_ref[0:HALF, :]) + _dot(y_b, w_ref[HALF:D_MODEL, :])
    o_ref[...] = x_ref[...] + mod_ref[:, 2 * D_MODEL:3 * D_MODEL] * y


def _out_proj_e(bcu, o_ctx, o_smp, x, mod, w_conv_t, w):
    tm = 256
    n_ctx = T_CTX // tm
    cidx = _cond_index(tm)
    row = lambda i: (i, 0)
    nb8 = T_ALL // 8
    return pl.pallas_call(
        functools.partial(_outproj_e_kernel, tm=tm),
        grid=(T_ALL // tm,),
        in_specs=[
            pl.BlockSpec((tm, 3 * HALF), row),
            pl.BlockSpec((8, 3 * HALF), lambda i: (jnp.maximum(i * (tm // 8) - 1, 0), 0)),
            pl.BlockSpec((8, 3 * HALF), lambda i: (jnp.minimum((i + 1) * (tm // 8), nb8 - 1), 0)),
            pl.BlockSpec((tm, HALF), lambda i: (jnp.minimum(i, n_ctx - 1), 0)),
            pl.BlockSpec((tm, HALF), lambda i: (jnp.maximum(i - n_ctx, 0), 0)),
            pl.BlockSpec((tm, D_MODEL), row),
            pl.BlockSpec((None, 1, N_MOD), lambda i: (cidx(i), 0, 0)),
            pl.BlockSpec((3, HALF), lambda i: (0, 0)),
            pl.BlockSpec((D_MODEL, D_MODEL), lambda i: (0, 0)),
        ],
        out_specs=pl.BlockSpec((tm, D_MODEL), row),
        out_shape=jax.ShapeDtypeStruct((T_ALL, D_MODEL), F32),
        compiler_params=_cparams("parallel"),
        name="out_proj_even",
    )(bcu, bcu, bcu, o_ctx, o_smp, x, mod, w_conv_t, w)


def _ffn_kernel(x_ref, mod_ref, g_ref, wg_ref, wu_ref, wd_ref, o_ref, h_ref, acc_ref):
    f = pl.program_id(1)

    @pl.when(f == 0)
    def _():
        m = mod_ref[...]
        h = _norm_mod(x_ref[...], g_ref[...], m[:, 4 * D_MODEL:5 * D_MODEL], m[:, 3 * D_MODEL:4 * D_MODEL])
        h_ref[...] = h.astype(BF16)
        acc_ref[...] = jnp.zeros_like(acc_ref)

    h = h_ref[...]
    a = _silu(_dot(h, wg_ref[...])) * _dot(h, wu_ref[...])
    acc_ref[...] += _dot(a.astype(BF16), wd_ref[...])

    @pl.when(f == pl.num_programs(1) - 1)
    def _():
        o_ref[...] = x_ref[...] + mod_ref[:, 5 * D_MODEL:6 * D_MODEL] * acc_ref[...]


def _ffn(x, mod, g, wg, wu, wd):
    tm = 512
    tf = D_FF // 2
    cidx = _cond_index(tm)
    return pl.pallas_call(
        _ffn_kernel,
        grid=(T_ALL // tm, D_FF // tf),
        in_specs=[
            pl.BlockSpec((tm, D_MODEL), lambda i, f: (i, 0)),
            pl.BlockSpec((None, 1, N_MOD), lambda i, f: (cidx(i), 0, 0)),
            pl.BlockSpec((1, D_MODEL), lambda i, f: (0, 0)),
            pl.BlockSpec((D_MODEL, tf), lambda i, f: (0, f)),
            pl.BlockSpec((D_MODEL, tf), lambda i, f: (0, f)),
            pl.BlockSpec((tf, D_MODEL), lambda i, f: (f, 0)),
        ],
        out_specs=pl.BlockSpec((tm, D_MODEL), lambda i, f: (i, 0)),
        out_shape=jax.ShapeDtypeStruct((T_ALL, D_MODEL), F32),
        scratch_shapes=[pltpu.VMEM((tm, D_MODEL), BF16), pltpu.VMEM((tm, D_MODEL), F32)],
        compiler_params=_cparams("parallel", "arbitrary"),
        name="ffn_dense",
    )(x, mod, g, wg, wu, wd)


def _inproj_o_kernel(x_ref, mod_ref, g_ref, w_ref, mch_ref, mcl_ref, ar_ref, ai_ref, q_ref, k_ref, v_ref, gt_ref):
    m = mod_ref[...]
    h = _norm_mod(x_ref[...], g_ref[...], m[:, D_MODEL:2 * D_MODEL], m[:, 0:D_MODEL]).astype(BF16)
    u_hi, u_lo = _split(_dot(h, w_ref[:, 0:HALF]))
    a = _dot3(u_hi, u_lo, mch_ref[...], mcl_ref[...])
    ar_ref[...] = a[:, 0:HALF]
    ai_ref[...] = a[:, HALF:2 * HALF]
    q_ref[...] = _dot(h, w_ref[:, HALF:2 * HALF]).astype(BF16)
    k_ref[...] = _dot(h, w_ref[:, 2 * HALF:3 * HALF]) * ATT_SCALE
    v_ref[...] = _dot(h, w_ref[:, 3 * HALF:4 * HALF]).astype(BF16)
    gt_ref[...] = _dot(h, w_ref[:, 4 * HALF:5 * HALF])


def _in_proj_o(x, mod, g, w, mc_hi, mc_lo):
    tm = 512
    cidx = _cond_index(tm)
    row = lambda i: (i, 0)
    half_out = pl.BlockSpec((tm, HALF), row)
    sds = jax.ShapeDtypeStruct
    return pl.pallas_call(
        _inproj_o_kernel,
        grid=(T_ALL // tm,),
        in_specs=[
            pl.BlockSpec((tm, D_MODEL), row),
            pl.BlockSpec((None, 1, N_MOD), lambda i: (cidx(i), 0, 0)),
            pl.BlockSpec((1, D_MODEL), lambda i: (0, 0)),
            pl.BlockSpec((D_MODEL, 5 * HALF), lambda i: (0, 0)),
            pl.BlockSpec((HALF, 2 * HALF), lambda i: (0, 0)),
            pl.BlockSpec((HALF, 2 * HALF), lambda i: (0, 0)),
        ],
        out_specs=[half_out] * 6,
        out_shape=[
            sds((T_ALL, HALF), F32),
            sds((T_ALL, HALF), F32),
            sds((T_ALL, HALF), BF16),
            sds((T_ALL, HALF), F32),
            sds((T_ALL, HALF), BF16),
            sds((T_ALL, HALF), F32),
        ],
        compiler_params=_cparams("parallel"),
        name="in_proj_odd",
    )(x, mod, g, w, mc_hi, mc_lo)


def _dft_cos_sin(n):
    k = np.arange(n)
    ang = 2.0 * np.pi * ((k[:, None] * k[None, :]) % n) / n
    return np.cos(ang), np.sin(ang)


def _hi_lo(a):
    a = jnp.asarray(a, F32)
    hi = a.astype(BF16)
    return hi, (a - hi.astype(F32)).astype(BF16)


def _channel_dft_table():
    c, s = _dft_cos_sin(FT_GROUP_CH)
    scale = FT_GROUP_CH ** -0.5
    eye = np.eye(FT_GROUPS)
    return np.concatenate([np.kron(eye, c * scale), np.kron(eye, -s * scale)], axis=1)


def _fourier_ctx_kernel(ar_ref, ai_ref, th_ref, tl_ref, o_ref):
    a_hi, a_lo = _split(jnp.concatenate([ar_ref[...], ai_ref[...]], axis=0))
    o_ref[...] = _dot3(th_ref[...], tl_ref[...], a_hi, a_lo).astype(BF16)


def _fourier_ctx(ar, ai):
    c, s = _dft_cos_sin(SEQ)
    t_hi, t_lo = _hi_lo(np.concatenate([c, s], axis=1) * SEQ ** -0.5)
    blk = pl.BlockSpec((SEQ, HALF), lambda b: (b, 0))
    tab = pl.BlockSpec((SEQ, 2 * SEQ), lambda b: (0, 0))
    return pl.pallas_call(
        _fourier_ctx_kernel,
        grid=(BATCH,),
        in_specs=[blk, blk, tab, tab],
        out_specs=blk,
        out_shape=jax.ShapeDtypeStruct((T_CTX, HALF), BF16),
        compiler_params=_cparams("parallel"),
        name="fourier_context",
    )(ar, ai, t_hi, t_lo)


def _fourier_s1_kernel(ar_ref, ai_ref, mh_ref, ml_ref, yr_ref, yi_ref):
    a_hi, a_lo = _split(jnp.concatenate([ar_ref[...], ai_ref[...]], axis=0))
    y = _dot3(mh_ref[...], ml_ref[...], a_hi, a_lo)
    yr_ref[...] = y[0:FT_N1]
    yi_ref[...] = y[FT_N1:2 * FT_N1]


def _fourier_s3_kernel(yr_ref, yi_ref, tc_ref, ts_ref, mh_ref, ml_ref, o_ref, *, nk):
    lanes = HALF // tc_ref.shape[-1]
    for j in range(nk):
        tc = jnp.concatenate([tc_ref[j]] * lanes, axis=-1)
        ts = jnp.concatenate([ts_ref[j]] * lanes, axis=-1)
        yr = yr_ref[j]
        yi = yi_ref[j]
        z = jnp.concatenate([yr * tc + yi * ts, yi * tc - yr * ts], axis=0)
        z_hi, z_lo = _split(z)
        o_ref[:, j * HALF:(j + 1) * HALF] = _dot3(mh_ref[...], ml_ref[...], z_hi, z_lo).astype(BF16)


def _fourier_smp(ar, ai):
    n1 = FT_N1
    wide = n1 * HALF
    c, s = _dft_cos_sin(n1)
    m1_hi, m1_lo = _hi_lo(np.block([[c, s], [-s, c]]))
    m3_hi, m3_lo = _hi_lo(np.concatenate([c, s], axis=1) / n1)
    kk = np.arange(n1)
    ang = 2.0 * np.pi * (kk[:, None] * kk[None, :]) / DEC_SEQ
    tw_c = jnp.broadcast_to(jnp.asarray(np.cos(ang), F32)[:, :, None], (n1, n1, 128))
    tw_s = jnp.broadcast_to(jnp.asarray(np.sin(ang), F32)[:, :, None], (n1, n1, 128))

    ncol = 4096
    ctx_tiles = T_CTX // n1 // n1
    a_blk = pl.BlockSpec((n1, ncol), lambda b, j: (ctx_tiles + b, j))
    y_blk = pl.BlockSpec((n1, ncol), lambda b, j: (b, j))
    m1_blk = pl.BlockSpec((2 * n1, 2 * n1), lambda b, j: (0, 0))
    y_sds = jax.ShapeDtypeStruct((DEC_BATCH * n1, wide), F32)
    yr, yi = pl.pallas_call(
        _fourier_s1_kernel,
        grid=(DEC_BATCH, wide // ncol),
        in_specs=[a_blk, a_blk, m1_blk, m1_blk],
        out_specs=[y_blk, y_blk],
        out_shape=[y_sds, y_sds],
        compiler_params=_cparams("parallel", "parallel"),
        name="fourier_latent_stage1",
    )(ar.reshape(T_ALL // n1, wide), ai.reshape(T_ALL // n1, wide), m1_hi, m1_lo)

    nk = 8
    z_blk = pl.BlockSpec((nk, n1, HALF), lambda b, k: (b * (n1 // nk) + k, 0, 0))
    tw_blk = pl.BlockSpec((nk, n1, 128), lambda b, k: (k, 0, 0))
    m3_blk = pl.BlockSpec((n1, 2 * n1), lambda b, k: (0, 0))
    out = pl.pallas_call(
        functools.partial(_fourier_s3_kernel, nk=nk),
        grid=(DEC_BATCH, n1 // nk),
        in_specs=[z_blk, z_blk, tw_blk, tw_blk, m3_blk, m3_blk],
        out_specs=pl.BlockSpec((n1, nk * HALF), lambda b, k: (b, k)),
        out_shape=jax.ShapeDtypeStruct((DEC_BATCH * n1, wide), BF16),
        compiler_params=_cparams("parallel", "parallel"),
        name="fourier_latent_stage3",
    )(yr.reshape(DEC_BATCH * n1, n1, HALF), yi.reshape(DEC_BATCH * n1, n1, HALF), tw_c, tw_s, m3_hi, m3_lo)
    return out.reshape(T_SMP, HALF)


def _retention_kernel(lg_ref, q_ref, k_ref, v_ref, g_ref, *rest, seq, nh, has_s0):
    if has_s0:
        s0_ref, y_ref, of_ref, ob_ref, dec_ref, xi_ref, zeta_ref = rest
        st_ref = None
    else:
        y_ref, st_ref, of_ref, ob_ref, dec_ref, xi_ref, zeta_ref = rest
        s0_ref = None
    L = RET_CHUNK
    nc = seq // L
    hb = pl.program_id(1)
    diff = (lax.broadcasted_iota(jnp.int32, (L, L), 0) - lax.broadcasted_iota(jnp.int32, (L, L), 1)).astype(F32)
    li = lax.broadcasted_iota(jnp.int32, (L, HEAD_DIM), 0).astype(F32)
    one = jnp.ones((1, 1), F32)

    gcs = []
    for hh in range(nh):
        lgf = lg_ref[0, hb * nh + hh]
        lgb = lg_ref[1, hb * nh + hh]
        dec_ref[2 * hh] = jnp.where(diff >= 0, jnp.exp(lgf * jnp.maximum(diff, 0.0)), 0.0)
        dec_ref[2 * hh + 1] = jnp.where(diff <= 0, jnp.exp(lgb * jnp.maximum(-diff, 0.0)), 0.0)
        xi_ref[2 * hh] = jnp.exp(lgf * (li + 1.0))
        xi_ref[2 * hh + 1] = jnp.exp(lgb * (L - li))
        zeta_ref[2 * hh] = jnp.exp(lgf * (L - 1.0 - li))
        zeta_ref[2 * hh + 1] = jnp.exp(lgb * li)
        gcs.append(jnp.exp(one * (lgf * L)))
        gcs.append(jnp.exp(one * (lgb * L)))

    def chunk(c, s, t, sl, o_ref):
        r0 = pl.multiple_of(c * L, L)
        qc = q_ref[pl.ds(r0, L), sl]
        kc = k_ref[pl.ds(r0, L), sl]
        vc = v_ref[pl.ds(r0, L), sl]
        inner = _dot_nt(qc, kc.astype(BF16)) * dec_ref[t]
        o_ref[pl.ds(r0, L), sl] = _dot(inner.astype(BF16), vc) + _dot(qc, s.astype(BF16)) * xi_ref[t]
        return s * gcs[t] + _dot((kc * zeta_ref[t]).T.astype(BF16), vc)

    def scan_step(i, states):
        out = []
        for hh in range(nh):
            sl = slice(hh * HEAD_DIM, (hh + 1) * HEAD_DIM)
            out.append(chunk(i, states[2 * hh], 2 * hh, sl, of_ref))
            out.append(chunk(nc - 1 - i, states[2 * hh + 1], 2 * hh + 1, sl, ob_ref))
        return tuple(out)

    if has_s0:
        init = tuple(s0_ref[t % 2, t // 2] for t in range(2 * nh))
    else:
        init = tuple(jnp.zeros((HEAD_DIM, HEAD_DIM), F32) for _ in range(2 * nh))
    final = lax.fori_loop(0, nc, scan_step, init)
    if st_ref is not None:
        for t in range(2 * nh):
            st_ref[t % 2, t // 2] = final[t]

    def finish(c, carry):
        r0 = pl.multiple_of(c * L, L)
        o_all = of_ref[pl.ds(r0, L), :] + ob_ref[pl.ds(r0, L), :]
        gate = _silu(g_ref[pl.ds(r0, L), :])
        for hh in range(nh):
            sl = slice(hh * HEAD_DIM, (hh + 1) * HEAD_DIM)
            o = o_all[:, sl]
            mu = jnp.mean(o, axis=-1, keepdims=True)
            var = jnp.mean(jnp.square(o - mu), axis=-1, keepdims=True)
            y_ref[pl.ds(r0, L), sl] = (gate[:, sl] * ((o - mu) * lax.rsqrt(var + EPS))).astype(BF16)
        return carry

    lax.fori_loop(0, nc, finish, 0)


def _retention(lg, q, k, v, g, s0, *, seq, nbatch, row0, nh):
    has_s0 = s0 is not None
    tile0 = row0 // seq
    width = nh * HEAD_DIM
    blk = pl.BlockSpec((seq, width), lambda b, hb: (tile0 + b, hb))
    st_blk = pl.BlockSpec((None, 2, nh, HEAD_DIM, HEAD_DIM), lambda b, hb: (b, 0, hb, 0, 0))
    in_specs = [pl.BlockSpec(memory_space=pltpu.SMEM), blk, blk, blk, blk]
    args = [lg, q, k, v, g]
    y_spec = pl.BlockSpec((seq, width), lambda b, hb: (b, hb))
    y_sds = jax.ShapeDtypeStruct((nbatch * seq, HALF), BF16)
    if has_s0:
        in_specs.append(st_blk)
        args.append(s0)
        out_specs, out_shape = y_spec, y_sds
    else:
        out_specs = [y_spec, st_blk]
        out_shape = [y_sds, jax.ShapeDtypeStruct((nbatch, 2, N_HEADS, HEAD_DIM, HEAD_DIM), F32)]
    return pl.pallas_call(
        functools.partial(_retention_kernel, seq=seq, nh=nh, has_s0=has_s0),
        grid=(nbatch, N_HEADS // nh),
        in_specs=in_specs,
        out_specs=out_specs,
        out_shape=out_shape,
        scratch_shapes=[
            pltpu.VMEM((seq, width), F32),
            pltpu.VMEM((seq, width), F32),
            pltpu.VMEM((2 * nh, RET_CHUNK, RET_CHUNK), F32),
            pltpu.VMEM((2 * nh, RET_CHUNK, HEAD_DIM), F32),
            pltpu.VMEM((2 * nh, RET_CHUNK, HEAD_DIM), F32),
        ],
        compiler_params=_cparams("parallel", "parallel"),
        name="retention_%d" % seq,
    )(*args)


def _outproj_o_kernel(ycc_ref, ycs_ref, ydc_ref, yds_ref, x_ref, mod_ref, w_ref, g_ref, rh_ref, rl_ref,
                      x3_ref, h_ref, e1_ref, e2_ref, ga_ref, gb_ref, *, tm):
    is_ctx = pl.program_id(0) < T_CTX // tm
    y_c = jnp.where(is_ctx, ycc_ref[...], ycs_ref[...])
    y_d = jnp.where(is_ctx, ydc_ref[...], yds_ref[...])
    y = _dot(y_c, w_ref[0:HALF, :]) + _dot(y_d, w_ref[HALF:D_MODEL, :])
    m = mod_ref[...]
    x3 = x_ref[...] + m[:, 2 * D_MODEL:3 * D_MODEL] * y
    x3_ref[...] = x3
    h = _norm_mod(x3, g_ref[...], m[:, 4 * D_MODEL:5 * D_MODEL], m[:, 3 * D_MODEL:4 * D_MODEL])
    h_ref[...] = h
    h_hi, h_lo = _split(h)
    logits = _dot3(h_hi, h_lo, rh_ref[...], rl_ref[...])
    idx = lax.broadcasted_iota(jnp.int32, logits.shape, 1).astype(F32)
    logits = jnp.where(idx < float(N_EXPERTS), logits, -jnp.inf)
    m1 = logits.max(axis=-1, keepdims=True)
    e1 = jnp.where(logits == m1, idx, float(N_EXPERTS)).min(axis=-1, keepdims=True)
    rest = jnp.where(idx == e1, -jnp.inf, logits)
    m2 = rest.max(axis=-1, keepdims=True)
    e2 = jnp.where(rest == m2, idx, float(N_EXPERTS)).min(axis=-1, keepdims=True)
    ex = jnp.exp(m2 - m1)
    den = 1.0 + ex
    wide = (tm, 128)
    e1_ref[...] = jnp.broadcast_to(e1, wide).astype(jnp.int32)
    e2_ref[...] = jnp.broadcast_to(e2, wide).astype(jnp.int32)
    ga_ref[...] = jnp.broadcast_to(1.0 / den, wide)
    gb_ref[...] = jnp.broadcast_to(ex / den, wide)


def _out_proj_o(yc_ctx, yc_smp, yd_ctx, yd_smp, x, mod, w, g, r_hi, r_lo):
    tm = 256
    n_ctx = T_CTX // tm
    cidx = _cond_index(tm)
    row = lambda i: (i, 0)
    ctx_blk = pl.BlockSpec((tm, HALF), lambda i: (jnp.minimum(i, n_ctx - 1), 0))
    smp_blk = pl.BlockSpec((tm, HALF), lambda i: (jnp.maximum(i - n_ctx, 0), 0))
    sds = jax.ShapeDtypeStruct
    rep = pl.BlockSpec((tm, 128), row)
    return pl.pallas_call(
        functools.partial(_outproj_o_kernel, tm=tm),
        grid=(T_ALL // tm,),
        in_specs=[
            ctx_blk, smp_blk, ctx_blk, smp_blk,
            pl.BlockSpec((tm, D_MODEL), row),
            pl.BlockSpec((None, 1, N_MOD), lambda i: (cidx(i), 0, 0)),
            pl.BlockSpec((D_MODEL, D_MODEL), lambda i: (0, 0)),
            pl.BlockSpec((1, D_MODEL), lambda i: (0, 0)),
            pl.BlockSpec((D_MODEL, 128), lambda i: (0, 0)),
            pl.BlockSpec((D_MODEL, 128), lambda i: (0, 0)),
        ],
        out_specs=[pl.BlockSpec((tm, D_MODEL), row), pl.BlockSpec((tm, D_MODEL), row), rep, rep, rep, rep],
        out_shape=[
            sds((T_ALL, D_MODEL), F32), sds((T_ALL, D_MODEL), F32),
            sds((T_ALL, 128), jnp.int32), sds((T_ALL, 128), jnp.int32),
            sds((T_ALL, 128), F32), sds((T_ALL, 128), F32),
        ],
        compiler_params=_cparams("parallel"),
        name="out_proj_odd_route",
    )(yc_ctx, yc_smp, yd_ctx, yd_smp, x, mod, w, g, r_hi, r_lo)


def _routing_tables(e1, e2):
    flat_e = jnp.concatenate([e1, e2])
    onehot = (flat_e[:, None] == jnp.arange(N_EXPERTS, dtype=jnp.int32)[None, :]).astype(jnp.int32)
    csum = jnp.cumsum(onehot, axis=0)
    counts = csum[-1]
    padded = (counts + MOE_TM - 1) // MOE_TM * MOE_TM
    ends = jnp.cumsum(padded)
    pos = jnp.sum(onehot * (csum - 1 + (ends - padded)[None, :]), axis=1)
    tok = jnp.tile(jnp.arange(T_ALL, dtype=jnp.int32), 2)
    slot_t = jnp.zeros((MOE_P,), jnp.int32).at[pos].set(tok)
    block_row0 = jnp.arange(MOE_NB, dtype=jnp.int32) * MOE_TM
    block_e = jnp.minimum(
        jnp.sum((ends[None, :] <= block_row0[:, None]).astype(jnp.int32), axis=1), N_EXPERTS - 1
    ).astype(jnp.int32)
    n_valid = (ends[-1] // MOE_TM).astype(jnp.int32).reshape(1)
    return slot_t, pos[:T_ALL].astype(jnp.int32), pos[T_ALL:].astype(jnp.int32), block_e, n_valid


def _row_copy(src_ref, dst_ref, sem, src_row, dst_row):
    return pltpu.make_async_copy(src_ref.at[pl.ds(src_row, 1), :], dst_ref.at[pl.ds(dst_row, 1), :], sem)


def _gather_kernel(idx_ref, src_ref, o_ref, sem, *, tm):
    def start(r, c):
        _row_copy(src_ref, o_ref, sem, idx_ref[0, r], r).start()
        return c

    def wait(r, c):
        _row_copy(src_ref, o_ref, sem, 0, r).wait()
        return c

    lax.fori_loop(0, tm, start, 0, unroll=8)
    lax.fori_loop(0, tm, wait, 0, unroll=8)


def _gather_rows(src, slot_t):
    tm = GATHER_TM
    nblk = MOE_P // tm
    return pl.pallas_call(
        functools.partial(_gather_kernel, tm=tm),
        grid=(nblk,),
        in_specs=[
            pl.BlockSpec((None, 1, tm), lambda i: (i, 0, 0), memory_space=pltpu.SMEM),
            pl.BlockSpec(memory_space=pl.ANY),
        ],
        out_specs=pl.BlockSpec((tm, D_MODEL), lambda i: (i, 0)),
        out_shape=jax.ShapeDtypeStruct((MOE_P, D_MODEL), F32),
        scratch_shapes=[pltpu.SemaphoreType.DMA(())],
        compiler_params=_cparams("arbitrary"),
        name="moe_gather",
    )(slot_t.reshape(nblk, 1, tm), src)


def _experts_kernel(be_ref, nv_ref, x_ref, wg_ref, wu_ref, wd_ref, o_ref, xb_ref, acc_ref):
    i = pl.program_id(0)
    f = pl.program_id(1)
    valid = i < nv_ref[0]

    @pl.when(jnp.logical_and(valid, f == 0))
    def _():
        xb_ref[...] = x_ref[...].astype(BF16)
        acc_ref[...] = jnp.zeros_like(acc_ref)

    @pl.when(valid)
    def _():
        x = xb_ref[...]
        a = _silu(_dot(x, wg_ref[...])) * _dot(x, wu_ref[...])
        acc_ref[...] += _dot(a.astype(BF16), wd_ref[...])

    last = f == pl.num_programs(1) - 1

    @pl.when(jnp.logical_and(valid, last))
    def _():
        o_ref[...] = acc_ref[...]

    @pl.when(jnp.logical_and(jnp.logical_not(valid), last))
    def _():
        o_ref[...] = jnp.zeros_like(o_ref)


def _experts(xs, block_e, n_valid, wg, wu, wd):
    nf = D_FF_EXPERT // MOE_TF

    def f_eff(i, f, nv):
        return jnp.where(i < nv[0], f, nf - 1)

    return pl.pallas_call(
        _experts_kernel,
        grid_spec=pltpu.PrefetchScalarGridSpec(
            num_scalar_prefetch=2,
            grid=(MOE_NB, nf),
            in_specs=[
                pl.BlockSpec((MOE_TM, D_MODEL), lambda i, f, be, nv: (i, 0)),
                pl.BlockSpec((None, D_MODEL, MOE_TF), lambda i, f, be, nv: (be[i], 0, f_eff(i, f, nv))),
                pl.BlockSpec((None, D_MODEL, MOE_TF), lambda i, f, be, nv: (be[i], 0, f_eff(i, f, nv))),
                pl.BlockSpec((None, MOE_TF, D_MODEL), lambda i, f, be, nv: (be[i], f_eff(i, f, nv), 0)),
            ],
            out_specs=pl.BlockSpec((MOE_TM, D_MODEL), lambda i, f, be, nv: (i, 0)),
            scratch_shapes=[pltpu.VMEM((MOE_TM, D_MODEL), BF16), pltpu.VMEM((MOE_TM, D_MODEL), F32)],
        ),
        out_shape=jax.ShapeDtypeStruct((MOE_P, D_MODEL), F32),
        compiler_params=_cparams("arbitrary", "arbitrary"),
        name="moe_experts",
    )(block_e, n_valid, xs, wg, wu, wd)


def _combine_kernel(p1_ref, p2_ref, ys_ref, x_ref, ga_ref, gb_ref, mod_ref, g_ref, o_ref, a_ref, b_ref, sems, *, tm):
    def start(r, c):
        _row_copy(ys_ref, a_ref, sems.at[0], p1_ref[0, r], r).start()
        _row_copy(ys_ref, b_ref, sems.at[1], p2_ref[0, r], r).start()
        return c

    def wait(r, c):
        _row_copy(ys_ref, a_ref, sems.at[0], 0, r).wait()
        _row_copy(ys_ref, b_ref, sems.at[1], 0, r).wait()
        return c

    lax.fori_loop(0, tm, start, 0, unroll=8)
    lax.fori_loop(0, tm, wait, 0, unroll=8)
    reps = D_MODEL // 128
    ga = jnp.concatenate([ga_ref[...]] * reps, axis=-1)
    gb = jnp.concatenate([gb_ref[...]] * reps, axis=-1)
    f = a_ref[...] * ga + b_ref[...] * gb
    x = x_ref[...] + mod_ref[:, 5 * D_MODEL:6 * D_MODEL] * f
    o_ref[...] = (x * lax.rsqrt(jnp.mean(x * x, axis=-1, keepdims=True) + EPS)) * g_ref[...]


def _combine(ys, pos1, pos2, x, ga, gb, mod, final_g):
    tm = GATHER_TM
    nblk = T_ALL // tm
    cidx = _cond_index(tm)
    row = lambda i: (i, 0)
    idx_blk = pl.BlockSpec((None, 1, tm), lambda i: (i, 0, 0), memory_space=pltpu.SMEM)
    return pl.pallas_call(
        functools.partial(_combine_kernel, tm=tm),
        grid=(nblk,),
        in_specs=[
            idx_blk, idx_blk,
            pl.BlockSpec(memory_space=pl.ANY),
            pl.BlockSpec((tm, D_MODEL), row),
            pl.BlockSpec((tm, 128), row),
            pl.BlockSpec((tm, 128), row),
            pl.BlockSpec((None, 1, N_MOD), lambda i: (cidx(i), 0, 0)),
            pl.BlockSpec((1, D_MODEL), lambda i: (0, 0)),
        ],
        out_specs=pl.BlockSpec((tm, D_MODEL), row),
        out_shape=jax.ShapeDtypeStruct((T_ALL, D_MODEL), F32),
        scratch_shapes=[
            pltpu.VMEM((tm, D_MODEL), F32),
            pltpu.VMEM((tm, D_MODEL), F32),
            pltpu.SemaphoreType.DMA((2,)),
        ],
        compiler_params=_cparams("arbitrary"),
        name="moe_combine_final_norm",
    )(pos1.reshape(nblk, 1, tm), pos2.reshape(nblk, 1, tm), ys, x, ga, gb, mod, final_g)


def kernel(x_prompt, x_sample, cache_na_k, cache_na_v, state_ret, c, c_ctx, w_ada, b_ada, norm_g, final_g, w_in_e, w_conv_e, na_rpb_e, w_out_e, ff_gate_e, ff_up_e, ff_down_e, w_in_o, ret_decay_o, w_out_o, router_o, ex_gate_o, ex_up_o, ex_down_o):
    x0 = jnp.concatenate([x_prompt.reshape(T_CTX, D_MODEL), x_sample.reshape(T_SMP, D_MODEL)], axis=0)
    cond = jnp.concatenate([c_ctx[None, :], c, jnp.zeros((N_COND - 1 - DEC_BATCH, D_MODEL), F32)], axis=0)
    mod = _ada(cond, w_ada, b_ada)
    mod0 = mod[0].reshape(N_COND, 1, N_MOD)
    mod1 = mod[1].reshape(N_COND, 1, N_MOD)

    bcu, q, k, v, kb, vb = _in_proj_e(x0, mod0, norm_g[0, 0][None, :], w_in_e[0].astype(BF16))
    o_ctx = _attn_ctx(q, kb, vb)
    o_smp = _attn_na(q, kb, vb,
                     cache_na_k[:, 0].reshape(DEC_BATCH, PAST_LEN, HALF),
                     cache_na_v[:, 0].reshape(DEC_BATCH, PAST_LEN, HALF),
                     _na_bias_table(na_rpb_e[0]))
    x1 = _out_proj_e(bcu, o_ctx, o_smp, x0, mod0, w_conv_e[0].T, w_out_e[0].astype(BF16))
    x2 = _ffn(x1, mod0, norm_g[0, 1][None, :], ff_gate_e[0].astype(BF16), ff_up_e[0].astype(BF16),
              ff_down_e[0].astype(BF16))

    mc_hi, mc_lo = _hi_lo(_channel_dft_table())
    ar, ai, q1, k1, v1, g1 = _in_proj_o(x2, mod1, norm_g[1, 0][None, :], w_in_o[0].astype(BF16), mc_hi, mc_lo)
    yc_ctx = _fourier_ctx(ar, ai)
    yc_smp = _fourier_smp(ar, ai)
    lg = jax.nn.log_sigmoid(ret_decay_o[0].astype(F32))
    yd_ctx, new_state = _retention(lg, q1, k1, v1, g1, None, seq=SEQ, nbatch=BATCH, row0=0, nh=8)
    yd_smp = _retention(lg, q1, k1, v1, g1, state_ret[:, 0], seq=DEC_SEQ, nbatch=DEC_BATCH, row0=T_CTX, nh=4)
    r_hi, r_lo = _hi_lo(jnp.pad(router_o[0], ((0, 0), (0, 128 - N_EXPERTS))))
    x3, hm, e1, e2, ga, gb = _out_proj_o(yc_ctx, yc_smp, yd_ctx, yd_smp, x2, mod1, w_out_o[0].astype(BF16),
                                         norm_g[1, 1][None, :], r_hi, r_lo)
    slot_t, pos1, pos2, block_e, n_valid = _routing_tables(e1[:, 0], e2[:, 0])
    xs = _gather_rows(hm, slot_t)
    ys = _experts(xs, block_e, n_valid, ex_gate_o[0].astype(BF16), ex_up_o[0].astype(BF16),
                  ex_down_o[0].astype(BF16))
    y = _combine(ys, pos1, pos2, x3, ga, gb, mod1, final_g[None, :])

    y_prompt = y[:T_CTX].reshape(BATCH, SEQ, D_MODEL)
    y_sample = y[T_CTX:].reshape(DEC_BATCH, DEC_SEQ, D_MODEL)
    new_na_k = k[:T_CTX].reshape(BATCH, 1, SEQ, N_HEADS, HEAD_DIM)
    new_na_v = v[:T_CTX].reshape(BATCH, 1, SEQ, N_HEADS, HEAD_DIM)
    new_state_ret = new_state.reshape(BATCH, 1, 2, N_HEADS, HEAD_DIM, HEAD_DIM)
    return (y_prompt, y_sample, new_na_k, new_na_v, new_state_ret)
```

```python
import functools

import numpy as np
import jax
import jax.numpy as jnp
from jax import lax
from jax.experimental import pallas as pl
from jax.experimental.pallas import tpu as pltpu

D_MODEL = 1024
BATCH = 32
SEQ = 256
DEC_BATCH = 4
DEC_SEQ = 4096
PAST_LEN = 256
GRID_W = 64
HEAD_DIM = 64
HALF = D_MODEL // 2
N_HEADS = HALF // HEAD_DIM
NA_WIN_R = 8
NA_WIN_C = 16
FT_GROUPS = 4
FT_GROUP_CH = HALF // FT_GROUPS
RET_CHUNK = 128
D_FF = 2816
N_EXPERTS = 8
D_FF_EXPERT = 3584
EPS = 1e-6
NEG_INF = -1e30
ATT_SCALE = HEAD_DIM ** -0.5

T_CTX = BATCH * SEQ
T_SMP = DEC_BATCH * DEC_SEQ
T_ALL = T_CTX + T_SMP
N_COND = 8
N_MOD = 6 * D_MODEL
GRID_H = DEC_SEQ // GRID_W
FT_N1 = 64

F32 = jnp.float32
BF16 = jnp.bfloat16
VMEM_LIMIT = 56 * 1024 * 1024

MOE_TM = 512
MOE_TF = 1792
MOE_A = 2 * T_ALL
MOE_NB = MOE_A // MOE_TM + N_EXPERTS
MOE_P = MOE_NB * MOE_TM
GATHER_TM = 256


def _cparams(*sem):
    return pltpu.CompilerParams(dimension_semantics=sem, vmem_limit_bytes=VMEM_LIMIT)


def _cond_index(tm):
    n_ctx = T_CTX // tm
    per_b = DEC_SEQ // tm

    def f(i):
        return jnp.where(i < n_ctx, 0, 1 + (i - n_ctx) // per_b)

    return f


def _silu(x):
    return x * (1.0 / (1.0 + jnp.exp(-x)))


def _norm_mod(x, g, scale, shift):
    y = x * lax.rsqrt(jnp.mean(x * x, axis=-1, keepdims=True) + EPS)
    return (y * g) * (1.0 + scale) + shift


def _split(a):
    hi = a.astype(BF16)
    lo = (a - hi.astype(F32)).astype(BF16)
    return hi, lo


def _dot(a, b):
    return jnp.dot(a, b, preferred_element_type=F32)


def _dot_nt(a, b):
    return lax.dot_general(a, b, (((1,), (1,)), ((), ())), preferred_element_type=F32)


def _dot3(a_hi, a_lo, b_hi, b_lo):
    return _dot(a_hi, b_hi) + (_dot(a_hi, b_lo) + _dot(a_lo, b_hi))


def _ada_kernel(c_ref, w_ref, b_ref, o_ref):
    s = _silu(c_ref[...]).astype(BF16)
    o_ref[...] = _dot(s, w_ref[...].astype(BF16)) + b_ref[...]


def _ada(cond, w_ada, b_ada):
    depth = w_ada.shape[0]
    tn = 1536
    return pl.pallas_call(
        _ada_kernel,
        grid=(depth, N_MOD // tn),
        in_specs=[
            pl.BlockSpec((N_COND, D_MODEL), lambda l, j: (0, 0)),
            pl.BlockSpec((None, D_MODEL, tn), lambda l, j: (l, 0, j)),
            pl.BlockSpec((None, 1, tn), lambda l, j: (l, 0, j)),
        ],
        out_specs=pl.BlockSpec((None, N_COND, tn), lambda l, j: (l, 0, j)),
        out_shape=jax.ShapeDtypeStruct((depth, N_COND, N_MOD), F32),
        compiler_params=_cparams("parallel", "parallel"),
        name="ada_mod",
    )(cond, w_ada, b_ada.reshape(depth, 1, N_MOD))


def _inproj_e_kernel(x_ref, mod_ref, g_ref, w_ref, bcu_ref, q_ref, k_ref, v_ref, kb_ref, vb_ref):
    m = mod_ref[...]
    h = _norm_mod(x_ref[...], g_ref[...], m[:, D_MODEL:2 * D_MODEL], m[:, 0:D_MODEL]).astype(BF16)
    bcu_ref[...] = _dot(h, w_ref[:, 0:3 * HALF])
    q_ref[...] = _dot(h, w_ref[:, 3 * HALF:4 * HALF]).astype(BF16)
    k = _dot(h, w_ref[:, 4 * HALF:5 * HALF])
    k_ref[...] = k
    kb_ref[...] = k.astype(BF16)
    v = _dot(h, w_ref[:, 5 * HALF:6 * HALF])
    v_ref[...] = v
    vb_ref[...] = v.astype(BF16)


def _in_proj_e(x, mod, g, w):
    tm = 512
    cidx = _cond_index(tm)
    row = lambda i: (i, 0)
    sds = jax.ShapeDtypeStruct
    return pl.pallas_call(
        _inproj_e_kernel,
        grid=(T_ALL // tm,),
        in_specs=[
            pl.BlockSpec((tm, D_MODEL), row),
            pl.BlockSpec((None, 1, N_MOD), lambda i: (cidx(i), 0, 0)),
            pl.BlockSpec((1, D_MODEL), lambda i: (0, 0)),
            pl.BlockSpec((D_MODEL, 6 * HALF), lambda i: (0, 0)),
        ],
        out_specs=[
            pl.BlockSpec((tm, 3 * HALF), row),
            pl.BlockSpec((tm, HALF), row),
            pl.BlockSpec((tm, HALF), row),
            pl.BlockSpec((tm, HALF), row),
            pl.BlockSpec((tm, HALF), row),
            pl.BlockSpec((tm, HALF), row),
        ],
        out_shape=[
            sds((T_ALL, 3 * HALF), F32),
            sds((T_ALL, HALF), BF16),
            sds((T_ALL, HALF), F32),
            sds((T_ALL, HALF), F32),
            sds((T_ALL, HALF), BF16),
            sds((T_ALL, HALF), BF16),
        ],
        compiler_params=_cparams("parallel"),
        name="in_proj_even",
    )(x, mod, g, w)


def _softmax_parts(parts):
    m = parts[0].max(axis=-1, keepdims=True)
    for s in parts[1:]:
        m = jnp.maximum(m, s.max(axis=-1, keepdims=True))
    es = [jnp.exp(s - m) for s in parts]
    l = es[0].sum(axis=-1, keepdims=True)
    for e in es[1:]:
        l = l + e.sum(axis=-1, keepdims=True)
    inv = 1.0 / l
    return [e * inv for e in es]


def _attn_ctx_kernel(q_ref, k_ref, v_ref, o_ref):
    for h in range(N_HEADS):
        sl = slice(h * HEAD_DIM, (h + 1) * HEAD_DIM)
        s = _dot_nt(q_ref[:, sl], k_ref[:, sl]) * ATT_SCALE
        (p,) = _softmax_parts([s])
        o_ref[:, sl] = _dot(p.astype(BF16), v_ref[:, sl]).astype(BF16)


def _attn_ctx(q, kb, vb):
    blk = pl.BlockSpec((SEQ, HALF), lambda b: (b, 0))
    return pl.pallas_call(
        _attn_ctx_kernel,
        grid=(BATCH,),
        in_specs=[blk, blk, blk],
        out_specs=blk,
        out_shape=jax.ShapeDtypeStruct((T_CTX, HALF), BF16),
        compiler_params=_cparams("parallel"),
        name="attn_context",
    )(q, kb, vb)


NA_Q_ROWS = 4
NA_K_ROWS = NA_WIN_R + NA_Q_ROWS


def _na_key_row0(rb):
    return jnp.clip(rb * NA_Q_ROWS - NA_WIN_R // 2, 0, GRID_H - NA_K_ROWS)


def _attn_na_kernel(q_ref, k_ref, v_ref, kc_ref, vc_ref, bias_ref, o_ref):
    rb = pl.program_id(1)
    start = pl.multiple_of(_na_key_row0(rb) * GRID_W, GRID_W)
    n_loc = NA_K_ROWS * GRID_W
    for h in range(N_HEADS):
        sl = slice(h * HEAD_DIM, (h + 1) * HEAD_DIM)
        q = q_ref[:, sl]
        s_loc = _dot_nt(q, k_ref[pl.ds(start, n_loc), sl]) * ATT_SCALE + bias_ref[h]
        s_ctx = _dot_nt(q, kc_ref[:, sl].astype(BF16)) * ATT_SCALE
        p_loc, p_ctx = _softmax_parts([s_loc, s_ctx])
        o = _dot(p_loc.astype(BF16), v_ref[pl.ds(start, n_loc), sl])
        o = o + _dot(p_ctx.astype(BF16), vc_ref[:, sl].astype(BF16))
        o_ref[:, sl] = o.astype(BF16)


def _na_bias_table(rpb):
    cols = np.arange(GRID_W)
    col_start = np.clip(cols - NA_WIN_C // 2, 0, GRID_W - NA_WIN_C)
    col_in = (cols[None, :] >= col_start[:, None]) & (cols[None, :] < col_start[:, None] + NA_WIN_C)
    col_off = np.clip(cols[None, :] - cols[:, None], 1 - NA_WIN_C, NA_WIN_C - 1) + (NA_WIN_C - 1)
    n_off = 2 * NA_WIN_C - 1
    pick = jnp.asarray(col_off[None] == np.arange(n_off)[:, None, None], F32)
    by_col = (rpb.astype(F32)[:, :, :, None, None] * pick[None, None]).sum(axis=2)
    by_col = jnp.pad(by_col, ((0, 0), (NA_K_ROWS, NA_K_ROWS), (0, 0), (0, 0)))
    n_blocks = GRID_H // NA_Q_ROWS
    variants = []
    for rb in (0, 1, n_blocks - 1):
        k_row0 = int(np.clip(rb * NA_Q_ROWS - NA_WIN_R // 2, 0, GRID_H - NA_K_ROWS))
        k_row = k_row0 + np.arange(NA_K_ROWS)
        per_q_row = []
        for qr in range(NA_Q_ROWS):
            q_row = rb * NA_Q_ROWS + qr
            win0 = int(np.clip(q_row - NA_WIN_R // 2, 0, GRID_H - NA_WIN_R))
            row_in = (k_row >= win0) & (k_row < win0 + NA_WIN_R)
            ro0 = k_row0 - q_row + (NA_WIN_R - 1) + NA_K_ROWS
            b = by_col[:, ro0:ro0 + NA_K_ROWS]
            keep = row_in[:, None, None] & col_in[None]
            per_q_row.append(jnp.where(keep[None], b, NEG_INF))
        b = jnp.stack(per_q_row, axis=1)
        variants.append(b.transpose(0, 1, 3, 2, 4).reshape(N_HEADS, NA_Q_ROWS * GRID_W, NA_K_ROWS * GRID_W))
    return jnp.stack(variants)


def _attn_na(q, kb, vb, cache_k, cache_v, bias):
    tq = NA_Q_ROWS * GRID_W
    n_blocks = GRID_H // NA_Q_ROWS
    ctx_q_tiles = T_CTX // tq
    ctx_b_tiles = T_CTX // DEC_SEQ

    def bias_idx(b, rb):
        return (jnp.where(rb == 0, 0, jnp.where(rb == n_blocks - 1, 2, 1)), 0, 0, 0)

    full = pl.BlockSpec((DEC_SEQ, HALF), lambda b, r: (ctx_b_tiles + b, 0))
    cache = pl.BlockSpec((None, PAST_LEN, HALF), lambda b, r: (b, 0, 0))
    return pl.pallas_call(
        _attn_na_kernel,
        grid=(DEC_BATCH, n_blocks),
        in_specs=[
            pl.BlockSpec((tq, HALF), lambda b, r: (ctx_q_tiles + b * n_blocks + r, 0)),
            full, full, cache, cache,
            pl.BlockSpec((None, N_HEADS, tq, NA_K_ROWS * GRID_W), bias_idx),
        ],
        out_specs=pl.BlockSpec((tq, HALF), lambda b, r: (b * n_blocks + r, 0)),
        out_shape=jax.ShapeDtypeStruct((T_SMP, HALF), BF16),
        compiler_params=_cparams("parallel", "arbitrary"),
        name="attn_neighbourhood",
    )(q, kb, vb, cache_k, cache_v, bias)


def _seq_edges(i, tm):
    n_ctx = T_CTX // tm
    ctx_per = SEQ // tm
    smp_per = DEC_SEQ // tm
    j = i - n_ctx
    first = jnp.where(i < n_ctx, i % ctx_per == 0, j % smp_per == 0)
    last = jnp.where(i < n_ctx, i % ctx_per == ctx_per - 1, j % smp_per == smp_per - 1)
    return first, last


def _outproj_e_kernel(bcu_ref, prev_ref, next_ref, oc_ref, os_ref, x_ref, mod_ref, wc_ref, w_ref, o_ref, *, tm):
    i = pl.program_id(0)
    first, last = _seq_edges(i, tm)
    bcu = bcu_ref[...]
    cu = bcu[:, HALF:2 * HALF] * bcu[:, 2 * HALF:3 * HALF]
    pv = prev_ref[7:8, :]
    nx = next_ref[0:1, :]
    cu_prev_row = jnp.where(first, 0.0, pv[:, HALF:2 * HALF] * pv[:, 2 * HALF:3 * HALF])
    cu_next_row = jnp.where(last, 0.0, nx[:, HALF:2 * HALF] * nx[:, 2 * HALF:3 * HALF])
    rows = lax.broadcasted_iota(jnp.int32, (tm, HALF), 0)
    cu_prev = jnp.where(rows == 0, cu_prev_row, pltpu.roll(cu, 1, axis=0))
    cu_next = jnp.where(rows == tm - 1, cu_next_row, pltpu.roll(cu, tm - 1, axis=0))
    wc = wc_ref[...]
    y_a = bcu[:, 0:HALF] * (cu_prev * wc[0:1] + cu * wc[1:2] + cu_next * wc[2:3])
    y_b = jnp.where(i < T_CTX // tm, oc_ref[...], os_ref[...])
    y = _dot(y_a.astype(BF16), w_ref[0:HALF, :]) + _dot(y_b, w_ref[HALF:D_MODEL, :])
    o_ref[...] = x_ref[...] + mod_ref[:, 2 * D_MODEL:3 * D_MODEL] * y


def _out_proj_e(bcu, o_ctx, o_smp, x, mod, w_conv_t, w):
    tm = 256
    n_ctx = T_CTX // tm
    cidx = _cond_index(tm)
    row = lambda i: (i, 0)
    nb8 = T_ALL // 8
    return pl.pallas_call(
        functools.partial(_outproj_e_kernel, tm=tm),
        grid=(T_ALL // tm,),
        in_specs=[
            pl.BlockSpec((tm, 3 * HALF), row),
            pl.BlockSpec((8, 3 * HALF), lambda i: (jnp.maximum(i * (tm // 8) - 1, 0), 0)),
            pl.BlockSpec((8, 3 * HALF), lambda i: (jnp.minimum((i + 1) * (tm // 8), nb8 - 1), 0)),
            pl.BlockSpec((tm, HALF), lambda i: (jnp.minimum(i, n_ctx - 1), 0)),
            pl.BlockSpec((tm, HALF), lambda i: (jnp.maximum(i - n_ctx, 0), 0)),
            pl.BlockSpec((tm, D_MODEL), row),
            pl.BlockSpec((None, 1, N_MOD), lambda i: (cidx(i), 0, 0)),
            pl.BlockSpec((3, HALF), lambda i: (0, 0)),
            pl.BlockSpec((D_MODEL, D_MODEL), lambda i: (0, 0)),
        ],
        out_specs=pl.BlockSpec((tm, D_MODEL), row),
        out_shape=jax.ShapeDtypeStruct((T_ALL, D_MODEL), F32),
        compiler_params=_cparams("parallel"),
        name="out_proj_even",
    )(bcu, bcu, bcu, o_ctx, o_smp, x, mod, w_conv_t, w)


def _ffn_kernel(x_ref, mod_ref, g_ref, wg_ref, wu_ref, wd_ref, o_ref, h_ref, acc_ref):
    f = pl.program_id(1)

    @pl.when(f == 0)
    def _():
        m = mod_ref[...]
        h = _norm_mod(x_ref[...], g_ref[...], m[:, 4 * D_MODEL:5 * D_MODEL], m[:, 3 * D_MODEL:4 * D_MODEL])
        h_ref[...] = h.astype(BF16)
        acc_ref[...] = jnp.zeros_like(acc_ref)

    h = h_ref[...]
    a = _silu(_dot(h, wg_ref[...])) * _dot(h, wu_ref[...])
    acc_ref[...] += _dot(a.astype(BF16), wd_ref[...])

    @pl.when(f == pl.num_programs(1) - 1)
    def _():
        o_ref[...] = x_ref[...] + mod_ref[:, 5 * D_MODEL:6 * D_MODEL] * acc_ref[...]


def _ffn(x, mod, g, wg, wu, wd):
    tm = 512
    tf = D_FF // 2
    cidx = _cond_index(tm)
    return pl.pallas_call(
        _ffn_kernel,
        grid=(T_ALL // tm, D_FF // tf),
        in_specs=[
            pl.BlockSpec((tm, D_MODEL), lambda i, f: (i, 0)),
            pl.BlockSpec((None, 1, N_MOD), lambda i, f: (cidx(i), 0, 0)),
            pl.BlockSpec((1, D_MODEL), lambda i, f: (0, 0)),
            pl.BlockSpec((D_MODEL, tf), lambda i, f: (0, f)),
            pl.BlockSpec((D_MODEL, tf), lambda i, f: (0, f)),
            pl.BlockSpec((tf, D_MODEL), lambda i, f: (f, 0)),
        ],
        out_specs=pl.BlockSpec((tm, D_MODEL), lambda i, f: (i, 0)),
        out_shape=jax.ShapeDtypeStruct((T_ALL, D_MODEL), F32),
        scratch_shapes=[pltpu.VMEM((tm, D_MODEL), BF16), pltpu.VMEM((tm, D_MODEL), F32)],
        compiler_params=_cparams("parallel", "arbitrary"),
        name="ffn_dense",
    )(x, mod, g, wg, wu, wd)


def _inproj_o_kernel(x_ref, mod_ref, g_ref, w_ref, mch_ref, mcl_ref, ar_ref, ai_ref, q_ref, k_ref, v_ref, gt_ref):
    m = mod_ref[...]
    h = _norm_mod(x_ref[...], g_ref[...], m[:, D_MODEL:2 * D_MODEL], m[:, 0:D_MODEL]).astype(BF16)
    u_hi, u_lo = _split(_dot(h, w_ref[:, 0:HALF]))
    a = _dot3(u_hi, u_lo, mch_ref[...], mcl_ref[...])
    ar_ref[...] = a[:, 0:HALF]
    ai_ref[...] = a[:, HALF:2 * HALF]
    q_ref[...] = _dot(h, w_ref[:, HALF:2 * HALF]).astype(BF16)
    k_ref[...] = _dot(h, w_ref[:, 2 * HALF:3 * HALF]) * ATT_SCALE
    v_ref[...] = _dot(h, w_ref[:, 3 * HALF:4 * HALF]).astype(BF16)
    gt_ref[...] = _dot(h, w_ref[:, 4 * HALF:5 * HALF])


def _in_proj_o(x, mod, g, w, mc_hi, mc_lo):
    tm = 512
    cidx = _cond_index(tm)
    row = lambda i: (i, 0)
    half_out = pl.BlockSpec((tm, HALF), row)
    sds = jax.ShapeDtypeStruct
    return pl.pallas_call(
        _inproj_o_kernel,
        grid=(T_ALL // tm,),
        in_specs=[
            pl.BlockSpec((tm, D_MODEL), row),
            pl.BlockSpec((None, 1, N_MOD), lambda i: (cidx(i), 0, 0)),
            pl.BlockSpec((1, D_MODEL), lambda i: (0, 0)),
            pl.BlockSpec((D_MODEL, 5 * HALF), lambda i: (0, 0)),
            pl.BlockSpec((HALF, 2 * HALF), lambda i: (0, 0)),
            pl.BlockSpec((HALF, 2 * HALF), lambda i: (0, 0)),
        ],
        out_specs=[half_out] * 6,
        out_shape=[
            sds((T_ALL, HALF), F32),
            sds((T_ALL, HALF), F32),
            sds((T_ALL, HALF), BF16),
            sds((T_ALL, HALF), F32),
            sds((T_ALL, HALF), BF16),
            sds((T_ALL, HALF), F32),
        ],
        compiler_params=_cparams("parallel"),
        name="in_proj_odd",
    )(x, mod, g, w, mc_hi, mc_lo)


def _dft_cos_sin(n):
    k = np.arange(n)
    ang = 2.0 * np.pi * ((k[:, None] * k[None, :]) % n) / n
    return np.cos(ang), np.sin(ang)


def _hi_lo(a):
    a = jnp.asarray(a, F32)
    hi = a.astype(BF16)
    return hi, (a - hi.astype(F32)).astype(BF16)


def _channel_dft_table():
    c, s = _dft_cos_sin(FT_GROUP_CH)
    scale = FT_GROUP_CH ** -0.5
    eye = np.eye(FT_GROUPS)
    return np.concatenate([np.kron(eye, c * scale), np.kron(eye, -s * scale)], axis=1)


def _fourier_ctx_kernel(ar_ref, ai_ref, th_ref, tl_ref, o_ref):
    a_hi, a_lo = _split(jnp.concatenate([ar_ref[...], ai_ref[...]], axis=0))
    o_ref[...] = _dot3(th_ref[...], tl_ref[...], a_hi, a_lo).astype(BF16)


def _fourier_ctx(ar, ai):
    c, s = _dft_cos_sin(SEQ)
    t_hi, t_lo = _hi_lo(np.concatenate([c, s], axis=1) * SEQ ** -0.5)
    blk = pl.BlockSpec((SEQ, HALF), lambda b: (b, 0))
    tab = pl.BlockSpec((SEQ, 2 * SEQ), lambda b: (0, 0))
    return pl.pallas_call(
        _fourier_ctx_kernel,
        grid=(BATCH,),
        in_specs=[blk, blk, tab, tab],
        out_specs=blk,
        out_shape=jax.ShapeDtypeStruct((T_CTX, HALF), BF16),
        compiler_params=_cparams("parallel"),
        name="fourier_context",
    )(ar, ai, t_hi, t_lo)


def _fourier_s1_kernel(ar_ref, ai_ref, mh_ref, ml_ref, yr_ref, yi_ref):
    a_hi, a_lo = _split(jnp.concatenate([ar_ref[...], ai_ref[...]], axis=0))
    y = _dot3(mh_ref[...], ml_ref[...], a_hi, a_lo)
    yr_ref[...] = y[0:FT_N1]
    yi_ref[...] = y[FT_N1:2 * FT_N1]


def _fourier_s3_kernel(yr_ref, yi_ref, tc_ref, ts_ref, mh_ref, ml_ref, o_ref, *, nk):
    lanes = HALF // tc_ref.shape[-1]
    for j in range(nk):
        tc = jnp.concatenate([tc_ref[j]] * lanes, axis=-1)
        ts = jnp.concatenate([ts_ref[j]] * lanes, axis=-1)
        yr = yr_ref[j]
        yi = yi_ref[j]
        z = jnp.concatenate([yr * tc + yi * ts, yi * tc - yr * ts], axis=0)
        z_hi, z_lo = _split(z)
        o_ref[:, j * HALF:(j + 1) * HALF] = _dot3(mh_ref[...], ml_ref[...], z_hi, z_lo).astype(BF16)


def _fourier_smp(ar, ai):
    n1 = FT_N1
    wide = n1 * HALF
    c, s = _dft_cos_sin(n1)
    m1_hi, m1_lo = _hi_lo(np.block([[c, s], [-s, c]]))
    m3_hi, m3_lo = _hi_lo(np.concatenate([c, s], axis=1) / n1)
    kk = np.arange(n1)
    ang = 2.0 * np.pi * (kk[:, None] * kk[None, :]) / DEC_SEQ
    tw_c = jnp.broadcast_to(jnp.asarray(np.cos(ang), F32)[:, :, None], (n1, n1, 128))
    tw_s = jnp.broadcast_to(jnp.asarray(np.sin(ang), F32)[:, :, None], (n1, n1, 128))

    ncol = 4096
    ctx_tiles = T_CTX // n1 // n1
    a_blk = pl.BlockSpec((n1, ncol), lambda b, j: (ctx_tiles + b, j))
    y_blk = pl.BlockSpec((n1, ncol), lambda b, j: (b, j))
    m1_blk = pl.BlockSpec((2 * n1, 2 * n1), lambda b, j: (0, 0))
    y_sds = jax.ShapeDtypeStruct((DEC_BATCH * n1, wide), F32)
    yr, yi = pl.pallas_call(
        _fourier_s1_kernel,
        grid=(DEC_BATCH, wide // ncol),
        in_specs=[a_blk, a_blk, m1_blk, m1_blk],
        out_specs=[y_blk, y_blk],
        out_shape=[y_sds, y_sds],
        compiler_params=_cparams("parallel", "parallel"),
        name="fourier_latent_stage1",
    )(ar.reshape(T_ALL // n1, wide), ai.reshape(T_ALL // n1, wide), m1_hi, m1_lo)

    nk = 8
    z_blk = pl.BlockSpec((nk, n1, HALF), lambda b, k: (b * (n1 // nk) + k, 0, 0))
    tw_blk = pl.BlockSpec((nk, n1, 128), lambda b, k: (k, 0, 0))
    m3_blk = pl.BlockSpec((n1, 2 * n1), lambda b, k: (0, 0))
    out = pl.pallas_call(
        functools.partial(_fourier_s3_kernel, nk=nk),
        grid=(DEC_BATCH, n1 // nk),
        in_specs=[z_blk, z_blk, tw_blk, tw_blk, m3_blk, m3_blk],
        out_specs=pl.BlockSpec((n1, nk * HALF), lambda b, k: (b, k)),
        out_shape=jax.ShapeDtypeStruct((DEC_BATCH * n1, wide), BF16),
        compiler_params=_cparams("parallel", "parallel"),
        name="fourier_latent_stage3",
    )(yr.reshape(DEC_BATCH * n1, n1, HALF), yi.reshape(DEC_BATCH * n1, n1, HALF), tw_c, tw_s, m3_hi, m3_lo)
    return out.reshape(T_SMP, HALF)


def _retention_kernel(lg_ref, q_ref, k_ref, v_ref, g_ref, *rest, seq, nh, has_s0):
    if has_s0:
        s0_ref, y_ref, of_ref, ob_ref, dec_ref, xi_ref, zeta_ref = rest
        st_ref = None
    else:
        y_ref, st_ref, of_ref, ob_ref, dec_ref, xi_ref, zeta_ref = rest
        s0_ref = None
    L = RET_CHUNK
    nc = seq // L
    hb = pl.program_id(1)
    diff = (lax.broadcasted_iota(jnp.int32, (L, L), 0) - lax.broadcasted_iota(jnp.int32, (L, L), 1)).astype(F32)
    li = lax.broadcasted_iota(jnp.int32, (L, HEAD_DIM), 0).astype(F32)
    one = jnp.ones((1, 1), F32)

    gcs = []
    for hh in range(nh):
        lgf = lg_ref[0, hb * nh + hh]
        lgb = lg_ref[1, hb * nh + hh]
        dec_ref[2 * hh] = jnp.where(diff >= 0, jnp.exp(lgf * jnp.maximum(diff, 0.0)), 0.0)
        dec_ref[2 * hh + 1] = jnp.where(diff <= 0, jnp.exp(lgb * jnp.maximum(-diff, 0.0)), 0.0)
        xi_ref[2 * hh] = jnp.exp(lgf * (li + 1.0))
        xi_ref[2 * hh + 1] = jnp.exp(lgb * (L - li))
        zeta_ref[2 * hh] = jnp.exp(lgf * (L - 1.0 - li))
        zeta_ref[2 * hh + 1] = jnp.exp(lgb * li)
        gcs.append(jnp.exp(one * (lgf * L)))
        gcs.append(jnp.exp(one * (lgb * L)))

    def chunk(c, s, t, sl, o_ref):
        r0 = pl.multiple_of(c * L, L)
        qc = q_ref[pl.ds(r0, L), sl]
        kc = k_ref[pl.ds(r0, L), sl]
        vc = v_ref[pl.ds(r0, L), sl]
        inner = _dot_nt(qc, kc.astype(BF16)) * dec_ref[t]
        o_ref[pl.ds(r0, L), sl] = _dot(inner.astype(BF16), vc) + _dot(qc, s.astype(BF16)) * xi_ref[t]
        return s * gcs[t] + _dot((kc * zeta_ref[t]).T.astype(BF16), vc)

    def scan_step(i, states):
        out = []
        for hh in range(nh):
            sl = slice(hh * HEAD_DIM, (hh + 1) * HEAD_DIM)
            out.append(chunk(i, states[2 * hh], 2 * hh, sl, of_ref))
            out.append(chunk(nc - 1 - i, states[2 * hh + 1], 2 * hh + 1, sl, ob_ref))
        return tuple(out)

    if has_s0:
        init = tuple(s0_ref[t % 2, t // 2] for t in range(2 * nh))
    else:
        init = tuple(jnp.zeros((HEAD_DIM, HEAD_DIM), F32) for _ in range(2 * nh))
    final = lax.fori_loop(0, nc, scan_step, init)
    if st_ref is not None:
        for t in range(2 * nh):
            st_ref[t % 2, t // 2] = final[t]

    def finish(c, carry):
        r0 = pl.multiple_of(c * L, L)
        o_all = of_ref[pl.ds(r0, L), :] + ob_ref[pl.ds(r0, L), :]
        gate = _silu(g_ref[pl.ds(r0, L), :])
        for hh in range(nh):
            sl = slice(hh * HEAD_DIM, (hh + 1) * HEAD_DIM)
            o = o_all[:, sl]
            mu = jnp.mean(o, axis=-1, keepdims=True)
            var = jnp.mean(jnp.square(o - mu), axis=-1, keepdims=True)
            y_ref[pl.ds(r0, L), sl] = (gate[:, sl] * ((o - mu) * lax.rsqrt(var + EPS))).astype(BF16)
        return carry

    lax.fori_loop(0, nc, finish, 0)


def _retention(lg, q, k, v, g, s0, *, seq, nbatch, row0, nh):
    has_s0 = s0 is not None
    tile0 = row0 // seq
    width = nh * HEAD_DIM
    blk = pl.BlockSpec((seq, width), lambda b, hb: (tile0 + b, hb))
    st_blk = pl.BlockSpec((None, 2, nh, HEAD_DIM, HEAD_DIM), lambda b, hb: (b, 0, hb, 0, 0))
    in_specs = [pl.BlockSpec(memory_space=pltpu.SMEM), blk, blk, blk, blk]
    args = [lg, q, k, v, g]
    y_spec = pl.BlockSpec((seq, width), lambda b, hb: (b, hb))
    y_sds = jax.ShapeDtypeStruct((nbatch * seq, HALF), BF16)
    if has_s0:
        in_specs.append(st_blk)
        args.append(s0)
        out_specs, out_shape = y_spec, y_sds
    else:
        out_specs = [y_spec, st_blk]
        out_shape = [y_sds, jax.ShapeDtypeStruct((nbatch, 2, N_HEADS, HEAD_DIM, HEAD_DIM), F32)]
    return pl.pallas_call(
        functools.partial(_retention_kernel, seq=seq, nh=nh, has_s0=has_s0),
        grid=(nbatch, N_HEADS // nh),
        in_specs=in_specs,
        out_specs=out_specs,
        out_shape=out_shape,
        scratch_shapes=[
            pltpu.VMEM((seq, width), F32),
            pltpu.VMEM((seq, width), F32),
            pltpu.VMEM((2 * nh, RET_CHUNK, RET_CHUNK), F32),
            pltpu.VMEM((2 * nh, RET_CHUNK, HEAD_DIM), F32),
            pltpu.VMEM((2 * nh, RET_CHUNK, HEAD_DIM), F32),
        ],
        compiler_params=_cparams("parallel", "parallel"),
        name="retention_%d" % seq,
    )(*args)


def _outproj_o_kernel(ycc_ref, ycs_ref, ydc_ref, yds_ref, x_ref, mod_ref, w_ref, g_ref, rh_ref, rl_ref,
                      x3_ref, h_ref, e1_ref, e2_ref, ga_ref, gb_ref, *, tm):
    is_ctx = pl.program_id(0) < T_CTX // tm
    y_c = jnp.where(is_ctx, ycc_ref[...], ycs_ref[...])
    y_d = jnp.where(is_ctx, ydc_ref[...], yds_ref[...])
    y = _dot(y_c, w_ref[0:HALF, :]) + _dot(y_d, w_ref[HALF:D_MODEL, :])
    m = mod_ref[...]
    x3 = x_ref[...] + m[:, 2 * D_MODEL:3 * D_MODEL] * y
    x3_ref[...] = x3
    h = _norm_mod(x3, g_ref[...], m[:, 4 * D_MODEL:5 * D_MODEL], m[:, 3 * D_MODEL:4 * D_MODEL])
    h_ref[...] = h
    h_hi, h_lo = _split(h)
    logits = _dot3(h_hi, h_lo, rh_ref[...], rl_ref[...])
    idx = lax.broadcasted_iota(jnp.int32, logits.shape, 1).astype(F32)
    logits = jnp.where(idx < float(N_EXPERTS), logits, -jnp.inf)
    m1 = logits.max(axis=-1, keepdims=True)
    e1 = jnp.where(logits == m1, idx, float(N_EXPERTS)).min(axis=-1, keepdims=True)
    rest = jnp.where(idx == e1, -jnp.inf, logits)
    m2 = rest.max(axis=-1, keepdims=True)
    e2 = jnp.where(rest == m2, idx, float(N_EXPERTS)).min(axis=-1, keepdims=True)
    ex = jnp.exp(m2 - m1)
    den = 1.0 + ex
    wide = (tm, 128)
    e1_ref[...] = jnp.broadcast_to(e1, wide).astype(jnp.int32)
    e2_ref[...] = jnp.broadcast_to(e2, wide).astype(jnp.int32)
    ga_ref[...] = jnp.broadcast_to(1.0 / den, wide)
    gb_ref[...] = jnp.broadcast_to(ex / den, wide)


def _out_proj_o(yc_ctx, yc_smp, yd_ctx, yd_smp, x, mod, w, g, r_hi, r_lo):
    tm = 256
    n_ctx = T_CTX // tm
    cidx = _cond_index(tm)
    row = lambda i: (i, 0)
    ctx_blk = pl.BlockSpec((tm, HALF), lambda i: (jnp.minimum(i, n_ctx - 1), 0))
    smp_blk = pl.BlockSpec((tm, HALF), lambda i: (jnp.maximum(i - n_ctx, 0), 0))
    sds = jax.ShapeDtypeStruct
    rep = pl.BlockSpec((tm, 128), row)
    return pl.pallas_call(
        functools.partial(_outproj_o_kernel, tm=tm),
        grid=(T_ALL // tm,),
        in_specs=[
            ctx_blk, smp_blk, ctx_blk, smp_blk,
            pl.BlockSpec((tm, D_MODEL), row),
            pl.BlockSpec((None, 1, N_MOD), lambda i: (cidx(i), 0, 0)),
            pl.BlockSpec((D_MODEL, D_MODEL), lambda i: (0, 0)),
            pl.BlockSpec((1, D_MODEL), lambda i: (0, 0)),
            pl.BlockSpec((D_MODEL, 128), lambda i: (0, 0)),
            pl.BlockSpec((D_MODEL, 128), lambda i: (0, 0)),
        ],
        out_specs=[pl.BlockSpec((tm, D_MODEL), row), pl.BlockSpec((tm, D_MODEL), row), rep, rep, rep, rep],
        out_shape=[
            sds((T_ALL, D_MODEL), F32), sds((T_ALL, D_MODEL), F32),
            sds((T_ALL, 128), jnp.int32), sds((T_ALL, 128), jnp.int32),
            sds((T_ALL, 128), F32), sds((T_ALL, 128), F32),
        ],
        compiler_params=_cparams("parallel"),
        name="out_proj_odd_route",
    )(yc_ctx, yc_smp, yd_ctx, yd_smp, x, mod, w, g, r_hi, r_lo)


def _routing_tables(e1, e2):
    flat_e = jnp.concatenate([e1, e2])
    onehot = (flat_e[:, None] == jnp.arange(N_EXPERTS, dtype=jnp.int32)[None, :]).astype(jnp.int32)
    csum = jnp.cumsum(onehot, axis=0)
    counts = csum[-1]
    padded = (counts + MOE_TM - 1) // MOE_TM * MOE_TM
    ends = jnp.cumsum(padded)
    pos = jnp.sum(onehot * (csum - 1 + (ends - padded)[None, :]), axis=1)
    tok = jnp.tile(jnp.arange(T_ALL, dtype=jnp.int32), 2)
    slot_t = jnp.zeros((MOE_P,), jnp.int32).at[pos].set(tok)
    block_row0 = jnp.arange(MOE_NB, dtype=jnp.int32) * MOE_TM
    block_e = jnp.minimum(
        jnp.sum((ends[None, :] <= block_row0[:, None]).astype(jnp.int32), axis=1), N_EXPERTS - 1
    ).astype(jnp.int32)
    n_valid = (ends[-1] // MOE_TM).astype(jnp.int32).reshape(1)
    return slot_t, pos[:T_ALL].astype(jnp.int32), pos[T_ALL:].astype(jnp.int32), block_e, n_valid


def _row_copy(src_ref, dst_ref, sem, src_row, dst_row):
    return pltpu.make_async_copy(src_ref.at[pl.ds(src_row, 1), :], dst_ref.at[pl.ds(dst_row, 1), :], sem)


def _gather_kernel(idx_ref, src_ref, o_ref, sem, *, tm):
    def start(r, c):
        _row_copy(src_ref, o_ref, sem, idx_ref[0, r], r).start()
        return c

    def wait(r, c):
        _row_copy(src_ref, o_ref, sem, 0, r).wait()
        return c

    lax.fori_loop(0, tm, start, 0, unroll=8)
    lax.fori_loop(0, tm, wait, 0, unroll=8)


def _gather_rows(src, slot_t):
    tm = GATHER_TM
    nblk = MOE_P // tm
    return pl.pallas_call(
        functools.partial(_gather_kernel, tm=tm),
        grid=(nblk,),
        in_specs=[
            pl.BlockSpec((None, 1, tm), lambda i: (i, 0, 0), memory_space=pltpu.SMEM),
            pl.BlockSpec(memory_space=pl.ANY),
        ],
        out_specs=pl.BlockSpec((tm, D_MODEL), lambda i: (i, 0)),
        out_shape=jax.ShapeDtypeStruct((MOE_P, D_MODEL), F32),
        scratch_shapes=[pltpu.SemaphoreType.DMA(())],
        compiler_params=_cparams("arbitrary"),
        name="moe_gather",
    )(slot_t.reshape(nblk, 1, tm), src)


def _experts_kernel(be_ref, nv_ref, x_ref, wg_ref, wu_ref, wd_ref, o_ref, xb_ref, acc_ref):
    i = pl.program_id(0)
    f = pl.program_id(1)
    valid = i < nv_ref[0]

    @pl.when(jnp.logical_and(valid, f == 0))
    def _():
        xb_ref[...] = x_ref[...].astype(BF16)
        acc_ref[...] = jnp.zeros_like(acc_ref)

    @pl.when(valid)
    def _():
        x = xb_ref[...]
        a = _silu(_dot(x, wg_ref[...])) * _dot(x, wu_ref[...])
        acc_ref[...] += _dot(a.astype(BF16), wd_ref[...])

    last = f == pl.num_programs(1) - 1

    @pl.when(jnp.logical_and(valid, last))
    def _():
        o_ref[...] = acc_ref[...]

    @pl.when(jnp.logical_and(jnp.logical_not(valid), last))
    def _():
        o_ref[...] = jnp.zeros_like(o_ref)


def _experts(xs, block_e, n_valid, wg, wu, wd):
    nf = D_FF_EXPERT // MOE_TF

    def f_eff(i, f, nv):
        return jnp.where(i < nv[0], f, nf - 1)

    return pl.pallas_call(
        _experts_kernel,
        grid_spec=pltpu.PrefetchScalarGridSpec(
            num_scalar_prefetch=2,
            grid=(MOE_NB, nf),
            in_specs=[
                pl.BlockSpec((MOE_TM, D_MODEL), lambda i, f, be, nv: (i, 0)),
                pl.BlockSpec((None, D_MODEL, MOE_TF), lambda i, f, be, nv: (be[i], 0, f_eff(i, f, nv))),
                pl.BlockSpec((None, D_MODEL, MOE_TF), lambda i, f, be, nv: (be[i], 0, f_eff(i, f, nv))),
                pl.BlockSpec((None, MOE_TF, D_MODEL), lambda i, f, be, nv: (be[i], f_eff(i, f, nv), 0)),
            ],
            out_specs=pl.BlockSpec((MOE_TM, D_MODEL), lambda i, f, be, nv: (i, 0)),
            scratch_shapes=[pltpu.VMEM((MOE_TM, D_MODEL), BF16), pltpu.VMEM((MOE_TM, D_MODEL), F32)],
        ),
        out_shape=jax.ShapeDtypeStruct((MOE_P, D_MODEL), F32),
        compiler_params=_cparams("arbitrary", "arbitrary"),
        name="moe_experts",
    )(block_e, n_valid, xs, wg, wu, wd)


def _combine_kernel(p1_ref, p2_ref, ys_ref, x_ref, ga_ref, gb_ref, mod_ref, g_ref, o_ref, a_ref, b_ref, sems, *, tm):
    def start(r, c):
        _row_copy(ys_ref, a_ref, sems.at[0], p1_ref[0, r], r).start()
        _row_copy(ys_ref, b_ref, sems.at[1], p2_ref[0, r], r).start()
        return c

    def wait(r, c):
        _row_copy(ys_ref, a_ref, sems.at[0], 0, r).wait()
        _row_copy(ys_ref, b_ref, sems.at[1], 0, r).wait()
        return c

    lax.fori_loop(0, tm, start, 0, unroll=8)
    lax.fori_loop(0, tm, wait, 0, unroll=8)
    reps = D_MODEL // 128
    ga = jnp.concatenate([ga_ref[...]] * reps, axis=-1)
    gb = jnp.concatenate([gb_ref[...]] * reps, axis=-1)
    f = a_ref[...] * ga + b_ref[...] * gb
    x = x_ref[...] + mod_ref[:, 5 * D_MODEL:6 * D_MODEL] * f
    o_ref[...] = (x * lax.rsqrt(jnp.mean(x * x, axis=-1, keepdims=True) + EPS)) * g_ref[...]


def _combine(ys, pos1, pos2, x, ga, gb, mod, final_g):
    tm = GATHER_TM
    nblk = T_ALL // tm
    cidx = _cond_index(tm)
    row = lambda i: (i, 0)
    idx_blk = pl.BlockSpec((None, 1, tm), lambda i: (i, 0, 0), memory_space=pltpu.SMEM)
    return pl.pallas_call(
        functools.partial(_combine_kernel, tm=tm),
        grid=(nblk,),
        in_specs=[
            idx_blk, idx_blk,
            pl.BlockSpec(memory_space=pl.ANY),
            pl.BlockSpec((tm, D_MODEL), row),
            pl.BlockSpec((tm, 128), row),
            pl.BlockSpec((tm, 128), row),
            pl.BlockSpec((None, 1, N_MOD), lambda i: (cidx(i), 0, 0)),
            pl.BlockSpec((1, D_MODEL), lambda i: (0, 0)),
        ],
        out_specs=pl.BlockSpec((tm, D_MODEL), row),
        out_shape=jax.ShapeDtypeStruct((T_ALL, D_MODEL), F32),
        scratch_shapes=[
            pltpu.VMEM((tm, D_MODEL), F32),
            pltpu.VMEM((tm, D_MODEL), F32),
            pltpu.SemaphoreType.DMA((2,)),
        ],
        compiler_params=_cparams("arbitrary"),
        name="moe_combine_final_norm",
    )(pos1.reshape(nblk, 1, tm), pos2.reshape(nblk, 1, tm), ys, x, ga, gb, mod, final_g)


def kernel(x_prompt, x_sample, cache_na_k, cache_na_v, state_ret, c, c_ctx, w_ada, b_ada, norm_g, final_g, w_in_e, w_conv_e, na_rpb_e, w_out_e, ff_gate_e, ff_up_e, ff_down_e, w_in_o, ret_decay_o, w_out_o, router_o, ex_gate_o, ex_up_o, ex_down_o):
    x0 = jnp.concatenate([x_prompt.reshape(T_CTX, D_MODEL), x_sample.reshape(T_SMP, D_MODEL)], axis=0)
    cond = jnp.concatenate([c_ctx[None, :], c, jnp.zeros((N_COND - 1 - DEC_BATCH, D_MODEL), F32)], axis=0)
    mod = _ada(cond, w_ada, b_ada)
    mod0 = mod[0].reshape(N_COND, 1, N_MOD)
    mod1 = mod[1].reshape(N_COND, 1, N_MOD)

    bcu, q, k, v, kb, vb = _in_proj_e(x0, mod0, norm_g[0, 0][None, :], w_in_e[0].astype(BF16))
    o_ctx = _attn_ctx(q, kb, vb)
    o_smp = _attn_na(q, kb, vb,
                     cache_na_k[:, 0].reshape(DEC_BATCH, PAST_LEN, HALF),
                     cache_na_v[:, 0].reshape(DEC_BATCH, PAST_LEN, HALF),
                     _na_bias_table(na_rpb_e[0]))
    x1 = _out_proj_e(bcu, o_ctx, o_smp, x0, mod0, w_conv_e[0].T, w_out_e[0].astype(BF16))
    x2 = _ffn(x1, mod0, norm_g[0, 1][None, :], ff_gate_e[0].astype(BF16), ff_up_e[0].astype(BF16),
              ff_down_e[0].astype(BF16))

    mc_hi, mc_lo = _hi_lo(_channel_dft_table())
    ar, ai, q1, k1, v1, g1 = _in_proj_o(x2, mod1, norm_g[1, 0][None, :], w_in_o[0].astype(BF16), mc_hi, mc_lo)
    yc_ctx = _fourier_ctx(ar, ai)
    yc_smp = _fourier_smp(ar, ai)
    lg = jax.nn.log_sigmoid(ret_decay_o[0].astype(F32))
    yd_ctx, new_state = _retention(lg, q1, k1, v1, g1, None, seq=SEQ, nbatch=BATCH, row0=0, nh=8)
    yd_smp = _retention(lg, q1, k1, v1, g1, state_ret[:, 0], seq=DEC_SEQ, nbatch=DEC_BATCH, row0=T_CTX, nh=4)
    r_hi, r_lo = _hi_lo(jnp.pad(router_o[0], ((0, 0), (0, 128 - N_EXPERTS))))
    x3, hm, e1, e2, ga, gb = _out_proj_o(yc_ctx, yc_smp, yd_ctx, yd_smp, x2, mod1, w_out_o[0].astype(BF16),
                                         norm_g[1, 1][None, :], r_hi, r_lo)
    slot_t, pos1, pos2, block_e, n_valid = _routing_tables(e1[:, 0], e2[:, 0])
    xs = _gather_rows(hm, slot_t)
    ys = _experts(xs, block_e, n_valid, ex_gate_o[0].astype(BF16), ex_up_o[0].astype(BF16),
                  ex_down_o[0].astype(BF16))
    y = _combine(ys, pos1, pos2, x3, ga, gb, mod1, final_g[None, :])

    y_prompt = y[:T_CTX].reshape(BATCH, SEQ, D_MODEL)
    y_sample = y[T_CTX:].reshape(DEC_BATCH, DEC_SEQ, D_MODEL)
    new_na_k = k[:T_CTX].reshape(BATCH, 1, SEQ, N_HEADS, HEAD_DIM)
    new_na_v = v[:T_CTX].reshape(BATCH, 1, SEQ, N_HEADS, HEAD_DIM)
    new_state_ret = new_state.reshape(BATCH, 1, 2, N_HEADS, HEAD_DIM, HEAD_DIM)
    return (y_prompt, y_sample, new_na_k, new_na_v, new_state_ret)
```

```python
import functools

import numpy as np
import jax
import jax.numpy as jnp
from jax import lax
from jax.experimental import pallas as pl
from jax.experimental.pallas import tpu as pltpu

D_MODEL = 1024
BATCH = 32
SEQ = 256
DEC_BATCH = 4
DEC_SEQ = 4096
PAST_LEN = 256
GRID_W = 64
HEAD_DIM = 64
HALF = D_MODEL // 2
N_HEADS = HALF // HEAD_DIM
NA_WIN_R = 8
NA_WIN_C = 16
FT_GROUPS = 4
FT_GROUP_CH = HALF // FT_GROUPS
RET_CHUNK = 256
D_FF = 2816
N_EXPERTS = 8
D_FF_EXPERT = 3584
EPS = 1e-6
NEG_INF = -1e30
ATT_SCALE = HEAD_DIM ** -0.5

T_CTX = BATCH * SEQ
T_SMP = DEC_BATCH * DEC_SEQ
T_ALL = T_CTX + T_SMP
N_COND = 8
N_MOD = 6 * D_MODEL
GRID_H = DEC_SEQ // GRID_W
FT_N1 = 64

F32 = jnp.float32
BF16 = jnp.bfloat16
VMEM_LIMIT = 56 * 1024 * 1024

MOE_TM = 512
MOE_TF = 1792
MOE_A = 2 * T_ALL
MOE_NB = MOE_A // MOE_TM + N_EXPERTS
MOE_P = MOE_NB * MOE_TM
GATHER_TM = 256


ROW_TILE = (D_MODEL // 128, 128)


def _store_row_tiles(ref, x):
    for j in range(ROW_TILE[0]):
        ref[:, j, :] = x[:, j * 128:(j + 1) * 128]


def _load_row_tiles(ref):
    return jnp.concatenate([ref[:, j, :] for j in range(ROW_TILE[0])], axis=-1)


def _cparams(*sem):
    return pltpu.CompilerParams(dimension_semantics=sem, vmem_limit_bytes=VMEM_LIMIT)


def _cond_index(tm):
    n_ctx = T_CTX // tm
    per_b = DEC_SEQ // tm

    def f(i):
        return jnp.where(i < n_ctx, 0, 1 + (i - n_ctx) // per_b)

    return f


def _silu(x):
    return x * (1.0 / (1.0 + jnp.exp(-x)))


def _norm_mod(x, g, scale, shift):
    y = x * lax.rsqrt(jnp.mean(x * x, axis=-1, keepdims=True) + EPS)
    return (y * g) * (1.0 + scale) + shift


def _split(a):
    hi = a.astype(BF16)
    lo = (a - hi.astype(F32)).astype(BF16)
    return hi, lo


def _dot(a, b):
    return jnp.dot(a, b, preferred_element_type=F32)


def _dot_nt(a, b):
    return lax.dot_general(a, b, (((1,), (1,)), ((), ())), preferred_element_type=F32)


def _dot3(a_hi, a_lo, b_hi, b_lo):
    return _dot(a_hi, b_hi) + (_dot(a_hi, b_lo) + _dot(a_lo, b_hi))


def _ada_kernel(c_ref, w_ref, b_ref, o_ref):
    s = _silu(c_ref[...]).astype(BF16)
    o_ref[...] = _dot(s, w_ref[...].astype(BF16)) + b_ref[...]


def _ada(cond, w_ada, b_ada):
    depth = w_ada.shape[0]
    tn = 1536
    return pl.pallas_call(
        _ada_kernel,
        grid=(depth, N_MOD // tn),
        in_specs=[
            pl.BlockSpec((N_COND, D_MODEL), lambda l, j: (0, 0)),
            pl.BlockSpec((None, D_MODEL, tn), lambda l, j: (l, 0, j)),
            pl.BlockSpec((None, 1, tn), lambda l, j: (l, 0, j)),
        ],
        out_specs=pl.BlockSpec((None, N_COND, tn), lambda l, j: (l, 0, j)),
        out_shape=jax.ShapeDtypeStruct((depth, N_COND, N_MOD), F32),
        compiler_params=_cparams("parallel", "parallel"),
        name="ada_mod",
    )(cond, w_ada, b_ada.reshape(depth, 1, N_MOD))


def _inproj_e_kernel(x_ref, mod_ref, g_ref, w_ref, bcu_ref, q_ref, k_ref, v_ref, kb_ref, vb_ref):
    m = mod_ref[...]
    h = _norm_mod(x_ref[...], g_ref[...], m[:, D_MODEL:2 * D_MODEL], m[:, 0:D_MODEL]).astype(BF16)
    bcu_ref[...] = _dot(h, w_ref[:, 0:3 * HALF])
    q_ref[...] = _dot(h, w_ref[:, 3 * HALF:4 * HALF]).astype(BF16)
    k = _dot(h, w_ref[:, 4 * HALF:5 * HALF])
    k_ref[...] = k
    kb_ref[...] = k.astype(BF16)
    v = _dot(h, w_ref[:, 5 * HALF:6 * HALF])
    v_ref[...] = v
    vb_ref[...] = v.astype(BF16)


def _in_proj_e(x, mod, g, w):
    tm = 512
    cidx = _cond_index(tm)
    row = lambda i: (i, 0)
    sds = jax.ShapeDtypeStruct
    return pl.pallas_call(
        _inproj_e_kernel,
        grid=(T_ALL // tm,),
        in_specs=[
            pl.BlockSpec((tm, D_MODEL), row),
            pl.BlockSpec((None, 1, N_MOD), lambda i: (cidx(i), 0, 0)),
            pl.BlockSpec((1, D_MODEL), lambda i: (0, 0)),
            pl.BlockSpec((D_MODEL, 6 * HALF), lambda i: (0, 0)),
        ],
        out_specs=[
            pl.BlockSpec((tm, 3 * HALF), row),
            pl.BlockSpec((tm, HALF), row),
            pl.BlockSpec((tm, HALF), row),
            pl.BlockSpec((tm, HALF), row),
            pl.BlockSpec((tm, HALF), row),
            pl.BlockSpec((tm, HALF), row),
        ],
        out_shape=[
            sds((T_ALL, 3 * HALF), F32),
            sds((T_ALL, HALF), BF16),
            sds((T_ALL, HALF), F32),
            sds((T_ALL, HALF), F32),
            sds((T_ALL, HALF), BF16),
            sds((T_ALL, HALF), BF16),
        ],
        compiler_params=_cparams("parallel"),
        name="in_proj_even",
    )(x, mod, g, w)


def _softmax_parts(parts):
    m = parts[0].max(axis=-1, keepdims=True)
    for s in parts[1:]:
        m = jnp.maximum(m, s.max(axis=-1, keepdims=True))
    es = [jnp.exp(s - m) for s in parts]
    l = es[0].sum(axis=-1, keepdims=True)
    for e in es[1:]:
        l = l + e.sum(axis=-1, keepdims=True)
    inv = 1.0 / l
    return [e * inv for e in es]


def _attn_ctx_kernel(q_ref, k_ref, v_ref, o_ref):
    for h in range(N_HEADS):
        sl = slice(h * HEAD_DIM, (h + 1) * HEAD_DIM)
        s = _dot_nt(q_ref[:, sl], k_ref[:, sl]) * ATT_SCALE
        (p,) = _softmax_parts([s])
        o_ref[:, sl] = _dot(p.astype(BF16), v_ref[:, sl]).astype(BF16)


def _attn_ctx(q, kb, vb):
    blk = pl.BlockSpec((SEQ, HALF), lambda b: (b, 0))
    return pl.pallas_call(
        _attn_ctx_kernel,
        grid=(BATCH,),
        in_specs=[blk, blk, blk],
        out_specs=blk,
        out_shape=jax.ShapeDtypeStruct((T_CTX, HALF), BF16),
        compiler_params=_cparams("parallel"),
        name="attn_context",
    )(q, kb, vb)


NA_Q_ROWS = 4
NA_K_ROWS = NA_WIN_R + NA_Q_ROWS


def _na_key_row0(rb):
    return jnp.clip(rb * NA_Q_ROWS - NA_WIN_R // 2, 0, GRID_H - NA_K_ROWS)


def _attn_na_kernel(q_ref, k_ref, v_ref, kc_ref, vc_ref, bias_ref, o_ref):
    rb = pl.program_id(1)
    start = pl.multiple_of(_na_key_row0(rb) * GRID_W, GRID_W)
    n_loc = NA_K_ROWS * GRID_W
    for h in range(N_HEADS):
        sl = slice(h * HEAD_DIM, (h + 1) * HEAD_DIM)
        q = q_ref[:, sl]
        s_loc = _dot_nt(q, k_ref[pl.ds(start, n_loc), sl]) * ATT_SCALE + bias_ref[h]
        s_ctx = _dot_nt(q, kc_ref[:, sl].astype(BF16)) * ATT_SCALE
        p_loc, p_ctx = _softmax_parts([s_loc, s_ctx])
        o = _dot(p_loc.astype(BF16), v_ref[pl.ds(start, n_loc), sl])
        o = o + _dot(p_ctx.astype(BF16), vc_ref[:, sl].astype(BF16))
        o_ref[:, sl] = o.astype(BF16)


def _na_bias_table(rpb):
    cols = np.arange(GRID_W)
    col_start = np.clip(cols - NA_WIN_C // 2, 0, GRID_W - NA_WIN_C)
    col_in = (cols[None, :] >= col_start[:, None]) & (cols[None, :] < col_start[:, None] + NA_WIN_C)
    col_off = np.clip(cols[None, :] - cols[:, None], 1 - NA_WIN_C, NA_WIN_C - 1) + (NA_WIN_C - 1)
    n_off = 2 * NA_WIN_C - 1
    pick = jnp.asarray(col_off[None] == np.arange(n_off)[:, None, None], F32)
    by_col = (rpb.astype(F32)[:, :, :, None, None] * pick[None, None]).sum(axis=2)
    by_col = jnp.pad(by_col, ((0, 0), (NA_K_ROWS, NA_K_ROWS), (0, 0), (0, 0)))
    n_blocks = GRID_H // NA_Q_ROWS
    variants = []
    for rb in (0, 1, n_blocks - 1):
        k_row0 = int(np.clip(rb * NA_Q_ROWS - NA_WIN_R // 2, 0, GRID_H - NA_K_ROWS))
        k_row = k_row0 + np.arange(NA_K_ROWS)
        per_q_row = []
        for qr in range(NA_Q_ROWS):
            q_row = rb * NA_Q_ROWS + qr
            win0 = int(np.clip(q_row - NA_WIN_R // 2, 0, GRID_H - NA_WIN_R))
            row_in = (k_row >= win0) & (k_row < win0 + NA_WIN_R)
            ro0 = k_row0 - q_row + (NA_WIN_R - 1) + NA_K_ROWS
            b = by_col[:, ro0:ro0 + NA_K_ROWS]
            keep = row_in[:, None, None] & col_in[None]
            per_q_row.append(jnp.where(keep[None], b, NEG_INF))
        b = jnp.stack(per_q_row, axis=1)
        variants.append(b.transpose(0, 1, 3, 2, 4).reshape(N_HEADS, NA_Q_ROWS * GRID_W, NA_K_ROWS * GRID_W))
    return jnp.stack(variants)


def _attn_na(q, kb, vb, cache_k, cache_v, bias):
    tq = NA_Q_ROWS * GRID_W
    n_blocks = GRID_H // NA_Q_ROWS
    ctx_q_tiles = T_CTX // tq
    ctx_b_tiles = T_CTX // DEC_SEQ

    def bias_idx(b, rb):
        return (jnp.where(rb == 0, 0, jnp.where(rb == n_blocks - 1, 2, 1)), 0, 0, 0)

    full = pl.BlockSpec((DEC_SEQ, HALF), lambda b, r: (ctx_b_tiles + b, 0))
    cache = pl.BlockSpec((None, PAST_LEN, HALF), lambda b, r: (b, 0, 0))
    return pl.pallas_call(
        _attn_na_kernel,
        grid=(DEC_BATCH, n_blocks),
        in_specs=[
            pl.BlockSpec((tq, HALF), lambda b, r: (ctx_q_tiles + b * n_blocks + r, 0)),
            full, full, cache, cache,
            pl.BlockSpec((None, N_HEADS, tq, NA_K_ROWS * GRID_W), bias_idx),
        ],
        out_specs=pl.BlockSpec((tq, HALF), lambda b, r: (b * n_blocks + r, 0)),
        out_shape=jax.ShapeDtypeStruct((T_SMP, HALF), BF16),
        compiler_params=_cparams("parallel", "arbitrary"),
        name="attn_neighbourhood",
    )(q, kb, vb, cache_k, cache_v, bias)


def _seq_edges(i, tm):
    n_ctx = T_CTX // tm
    ctx_per = SEQ // tm
    smp_per = DEC_SEQ // tm
    j = i - n_ctx
    first = jnp.where(i < n_ctx, i % ctx_per == 0, j % smp_per == 0)
    last = jnp.where(i < n_ctx, i % ctx_per == ctx_per - 1, j % smp_per == smp_per - 1)
    return first, last


def _outproj_e_kernel(bcu_ref, prev_ref, next_ref, oc_ref, os_ref, x_ref, mod_ref, wc_ref, w_ref, o_ref, *, tm):
    i = pl.program_id(0)
    first, last = _seq_edges(i, tm)
    bcu = bcu_ref[...]
    cu = bcu[:, HALF:2 * HALF] * bcu[:, 2 * HALF:3 * HALF]
    pv = prev_ref[7:8, :]
    nx = next_ref[0:1, :]
    cu_prev_row = jnp.where(first, 0.0, pv[:, HALF:2 * HALF] * pv[:, 2 * HALF:3 * HALF])
    cu_next_row = jnp.where(last, 0.0, nx[:, HALF:2 * HALF] * nx[:, 2 * HALF:3 * HALF])
    rows = lax.broadcasted_iota(jnp.int32, (tm, HALF), 0)
    cu_prev = jnp.where(rows == 0, cu_prev_row, pltpu.roll(cu, 1, axis=0))
    cu_next = jnp.where(rows == tm - 1, cu_next_row, pltpu.roll(cu, tm - 1, axis=0))
    wc = wc_ref[...]
    y_a = bcu[:, 0:HALF] * (cu_prev * wc[0:1] + cu * wc[1:2] + cu_next * wc[2:3])
    y_b = jnp.where(i < T_CTX // tm, oc_ref[...], os_ref[...])
    y = _dot(y_a.astype(BF16), w_ref[0:HALF, :]) + _dot(y_b, w_ref[HALF:D_MODEL, :])
    o_ref[...] = x_ref[...] + mod_ref[:, 2 * D_MODEL:3 * D_MODEL] * y


def _out_proj_e(bcu, o_ctx, o_smp, x, mod, w_conv_t, w):
    tm = 256
    n_ctx = T_CTX // tm
    cidx = _cond_index(tm)
    row = lambda i: (i, 0)
    nb8 = T_ALL // 8
    return pl.pallas_call(
        functools.partial(_outproj_e_kernel, tm=tm),
        grid=(T_ALL // tm,),
        in_specs=[
            pl.BlockSpec((tm, 3 * HALF), row),
            pl.BlockSpec((8, 3 * HALF), lambda i: (jnp.maximum(i * (tm // 8) - 1, 0), 0)),
            pl.BlockSpec((8, 3 * HALF), lambda i: (jnp.minimum((i + 1) * (tm // 8), nb8 - 1), 0)),
            pl.BlockSpec((tm, HALF), lambda i: (jnp.minimum(i, n_ctx - 1), 0)),
            pl.BlockSpec((tm, HALF), lambda i: (jnp.maximum(i - n_ctx, 0), 0)),
            pl.BlockSpec((tm, D_MODEL), row),
            pl.BlockSpec((None, 1, N_MOD), lambda i: (cidx(i), 0, 0)),
            pl.BlockSpec((3, HALF), lambda i: (0, 0)),
            pl.BlockSpec((D_MODEL, D_MODEL), lambda i: (0, 0)),
        ],
        out_specs=pl.BlockSpec((tm, D_MODEL), row),
        out_shape=jax.ShapeDtypeStruct((T_ALL, D_MODEL), F32),
        compiler_params=_cparams("parallel"),
        name="out_proj_even",
    )(bcu, bcu, bcu, o_ctx, o_smp, x, mod, w_conv_t, w)


def _ffn_kernel(x_ref, mod_ref, g_ref, wg_ref, wu_ref, wd_ref, o_ref, h_ref, acc_ref):
    f = pl.program_id(1)

    @pl.when(f == 0)
    def _():
        m = mod_ref[...]
        h = _norm_mod(x_ref[...], g_ref[...], m[:, 4 * D_MODEL:5 * D_MODEL], m[:, 3 * D_MODEL:4 * D_MODEL])
        h_ref[...] = h.astype(BF16)
        acc_ref[...] = jnp.zeros_like(acc_ref)

    h = h_ref[...]
    a = _silu(_dot(h, wg_ref[...])) * _dot(h, wu_ref[...])
    acc_ref[...] += _dot(a.astype(BF16), wd_ref[...])

    @pl.when(f == pl.num_programs(1) - 1)
    def _():
        o_ref[...] = x_ref[...] + mod_ref[:, 5 * D_MODEL:6 * D_MODEL] * acc_ref[...]


def _ffn(x, mod, g, wg, wu, wd):
    tm = 512
    tf = D_FF // 2
    cidx = _cond_index(tm)
    return pl.pallas_call(
        _ffn_kernel,
        grid=(T_ALL // tm, D_FF // tf),
        in_specs=[
            pl.BlockSpec((tm, D_MODEL), lambda i, f: (i, 0)),
            pl.BlockSpec((None, 1, N_MOD), lambda i, f: (cidx(i), 0, 0)),
            pl.BlockSpec((1, D_MODEL), lambda i, f: (0, 0)),
            pl.BlockSpec((D_MODEL, tf), lambda i, f: (0, f)),
            pl.BlockSpec((D_MODEL, tf), lambda i, f: (0, f)),
            pl.BlockSpec((tf, D_MODEL), lambda i, f: (f, 0)),
        ],
        out_specs=pl.BlockSpec((tm, D_MODEL), lambda i, f: (i, 0)),
        out_shape=jax.ShapeDtypeStruct((T_ALL, D_MODEL), F32),
        scratch_shapes=[pltpu.VMEM((tm, D_MODEL), BF16), pltpu.VMEM((tm, D_MODEL), F32)],
        compiler_params=_cparams("parallel", "arbitrary"),
        name="ffn_dense",
    )(x, mod, g, wg, wu, wd)


def _inproj_o_kernel(x_ref, mod_ref, g_ref, w_ref, mch_ref, mcl_ref, ar_ref, ai_ref, q_ref, k_ref, v_ref, gt_ref):
    m = mod_ref[...]
    h = _norm_mod(x_ref[...], g_ref[...], m[:, D_MODEL:2 * D_MODEL], m[:, 0:D_MODEL]).astype(BF16)
    u_hi, u_lo = _split(_dot(h, w_ref[:, 0:HALF]))
    a = _dot3(u_hi, u_lo, mch_ref[...], mcl_ref[...])
    ar_ref[...] = a[:, 0:HALF]
    ai_ref[...] = a[:, HALF:2 * HALF]
    q_ref[...] = _dot(h, w_ref[:, HALF:2 * HALF]).astype(BF16)
    k_ref[...] = _dot(h, w_ref[:, 2 * HALF:3 * HALF]) * ATT_SCALE
    v_ref[...] = _dot(h, w_ref[:, 3 * HALF:4 * HALF]).astype(BF16)
    gt_ref[...] = _dot(h, w_ref[:, 4 * HALF:5 * HALF])


def _in_proj_o(x, mod, g, w, mc_hi, mc_lo):
    tm = 512
    cidx = _cond_index(tm)
    row = lambda i: (i, 0)
    half_out = pl.BlockSpec((tm, HALF), row)
    sds = jax.ShapeDtypeStruct
    return pl.pallas_call(
        _inproj_o_kernel,
        grid=(T_ALL // tm,),
        in_specs=[
            pl.BlockSpec((tm, D_MODEL), row),
            pl.BlockSpec((None, 1, N_MOD), lambda i: (cidx(i), 0, 0)),
            pl.BlockSpec((1, D_MODEL), lambda i: (0, 0)),
            pl.BlockSpec((D_MODEL, 5 * HALF), lambda i: (0, 0)),
            pl.BlockSpec((HALF, 2 * HALF), lambda i: (0, 0)),
            pl.BlockSpec((HALF, 2 * HALF), lambda i: (0, 0)),
        ],
        out_specs=[half_out] * 6,
        out_shape=[
            sds((T_ALL, HALF), F32),
            sds((T_ALL, HALF), F32),
            sds((T_ALL, HALF), BF16),
            sds((T_ALL, HALF), F32),
            sds((T_ALL, HALF), BF16),
            sds((T_ALL, HALF), F32),
        ],
        compiler_params=_cparams("parallel"),
        name="in_proj_odd",
    )(x, mod, g, w, mc_hi, mc_lo)


def _dft_cos_sin(n):
    k = np.arange(n)
    ang = 2.0 * np.pi * ((k[:, None] * k[None, :]) % n) / n
    return np.cos(ang), np.sin(ang)


def _hi_lo(a):
    a = jnp.asarray(a, F32)
    hi = a.astype(BF16)
    return hi, (a - hi.astype(F32)).astype(BF16)


def _channel_dft_table():
    c, s = _dft_cos_sin(FT_GROUP_CH)
    scale = FT_GROUP_CH ** -0.5
    eye = np.eye(FT_GROUPS)
    return np.concatenate([np.kron(eye, c * scale), np.kron(eye, -s * scale)], axis=1)


def _fourier_ctx_kernel(ar_ref, ai_ref, th_ref, tl_ref, o_ref):
    a_hi, a_lo = _split(jnp.concatenate([ar_ref[...], ai_ref[...]], axis=0))
    o_ref[...] = _dot3(th_ref[...], tl_ref[...], a_hi, a_lo).astype(BF16)


def _fourier_ctx(ar, ai):
    c, s = _dft_cos_sin(SEQ)
    t_hi, t_lo = _hi_lo(np.concatenate([c, s], axis=1) * SEQ ** -0.5)
    blk = pl.BlockSpec((SEQ, HALF), lambda b: (b, 0))
    tab = pl.BlockSpec((SEQ, 2 * SEQ), lambda b: (0, 0))
    return pl.pallas_call(
        _fourier_ctx_kernel,
        grid=(BATCH,),
        in_specs=[blk, blk, tab, tab],
        out_specs=blk,
        out_shape=jax.ShapeDtypeStruct((T_CTX, HALF), BF16),
        compiler_params=_cparams("parallel"),
        name="fourier_context",
    )(ar, ai, t_hi, t_lo)


def _fourier_s1_kernel(ar_ref, ai_ref, mh_ref, ml_ref, yr_ref, yi_ref):
    a_hi, a_lo = _split(jnp.concatenate([ar_ref[...], ai_ref[...]], axis=0))
    y = _dot3(mh_ref[...], ml_ref[...], a_hi, a_lo)
    yr_ref[...] = y[0:FT_N1]
    yi_ref[...] = y[FT_N1:2 * FT_N1]


def _fourier_s3_kernel(yr_ref, yi_ref, tc_ref, ts_ref, mh_ref, ml_ref, o_ref, *, nk):
    lanes = HALF // tc_ref.shape[-1]
    for j in range(nk):
        tc = jnp.concatenate([tc_ref[j]] * lanes, axis=-1)
        ts = jnp.concatenate([ts_ref[j]] * lanes, axis=-1)
        yr = yr_ref[j]
        yi = yi_ref[j]
        z = jnp.concatenate([yr * tc + yi * ts, yi * tc - yr * ts], axis=0)
        z_hi, z_lo = _split(z)
        o_ref[:, j * HALF:(j + 1) * HALF] = _dot3(mh_ref[...], ml_ref[...], z_hi, z_lo).astype(BF16)


def _fourier_smp(ar, ai):
    n1 = FT_N1
    wide = n1 * HALF
    c, s = _dft_cos_sin(n1)
    m1_hi, m1_lo = _hi_lo(np.block([[c, s], [-s, c]]))
    m3_hi, m3_lo = _hi_lo(np.concatenate([c, s], axis=1) / n1)
    kk = np.arange(n1)
    ang = 2.0 * np.pi * (kk[:, None] * kk[None, :]) / DEC_SEQ
    tw_c = jnp.broadcast_to(jnp.asarray(np.cos(ang), F32)[:, :, None], (n1, n1, 128))
    tw_s = jnp.broadcast_to(jnp.asarray(np.sin(ang), F32)[:, :, None], (n1, n1, 128))

    ncol = 4096
    ctx_tiles = T_CTX // n1 // n1
    a_blk = pl.BlockSpec((n1, ncol), lambda b, j: (ctx_tiles + b, j))
    y_blk = pl.BlockSpec((n1, ncol), lambda b, j: (b, j))
    m1_blk = pl.BlockSpec((2 * n1, 2 * n1), lambda b, j: (0, 0))
    y_sds = jax.ShapeDtypeStruct((DEC_BATCH * n1, wide), F32)
    yr, yi = pl.pallas_call(
        _fourier_s1_kernel,
        grid=(DEC_BATCH, wide // ncol),
        in_specs=[a_blk, a_blk, m1_blk, m1_blk],
        out_specs=[y_blk, y_blk],
        out_shape=[y_sds, y_sds],
        compiler_params=_cparams("parallel", "parallel"),
        name="fourier_latent_stage1",
    )(ar.reshape(T_ALL // n1, wide), ai.reshape(T_ALL // n1, wide), m1_hi, m1_lo)

    nk = 8
    z_blk = pl.BlockSpec((nk, n1, HALF), lambda b, k: (b * (n1 // nk) + k, 0, 0))
    tw_blk = pl.BlockSpec((nk, n1, 128), lambda b, k: (k, 0, 0))
    m3_blk = pl.BlockSpec((n1, 2 * n1), lambda b, k: (0, 0))
    out = pl.pallas_call(
        functools.partial(_fourier_s3_kernel, nk=nk),
        grid=(DEC_BATCH, n1 // nk),
        in_specs=[z_blk, z_blk, tw_blk, tw_blk, m3_blk, m3_blk],
        out_specs=pl.BlockSpec((n1, nk * HALF), lambda b, k: (b, k)),
        out_shape=jax.ShapeDtypeStruct((DEC_BATCH * n1, wide), BF16),
        compiler_params=_cparams("parallel", "parallel"),
        name="fourier_latent_stage3",
    )(yr.reshape(DEC_BATCH * n1, n1, HALF), yi.reshape(DEC_BATCH * n1, n1, HALF), tw_c, tw_s, m3_hi, m3_lo)
    return out.reshape(T_SMP, HALF)


def _retention_kernel(lg_ref, q_ref, k_ref, v_ref, g_ref, *rest, seq, nh, has_s0):
    if has_s0:
        s0_ref, y_ref, of_ref, ob_ref, dec_ref, xi_ref, zeta_ref = rest
        st_ref = None
    else:
        y_ref, st_ref, of_ref, ob_ref, dec_ref, xi_ref, zeta_ref = rest
        s0_ref = None
    L = RET_CHUNK
    nc = seq // L
    hb = pl.program_id(1)
    diff = (lax.broadcasted_iota(jnp.int32, (L, L), 0) - lax.broadcasted_iota(jnp.int32, (L, L), 1)).astype(F32)
    li = lax.broadcasted_iota(jnp.int32, (L, HEAD_DIM), 0).astype(F32)
    one = jnp.ones((1, 1), F32)

    gcs = []
    for hh in range(nh):
        lgf = lg_ref[0, hb * nh + hh]
        lgb = lg_ref[1, hb * nh + hh]
        dec_ref[2 * hh] = jnp.where(diff >= 0, jnp.exp(lgf * jnp.maximum(diff, 0.0)), 0.0)
        dec_ref[2 * hh + 1] = jnp.where(diff <= 0, jnp.exp(lgb * jnp.maximum(-diff, 0.0)), 0.0)
        xi_ref[2 * hh] = jnp.exp(lgf * (li + 1.0))
        xi_ref[2 * hh + 1] = jnp.exp(lgb * (L - li))
        zeta_ref[2 * hh] = jnp.exp(lgf * (L - 1.0 - li))
        zeta_ref[2 * hh + 1] = jnp.exp(lgb * li)
        gcs.append(jnp.exp(one * (lgf * L)))
        gcs.append(jnp.exp(one * (lgb * L)))

    def chunk(c, s, t, sl, o_ref):
        r0 = pl.multiple_of(c * L, L)
        qc = q_ref[pl.ds(r0, L), sl]
        kc = k_ref[pl.ds(r0, L), sl]
        vc = v_ref[pl.ds(r0, L), sl]
        inner = _dot_nt(qc, kc.astype(BF16)) * dec_ref[t]
        o_ref[pl.ds(r0, L), sl] = _dot(inner.astype(BF16), vc) + _dot(qc, s.astype(BF16)) * xi_ref[t]
        return s * gcs[t] + _dot((kc * zeta_ref[t]).T.astype(BF16), vc)

    def scan_step(i, states):
        out = []
        for hh in range(nh):
            sl = slice(hh * HEAD_DIM, (hh + 1) * HEAD_DIM)
            out.append(chunk(i, states[2 * hh], 2 * hh, sl, of_ref))
            out.append(chunk(nc - 1 - i, states[2 * hh + 1], 2 * hh + 1, sl, ob_ref))
        return tuple(out)

    if has_s0:
        init = tuple(s0_ref[t % 2, t // 2] for t in range(2 * nh))
    else:
        init = tuple(jnp.zeros((HEAD_DIM, HEAD_DIM), F32) for _ in range(2 * nh))
    final = lax.fori_loop(0, nc, scan_step, init)
    if st_ref is not None:
        for t in range(2 * nh):
            st_ref[t % 2, t // 2] = final[t]

    def finish(c, carry):
        r0 = pl.multiple_of(c * L, L)
        o_all = of_ref[pl.ds(r0, L), :] + ob_ref[pl.ds(r0, L), :]
        gate = _silu(g_ref[pl.ds(r0, L), :])
        for hh in range(nh):
            sl = slice(hh * HEAD_DIM, (hh + 1) * HEAD_DIM)
            o = o_all[:, sl]
            mu = jnp.mean(o, axis=-1, keepdims=True)
            var = jnp.mean(jnp.square(o - mu), axis=-1, keepdims=True)
            y_ref[pl.ds(r0, L), sl] = (gate[:, sl] * ((o - mu) * lax.rsqrt(var + EPS))).astype(BF16)
        return carry

    lax.fori_loop(0, nc, finish, 0)


def _retention(lg, q, k, v, g, s0, *, seq, nbatch, row0, nh):
    has_s0 = s0 is not None
    tile0 = row0 // seq
    width = nh * HEAD_DIM
    blk = pl.BlockSpec((seq, width), lambda b, hb: (tile0 + b, hb))
    st_blk = pl.BlockSpec((None, 2, nh, HEAD_DIM, HEAD_DIM), lambda b, hb: (b, 0, hb, 0, 0))
    in_specs = [pl.BlockSpec(memory_space=pltpu.SMEM), blk, blk, blk, blk]
    args = [lg, q, k, v, g]
    y_spec = pl.BlockSpec((seq, width), lambda b, hb: (b, hb))
    y_sds = jax.ShapeDtypeStruct((nbatch * seq, HALF), BF16)
    if has_s0:
        in_specs.append(st_blk)
        args.append(s0)
        out_specs, out_shape = y_spec, y_sds
    else:
        out_specs = [y_spec, st_blk]
        out_shape = [y_sds, jax.ShapeDtypeStruct((nbatch, 2, N_HEADS, HEAD_DIM, HEAD_DIM), F32)]
    return pl.pallas_call(
        functools.partial(_retention_kernel, seq=seq, nh=nh, has_s0=has_s0),
        grid=(nbatch, N_HEADS // nh),
        in_specs=in_specs,
        out_specs=out_specs,
        out_shape=out_shape,
        scratch_shapes=[
            pltpu.VMEM((seq, width), F32),
            pltpu.VMEM((seq, width), F32),
            pltpu.VMEM((2 * nh, RET_CHUNK, RET_CHUNK), F32),
            pltpu.VMEM((2 * nh, RET_CHUNK, HEAD_DIM), F32),
            pltpu.VMEM((2 * nh, RET_CHUNK, HEAD_DIM), F32),
        ],
        compiler_params=_cparams("parallel", "parallel"),
        name="retention_%d" % seq,
    )(*args)


def _outproj_o_kernel(ycc_ref, ycs_ref, ydc_ref, yds_ref, x_ref, mod_ref, w_ref, g_ref, rh_ref, rl_ref,
                      x3_ref, h_ref, e1_ref, e2_ref, ga_ref, gb_ref, *, tm):
    is_ctx = pl.program_id(0) < T_CTX // tm
    y_c = jnp.where(is_ctx, ycc_ref[...], ycs_ref[...])
    y_d = jnp.where(is_ctx, ydc_ref[...], yds_ref[...])
    y = _dot(y_c, w_ref[0:HALF, :]) + _dot(y_d, w_ref[HALF:D_MODEL, :])
    m = mod_ref[...]
    x3 = x_ref[...] + m[:, 2 * D_MODEL:3 * D_MODEL] * y
    x3_ref[...] = x3
    h = _norm_mod(x3, g_ref[...], m[:, 4 * D_MODEL:5 * D_MODEL], m[:, 3 * D_MODEL:4 * D_MODEL])
    _store_row_tiles(h_ref, h)
    h_hi, h_lo = _split(h)
    logits = _dot3(h_hi, h_lo, rh_ref[...], rl_ref[...])
    idx = lax.broadcasted_iota(jnp.int32, logits.shape, 1).astype(F32)
    logits = jnp.where(idx < float(N_EXPERTS), logits, -jnp.inf)
    m1 = logits.max(axis=-1, keepdims=True)
    e1 = jnp.where(logits == m1, idx, float(N_EXPERTS)).min(axis=-1, keepdims=True)
    rest = jnp.where(idx == e1, -jnp.inf, logits)
    m2 = rest.max(axis=-1, keepdims=True)
    e2 = jnp.where(rest == m2, idx, float(N_EXPERTS)).min(axis=-1, keepdims=True)
    ex = jnp.exp(m2 - m1)
    den = 1.0 + ex
    wide = (tm, 128)
    e1_ref[...] = jnp.broadcast_to(e1, wide).astype(jnp.int32)
    e2_ref[...] = jnp.broadcast_to(e2, wide).astype(jnp.int32)
    ga_ref[...] = jnp.broadcast_to(1.0 / den, wide)
    gb_ref[...] = jnp.broadcast_to(ex / den, wide)


def _out_proj_o(yc_ctx, yc_smp, yd_ctx, yd_smp, x, mod, w, g, r_hi, r_lo):
    tm = 256
    n_ctx = T_CTX // tm
    cidx = _cond_index(tm)
    row = lambda i: (i, 0)
    ctx_blk = pl.BlockSpec((tm, HALF), lambda i: (jnp.minimum(i, n_ctx - 1), 0))
    smp_blk = pl.BlockSpec((tm, HALF), lambda i: (jnp.maximum(i - n_ctx, 0), 0))
    sds = jax.ShapeDtypeStruct
    rep = pl.BlockSpec((tm, 128), row)
    return pl.pallas_call(
        functools.partial(_outproj_o_kernel, tm=tm),
        grid=(T_ALL // tm,),
        in_specs=[
            ctx_blk, smp_blk, ctx_blk, smp_blk,
            pl.BlockSpec((tm, D_MODEL), row),
            pl.BlockSpec((None, 1, N_MOD), lambda i: (cidx(i), 0, 0)),
            pl.BlockSpec((D_MODEL, D_MODEL), lambda i: (0, 0)),
            pl.BlockSpec((1, D_MODEL), lambda i: (0, 0)),
            pl.BlockSpec((D_MODEL, 128), lambda i: (0, 0)),
            pl.BlockSpec((D_MODEL, 128), lambda i: (0, 0)),
        ],
        out_specs=[pl.BlockSpec((tm, D_MODEL), row), pl.BlockSpec((tm,) + ROW_TILE, lambda i: (i, 0, 0)),
                   rep, rep, rep, rep],
        out_shape=[
            sds((T_ALL, D_MODEL), F32), sds((T_ALL,) + ROW_TILE, F32),
            sds((T_ALL, 128), jnp.int32), sds((T_ALL, 128), jnp.int32),
            sds((T_ALL, 128), F32), sds((T_ALL, 128), F32),
        ],
        compiler_params=_cparams("parallel"),
        name="out_proj_odd_route",
    )(yc_ctx, yc_smp, yd_ctx, yd_smp, x, mod, w, g, r_hi, r_lo)


def _routing_tables(e1, e2):
    flat_e = jnp.concatenate([e1, e2])
    onehot = (flat_e[:, None] == jnp.arange(N_EXPERTS, dtype=jnp.int32)[None, :]).astype(jnp.int32)
    csum = jnp.cumsum(onehot, axis=0)
    counts = csum[-1]
    padded = (counts + MOE_TM - 1) // MOE_TM * MOE_TM
    ends = jnp.cumsum(padded)
    pos = jnp.sum(onehot * (csum - 1 + (ends - padded)[None, :]), axis=1)
    tok = jnp.tile(jnp.arange(T_ALL, dtype=jnp.int32), 2)
    slot_t = jnp.zeros((MOE_P,), jnp.int32).at[pos].set(tok)
    block_row0 = jnp.arange(MOE_NB, dtype=jnp.int32) * MOE_TM
    block_e = jnp.minimum(
        jnp.sum((ends[None, :] <= block_row0[:, None]).astype(jnp.int32), axis=1), N_EXPERTS - 1
    ).astype(jnp.int32)
    n_valid = (ends[-1] // MOE_TM).astype(jnp.int32).reshape(1)
    return slot_t, pos[:T_ALL].astype(jnp.int32), pos[T_ALL:].astype(jnp.int32), block_e, n_valid


def _row_copy(src_ref, dst_ref, sem, src_row, dst_row):
    return pltpu.make_async_copy(src_ref.at[pl.ds(src_row, 1)], dst_ref.at[pl.ds(dst_row, 1)], sem)


def _gather_kernel(idx_ref, src_ref, o_ref, sem, *, tm):
    def start(r, c):
        _row_copy(src_ref, o_ref, sem, idx_ref[0, r], r).start()
        return c

    def wait(r, c):
        _row_copy(src_ref, o_ref, sem, 0, r).wait()
        return c

    lax.fori_loop(0, tm, start, 0, unroll=8)
    lax.fori_loop(0, tm, wait, 0, unroll=8)


def _gather_rows(src, slot_t):
    tm = GATHER_TM
    nblk = MOE_P // tm
    return pl.pallas_call(
        functools.partial(_gather_kernel, tm=tm),
        grid=(nblk,),
        in_specs=[
            pl.BlockSpec((None, 1, tm), lambda i: (i, 0, 0), memory_space=pltpu.SMEM),
            pl.BlockSpec(memory_space=pl.ANY),
        ],
        out_specs=pl.BlockSpec((tm,) + ROW_TILE, lambda i: (i, 0, 0)),
        out_shape=jax.ShapeDtypeStruct((MOE_P,) + ROW_TILE, F32),
        scratch_shapes=[pltpu.SemaphoreType.DMA(())],
        compiler_params=_cparams("arbitrary"),
        name="moe_gather",
    )(slot_t.reshape(nblk, 1, tm), src)


def _experts_kernel(be_ref, nv_ref, x_ref, wg_ref, wu_ref, wd_ref, o_ref, xb_ref, acc_ref):
    i = pl.program_id(0)
    f = pl.program_id(1)
    valid = i < nv_ref[0]

    @pl.when(jnp.logical_and(valid, f == 0))
    def _():
        xb_ref[...] = _load_row_tiles(x_ref).astype(BF16)
        acc_ref[...] = jnp.zeros_like(acc_ref)

    @pl.when(valid)
    def _():
        x = xb_ref[...]
        a = _silu(_dot(x, wg_ref[...])) * _dot(x, wu_ref[...])
        acc_ref[...] += _dot(a.astype(BF16), wd_ref[...])

    last = f == pl.num_programs(1) - 1

    @pl.when(jnp.logical_and(valid, last))
    def _():
        _store_row_tiles(o_ref, acc_ref[...])

    @pl.when(jnp.logical_and(jnp.logical_not(valid), last))
    def _():
        o_ref[...] = jnp.zeros_like(o_ref)


def _experts(xs, block_e, n_valid, wg, wu, wd):
    nf = D_FF_EXPERT // MOE_TF

    def f_eff(i, f, nv):
        return jnp.where(i < nv[0], f, nf - 1)

    return pl.pallas_call(
        _experts_kernel,
        grid_spec=pltpu.PrefetchScalarGridSpec(
            num_scalar_prefetch=2,
            grid=(MOE_NB, nf),
            in_specs=[
                pl.BlockSpec((MOE_TM,) + ROW_TILE, lambda i, f, be, nv: (i, 0, 0)),
                pl.BlockSpec((None, D_MODEL, MOE_TF), lambda i, f, be, nv: (be[i], 0, f_eff(i, f, nv))),
                pl.BlockSpec((None, D_MODEL, MOE_TF), lambda i, f, be, nv: (be[i], 0, f_eff(i, f, nv))),
                pl.BlockSpec((None, MOE_TF, D_MODEL), lambda i, f, be, nv: (be[i], f_eff(i, f, nv), 0)),
            ],
            out_specs=pl.BlockSpec((MOE_TM,) + ROW_TILE, lambda i, f, be, nv: (i, 0, 0)),
            scratch_shapes=[pltpu.VMEM((MOE_TM, D_MODEL), BF16), pltpu.VMEM((MOE_TM, D_MODEL), F32)],
        ),
        out_shape=jax.ShapeDtypeStruct((MOE_P,) + ROW_TILE, F32),
        compiler_params=_cparams("arbitrary", "arbitrary"),
        name="moe_experts",
    )(block_e, n_valid, xs, wg, wu, wd)


def _combine_kernel(p1_ref, p2_ref, ys_ref, x_ref, ga_ref, gb_ref, mod_ref, g_ref, o_ref, a_ref, b_ref, sems, *, tm):
    def start(r, c):
        _row_copy(ys_ref, a_ref, sems.at[0], p1_ref[0, r], r).start()
        _row_copy(ys_ref, b_ref, sems.at[1], p2_ref[0, r], r).start()
        return c

    def wait(r, c):
        _row_copy(ys_ref, a_ref, sems.at[0], 0, r).wait()
        _row_copy(ys_ref, b_ref, sems.at[1], 0, r).wait()
        return c

    lax.fori_loop(0, tm, start, 0, unroll=8)
    lax.fori_loop(0, tm, wait, 0, unroll=8)
    ga = ga_ref[...]
    gb = gb_ref[...]
    f = jnp.concatenate([a_ref[:, j, :] * ga + b_ref[:, j, :] * gb for j in range(ROW_TILE[0])], axis=-1)
    x = x_ref[...] + mod_ref[:, 5 * D_MODEL:6 * D_MODEL] * f
    o_ref[...] = (x * lax.rsqrt(jnp.mean(x * x, axis=-1, keepdims=True) + EPS)) * g_ref[...]


def _combine(ys, pos1, pos2, x, ga, gb, mod, final_g):
    tm = GATHER_TM
    nblk = T_ALL // tm
    cidx = _cond_index(tm)
    row = lambda i: (i, 0)
    idx_blk = pl.BlockSpec((None, 1, tm), lambda i: (i, 0, 0), memory_space=pltpu.SMEM)
    return pl.pallas_call(
        functools.partial(_combine_kernel, tm=tm),
        grid=(nblk,),
        in_specs=[
            idx_blk, idx_blk,
            pl.BlockSpec(memory_space=pl.ANY),
            pl.BlockSpec((tm, D_MODEL), row),
            pl.BlockSpec((tm, 128), row),
            pl.BlockSpec((tm, 128), row),
            pl.BlockSpec((None, 1, N_MOD), lambda i: (cidx(i), 0, 0)),
            pl.BlockSpec((1, D_MODEL), lambda i: (0, 0)),
        ],
        out_specs=pl.BlockSpec((tm, D_MODEL), row),
        out_shape=jax.ShapeDtypeStruct((T_ALL, D_MODEL), F32),
        scratch_shapes=[
            pltpu.VMEM((tm,) + ROW_TILE, F32),
            pltpu.VMEM((tm,) + ROW_TILE, F32),
            pltpu.SemaphoreType.DMA((2,)),
        ],
        compiler_params=_cparams("arbitrary"),
        name="moe_combine_final_norm",
    )(pos1.reshape(nblk, 1, tm), pos2.reshape(nblk, 1, tm), ys, x, ga, gb, mod, final_g)


def kernel(x_prompt, x_sample, cache_na_k, cache_na_v, state_ret, c, c_ctx, w_ada, b_ada, norm_g, final_g, w_in_e, w_conv_e, na_rpb_e, w_out_e, ff_gate_e, ff_up_e, ff_down_e, w_in_o, ret_decay_o, w_out_o, router_o, ex_gate_o, ex_up_o, ex_down_o):
    x0 = jnp.concatenate([x_prompt.reshape(T_CTX, D_MODEL), x_sample.reshape(T_SMP, D_MODEL)], axis=0)
    cond = jnp.concatenate([c_ctx[None, :], c, jnp.zeros((N_COND - 1 - DEC_BATCH, D_MODEL), F32)], axis=0)
    mod = _ada(cond, w_ada, b_ada)
    mod0 = mod[0].reshape(N_COND, 1, N_MOD)
    mod1 = mod[1].reshape(N_COND, 1, N_MOD)

    bcu, q, k, v, kb, vb = _in_proj_e(x0, mod0, norm_g[0, 0][None, :], w_in_e[0].astype(BF16))
    o_ctx = _attn_ctx(q, kb, vb)
    o_smp = _attn_na(q, kb, vb,
                     cache_na_k[:, 0].reshape(DEC_BATCH, PAST_LEN, HALF),
                     cache_na_v[:, 0].reshape(DEC_BATCH, PAST_LEN, HALF),
                     _na_bias_table(na_rpb_e[0]))
    x1 = _out_proj_e(bcu, o_ctx, o_smp, x0, mod0, w_conv_e[0].T, w_out_e[0].astype(BF16))
    x2 = _ffn(x1, mod0, norm_g[0, 1][None, :], ff_gate_e[0].astype(BF16), ff_up_e[0].astype(BF16),
              ff_down_e[0].astype(BF16))

    mc_hi, mc_lo = _hi_lo(_channel_dft_table())
    ar, ai, q1, k1, v1, g1 = _in_proj_o(x2, mod1, norm_g[1, 0][None, :], w_in_o[0].astype(BF16), mc_hi, mc_lo)
    yc_ctx = _fourier_ctx(ar, ai)
    yc_smp = _fourier_smp(ar, ai)
    lg = jax.nn.log_sigmoid(ret_decay_o[0].astype(F32))
    yd_ctx, new_state = _retention(lg, q1, k1, v1, g1, None, seq=SEQ, nbatch=BATCH, row0=0, nh=8)
    yd_smp = _retention(lg, q1, k1, v1, g1, state_ret[:, 0], seq=DEC_SEQ, nbatch=DEC_BATCH, row0=T_CTX, nh=4)
    r_hi, r_lo = _hi_lo(jnp.pad(router_o[0], ((0, 0), (0, 128 - N_EXPERTS))))
    x3, hm, e1, e2, ga, gb = _out_proj_o(yc_ctx, yc_smp, yd_ctx, yd_smp, x2, mod1, w_out_o[0].astype(BF16),
                                         norm_g[1, 1][None, :], r_hi, r_lo)
    slot_t, pos1, pos2, block_e, n_valid = _routing_tables(e1[:, 0], e2[:, 0])
    xs = _gather_rows(hm, slot_t)
    ys = _experts(xs, block_e, n_valid, ex_gate_o[0].astype(BF16), ex_up_o[0].astype(BF16),
                  ex_down_o[0].astype(BF16))
    y = _combine(ys, pos1, pos2, x3, ga, gb, mod1, final_g[None, :])

    y_prompt = y[:T_CTX].reshape(BATCH, SEQ, D_MODEL)
    y_sample = y[T_CTX:].reshape(DEC_BATCH, DEC_SEQ, D_MODEL)
    new_na_k = k[:T_CTX].reshape(BATCH, 1, SEQ, N_HEADS, HEAD_DIM)
    new_na_v = v[:T_CTX].reshape(BATCH, 1, SEQ, N_HEADS, HEAD_DIM)
    new_state_ret = new_state.reshape(BATCH, 1, 2, N_HEADS, HEAD_DIM, HEAD_DIM)
    return (y_prompt, y_sample, new_na_k, new_na_v, new_state_ret)
```

```python
import functools

import numpy as np
import jax
import jax.numpy as jnp
from jax import lax
from jax.experimental import pallas as pl
from jax.experimental.pallas import tpu as pltpu

D_MODEL = 1024
BATCH = 32
SEQ = 256
DEC_BATCH = 4
DEC_SEQ = 4096
PAST_LEN = 256
GRID_W = 64
HEAD_DIM = 64
HALF = D_MODEL // 2
N_HEADS = HALF // HEAD_DIM
NA_WIN_R = 8
NA_WIN_C = 16
FT_GROUPS = 4
FT_GROUP_CH = HALF // FT_GROUPS
RET_CHUNK_CTX = 256
RET_CHUNK_SMP = 512
D_FF = 2816
N_EXPERTS = 8
D_FF_EXPERT = 3584
EPS = 1e-6
NEG_INF = -1e30
ATT_SCALE = HEAD_DIM ** -0.5

T_CTX = BATCH * SEQ
T_SMP = DEC_BATCH * DEC_SEQ
T_ALL = T_CTX + T_SMP
N_COND = 8
N_MOD = 6 * D_MODEL
GRID_H = DEC_SEQ // GRID_W
FT_N1 = 64

F32 = jnp.float32
BF16 = jnp.bfloat16
VMEM_LIMIT = 56 * 1024 * 1024

MOE_TM = 512
MOE_TF = 1792
MOE_A = 2 * T_ALL
MOE_NB = MOE_A // MOE_TM + N_EXPERTS
MOE_P = MOE_NB * MOE_TM
GATHER_TM = 256


def _cparams(*sem):
    return pltpu.CompilerParams(dimension_semantics=sem, vmem_limit_bytes=VMEM_LIMIT)


def _cond_index(tm):
    n_ctx = T_CTX // tm
    per_b = DEC_SEQ // tm

    def f(i):
        return jnp.where(i < n_ctx, 0, 1 + (i - n_ctx) // per_b)

    return f


def _ctx_smp_specs(tm, width):
    n_ctx = T_CTX // tm
    return (pl.BlockSpec((tm, width), lambda i: (jnp.minimum(i, n_ctx - 1), 0)),
            pl.BlockSpec((tm, width), lambda i: (jnp.maximum(i - n_ctx, 0), 0)))


def _pick_ctx_smp(tm, c_ref, s_ref):
    return jnp.where(pl.program_id(0) < T_CTX // tm, c_ref[...], s_ref[...])


def _silu(x):
    return x * (1.0 / (1.0 + jnp.exp(-x)))


def _norm_mod(x, g, scale, shift):
    y = x * lax.rsqrt(jnp.mean(x * x, axis=-1, keepdims=True) + EPS)
    return (y * g) * (1.0 + scale) + shift


def _split(a):
    hi = a.astype(BF16)
    lo = (a - hi.astype(F32)).astype(BF16)
    return hi, lo


def _dot(a, b):
    return jnp.dot(a, b, preferred_element_type=F32)


def _dot_nt(a, b):
    return lax.dot_general(a, b, (((1,), (1,)), ((), ())), preferred_element_type=F32)


def _dot3(a_hi, a_lo, b_hi, b_lo):
    return _dot(a_hi, b_hi) + (_dot(a_hi, b_lo) + _dot(a_lo, b_hi))


def _ada_kernel(c_ref, w_ref, b_ref, o_ref):
    s = _silu(c_ref[...]).astype(BF16)
    o_ref[...] = _dot(s, w_ref[...].astype(BF16)) + b_ref[...]


def _ada(cond, w_ada, b_ada):
    depth = w_ada.shape[0]
    tn = 1536
    return pl.pallas_call(
        _ada_kernel,
        grid=(depth, N_MOD // tn),
        in_specs=[
            pl.BlockSpec((N_COND, D_MODEL), lambda l, j: (0, 0)),
            pl.BlockSpec((None, D_MODEL, tn), lambda l, j: (l, 0, j)),
            pl.BlockSpec((None, 1, tn), lambda l, j: (l, 0, j)),
        ],
        out_specs=pl.BlockSpec((None, N_COND, tn), lambda l, j: (l, 0, j)),
        out_shape=jax.ShapeDtypeStruct((depth, N_COND, N_MOD), F32),
        compiler_params=_cparams("parallel", "parallel"),
        name="ada_mod",
    )(cond, w_ada, b_ada.reshape(depth, 1, N_MOD))


def _inproj_e_kernel(xc_ref, xs_ref, mod_ref, g_ref, w_ref, bcu_ref, q_ref, k_ref, v_ref, kb_ref, vb_ref, *, tm):
    m = mod_ref[...]
    x = _pick_ctx_smp(tm, xc_ref, xs_ref)
    h = _norm_mod(x, g_ref[...], m[:, D_MODEL:2 * D_MODEL], m[:, 0:D_MODEL]).astype(BF16)
    bcu_ref[...] = _dot(h, w_ref[:, 0:3 * HALF])
    q_ref[...] = _dot(h, w_ref[:, 3 * HALF:4 * HALF]).astype(BF16)
    k = _dot(h, w_ref[:, 4 * HALF:5 * HALF])
    k_ref[...] = k
    kb_ref[...] = k.astype(BF16)
    v = _dot(h, w_ref[:, 5 * HALF:6 * HALF])
    v_ref[...] = v
    vb_ref[...] = v.astype(BF16)


def _in_proj_e(x_ctx, x_smp, mod, g, w):
    tm = 512
    cidx = _cond_index(tm)
    row = lambda i: (i, 0)
    sds = jax.ShapeDtypeStruct
    return pl.pallas_call(
        functools.partial(_inproj_e_kernel, tm=tm),
        grid=(T_ALL // tm,),
        in_specs=[
            *_ctx_smp_specs(tm, D_MODEL),
            pl.BlockSpec((None, 1, N_MOD), lambda i: (cidx(i), 0, 0)),
            pl.BlockSpec((1, D_MODEL), lambda i: (0, 0)),
            pl.BlockSpec((D_MODEL, 6 * HALF), lambda i: (0, 0)),
        ],
        out_specs=[
            pl.BlockSpec((tm, 3 * HALF), row),
            pl.BlockSpec((tm, HALF), row),
            pl.BlockSpec((tm, HALF), row),
            pl.BlockSpec((tm, HALF), row),
            pl.BlockSpec((tm, HALF), row),
            pl.BlockSpec((tm, HALF), row),
        ],
        out_shape=[
            sds((T_ALL, 3 * HALF), F32),
            sds((T_ALL, HALF), BF16),
            sds((T_ALL, HALF), F32),
            sds((T_ALL, HALF), F32),
            sds((T_ALL, HALF), BF16),
            sds((T_ALL, HALF), BF16),
        ],
        compiler_params=_cparams("parallel"),
        name="in_proj_even",
    )(x_ctx, x_smp, mod, g, w)


def _softmax_parts(parts):
    m = parts[0].max(axis=-1, keepdims=True)
    for s in parts[1:]:
        m = jnp.maximum(m, s.max(axis=-1, keepdims=True))
    es = [jnp.exp(s - m) for s in parts]
    l = es[0].sum(axis=-1, keepdims=True)
    for e in es[1:]:
        l = l + e.sum(axis=-1, keepdims=True)
    inv = 1.0 / l
    return [e * inv for e in es]


def _attn_ctx_kernel(q_ref, k_ref, v_ref, o_ref):
    for h in range(N_HEADS):
        sl = slice(h * HEAD_DIM, (h + 1) * HEAD_DIM)
        s = _dot_nt(q_ref[:, sl], k_ref[:, sl]) * ATT_SCALE
        (p,) = _softmax_parts([s])
        o_ref[:, sl] = _dot(p.astype(BF16), v_ref[:, sl]).astype(BF16)


def _attn_ctx(q, kb, vb):
    blk = pl.BlockSpec((SEQ, HALF), lambda b: (b, 0))
    return pl.pallas_call(
        _attn_ctx_kernel,
        grid=(BATCH,),
        in_specs=[blk, blk, blk],
        out_specs=blk,
        out_shape=jax.ShapeDtypeStruct((T_CTX, HALF), BF16),
        compiler_params=_cparams("parallel"),
        name="attn_context",
    )(q, kb, vb)


NA_Q_ROWS = 4
NA_K_ROWS = NA_WIN_R + NA_Q_ROWS


def _na_key_row0(rb):
    return jnp.clip(rb * NA_Q_ROWS - NA_WIN_R // 2, 0, GRID_H - NA_K_ROWS)


def _attn_na_kernel(q_ref, k_ref, v_ref, kc_ref, vc_ref, bias_ref, o_ref):
    rb = pl.program_id(1)
    start = pl.multiple_of(_na_key_row0(rb) * GRID_W, GRID_W)
    n_loc = NA_K_ROWS * GRID_W
    for h in range(N_HEADS):
        sl = slice(h * HEAD_DIM, (h + 1) * HEAD_DIM)
        q = q_ref[:, sl]
        s_loc = _dot_nt(q, k_ref[pl.ds(start, n_loc), sl]) * ATT_SCALE + bias_ref[h]
        s_ctx = _dot_nt(q, kc_ref[:, sl].astype(BF16)) * ATT_SCALE
        p_loc, p_ctx = _softmax_parts([s_loc, s_ctx])
        o = _dot(p_loc.astype(BF16), v_ref[pl.ds(start, n_loc), sl])
        o = o + _dot(p_ctx.astype(BF16), vc_ref[:, sl].astype(BF16))
        o_ref[:, sl] = o.astype(BF16)


def _na_bias_table(rpb):
    cols = np.arange(GRID_W)
    col_start = np.clip(cols - NA_WIN_C // 2, 0, GRID_W - NA_WIN_C)
    col_in = (cols[None, :] >= col_start[:, None]) & (cols[None, :] < col_start[:, None] + NA_WIN_C)
    col_off = np.clip(cols[None, :] - cols[:, None], 1 - NA_WIN_C, NA_WIN_C - 1) + (NA_WIN_C - 1)
    n_off = 2 * NA_WIN_C - 1
    pick = jnp.asarray(col_off[None] == np.arange(n_off)[:, None, None], F32)
    by_col = (rpb.astype(F32)[:, :, :, None, None] * pick[None, None]).sum(axis=2)
    by_col = jnp.pad(by_col, ((0, 0), (NA_K_ROWS, NA_K_ROWS), (0, 0), (0, 0)))
    n_blocks = GRID_H // NA_Q_ROWS
    variants = []
    for rb in (0, 1, n_blocks - 1):
        k_row0 = int(np.clip(rb * NA_Q_ROWS - NA_WIN_R // 2, 0, GRID_H - NA_K_ROWS))
        k_row = k_row0 + np.arange(NA_K_ROWS)
        per_q_row = []
        for qr in range(NA_Q_ROWS):
            q_row = rb * NA_Q_ROWS + qr
            win0 = int(np.clip(q_row - NA_WIN_R // 2, 0, GRID_H - NA_WIN_R))
            row_in = (k_row >= win0) & (k_row < win0 + NA_WIN_R)
            ro0 = k_row0 - q_row + (NA_WIN_R - 1) + NA_K_ROWS
            b = by_col[:, ro0:ro0 + NA_K_ROWS]
            keep = row_in[:, None, None] & col_in[None]
            per_q_row.append(jnp.where(keep[None], b, NEG_INF))
        b = jnp.stack(per_q_row, axis=1)
        variants.append(b.transpose(0, 1, 3, 2, 4).reshape(N_HEADS, NA_Q_ROWS * GRID_W, NA_K_ROWS * GRID_W))
    return jnp.stack(variants)


def _attn_na(q, kb, vb, cache_k, cache_v, bias):
    tq = NA_Q_ROWS * GRID_W
    n_blocks = GRID_H // NA_Q_ROWS
    ctx_q_tiles = T_CTX // tq
    ctx_b_tiles = T_CTX // DEC_SEQ

    def bias_idx(b, rb):
        return (jnp.where(rb == 0, 0, jnp.where(rb == n_blocks - 1, 2, 1)), 0, 0, 0)

    full = pl.BlockSpec((DEC_SEQ, HALF), lambda b, r: (ctx_b_tiles + b, 0))
    cache = pl.BlockSpec((None, PAST_LEN, HALF), lambda b, r: (b, 0, 0))
    return pl.pallas_call(
        _attn_na_kernel,
        grid=(DEC_BATCH, n_blocks),
        in_specs=[
            pl.BlockSpec((tq, HALF), lambda b, r: (ctx_q_tiles + b * n_blocks + r, 0)),
            full, full, cache, cache,
            pl.BlockSpec((None, N_HEADS, tq, NA_K_ROWS * GRID_W), bias_idx),
        ],
        out_specs=pl.BlockSpec((tq, HALF), lambda b, r: (b * n_blocks + r, 0)),
        out_shape=jax.ShapeDtypeStruct((T_SMP, HALF), BF16),
        compiler_params=_cparams("parallel", "arbitrary"),
        name="attn_neighbourhood",
    )(q, kb, vb, cache_k, cache_v, bias)


def _seq_edges(i, tm):
    n_ctx = T_CTX // tm
    ctx_per = SEQ // tm
    smp_per = DEC_SEQ // tm
    j = i - n_ctx
    first = jnp.where(i < n_ctx, i % ctx_per == 0, j % smp_per == 0)
    last = jnp.where(i < n_ctx, i % ctx_per == ctx_per - 1, j % smp_per == smp_per - 1)
    return first, last


def _outproj_e_kernel(bcu_ref, prev_ref, next_ref, oc_ref, os_ref, xc_ref, xs_ref, mod_ref, wc_ref, w_ref, o_ref,
                      *, tm):
    i = pl.program_id(0)
    first, last = _seq_edges(i, tm)
    bcu = bcu_ref[...]
    cu = bcu[:, HALF:2 * HALF] * bcu[:, 2 * HALF:3 * HALF]
    pv = prev_ref[7:8, :]
    nx = next_ref[0:1, :]
    cu_prev_row = jnp.where(first, 0.0, pv[:, HALF:2 * HALF] * pv[:, 2 * HALF:3 * HALF])
    cu_next_row = jnp.where(last, 0.0, nx[:, HALF:2 * HALF] * nx[:, 2 * HALF:3 * HALF])
    rows = lax.broadcasted_iota(jnp.int32, (tm, HALF), 0)
    cu_prev = jnp.where(rows == 0, cu_prev_row, pltpu.roll(cu, 1, axis=0))
    cu_next = jnp.where(rows == tm - 1, cu_next_row, pltpu.roll(cu, tm - 1, axis=0))
    wc = wc_ref[...]
    y_a = bcu[:, 0:HALF] * (cu_prev * wc[0:1] + cu * wc[1:2] + cu_next * wc[2:3])
    y_b = _pick_ctx_smp(tm, oc_ref, os_ref)
    y = _dot(y_a.astype(BF16), w_ref[0:HALF, :]) + _dot(y_b, w_ref[HALF:D_MODEL, :])
    o_ref[...] = _pick_ctx_smp(tm, xc_ref, xs_ref) + mod_ref[:, 2 * D_MODEL:3 * D_MODEL] * y


def _out_proj_e(bcu, o_ctx, o_smp, x_ctx, x_smp, mod, w_conv_t, w):
    tm = 256
    cidx = _cond_index(tm)
    row = lambda i: (i, 0)
    nb8 = T_ALL // 8
    return pl.pallas_call(
        functools.partial(_outproj_e_kernel, tm=tm),
        grid=(T_ALL // tm,),
        in_specs=[
            pl.BlockSpec((tm, 3 * HALF), row),
            pl.BlockSpec((8, 3 * HALF), lambda i: (jnp.maximum(i * (tm // 8) - 1, 0), 0)),
            pl.BlockSpec((8, 3 * HALF), lambda i: (jnp.minimum((i + 1) * (tm // 8), nb8 - 1), 0)),
            *_ctx_smp_specs(tm, HALF),
            *_ctx_smp_specs(tm, D_MODEL),
            pl.BlockSpec((None, 1, N_MOD), lambda i: (cidx(i), 0, 0)),
            pl.BlockSpec((3, HALF), lambda i: (0, 0)),
            pl.BlockSpec((D_MODEL, D_MODEL), lambda i: (0, 0)),
        ],
        out_specs=pl.BlockSpec((tm, D_MODEL), row),
        out_shape=jax.ShapeDtypeStruct((T_ALL, D_MODEL), F32),
        compiler_params=_cparams("parallel"),
        name="out_proj_even",
    )(bcu, bcu, bcu, o_ctx, o_smp, x_ctx, x_smp, mod, w_conv_t, w)


def _ffn_kernel(x_ref, mod_ref, g_ref, wg_ref, wu_ref, wd_ref, o_ref, h_ref, acc_ref):
    f = pl.program_id(1)

    @pl.when(f == 0)
    def _():
        m = mod_ref[...]
        h = _norm_mod(x_ref[...], g_ref[...], m[:, 4 * D_MODEL:5 * D_MODEL], m[:, 3 * D_MODEL:4 * D_MODEL])
        h_ref[...] = h.astype(BF16)
        acc_ref[...] = jnp.zeros_like(acc_ref)

    h = h_ref[...]
    a = _silu(_dot(h, wg_ref[...])) * _dot(h, wu_ref[...])
    acc_ref[...] += _dot(a.astype(BF16), wd_ref[...])

    @pl.when(f == pl.num_programs(1) - 1)
    def _():
        o_ref[...] = x_ref[...] + mod_ref[:, 5 * D_MODEL:6 * D_MODEL] * acc_ref[...]


def _ffn(x, mod, g, wg, wu, wd):
    tm = 512
    tf = D_FF // 2
    cidx = _cond_index(tm)
    return pl.pallas_call(
        _ffn_kernel,
        grid=(T_ALL // tm, D_FF // tf),
        in_specs=[
            pl.BlockSpec((tm, D_MODEL), lambda i, f: (i, 0)),
            pl.BlockSpec((None, 1, N_MOD), lambda i, f: (cidx(i), 0, 0)),
            pl.BlockSpec((1, D_MODEL), lambda i, f: (0, 0)),
            pl.BlockSpec((D_MODEL, tf), lambda i, f: (0, f)),
            pl.BlockSpec((D_MODEL, tf), lambda i, f: (0, f)),
            pl.BlockSpec((tf, D_MODEL), lambda i, f: (f, 0)),
        ],
        out_specs=pl.BlockSpec((tm, D_MODEL), lambda i, f: (i, 0)),
        out_shape=jax.ShapeDtypeStruct((T_ALL, D_MODEL), F32),
        scratch_shapes=[pltpu.VMEM((tm, D_MODEL), BF16), pltpu.VMEM((tm, D_MODEL), F32)],
        compiler_params=_cparams("parallel", "arbitrary"),
        name="ffn_dense",
    )(x, mod, g, wg, wu, wd)


def _inproj_o_kernel(x_ref, mod_ref, g_ref, w_ref, mch_ref, mcl_ref, ar_ref, ai_ref, q_ref, k_ref, v_ref, gt_ref):
    m = mod_ref[...]
    h = _norm_mod(x_ref[...], g_ref[...], m[:, D_MODEL:2 * D_MODEL], m[:, 0:D_MODEL]).astype(BF16)
    u_hi, u_lo = _split(_dot(h, w_ref[:, 0:HALF]))
    a = _dot3(u_hi, u_lo, mch_ref[...], mcl_ref[...])
    ar_ref[...] = a[:, 0:HALF]
    ai_ref[...] = a[:, HALF:2 * HALF]
    q_ref[...] = _dot(h, w_ref[:, HALF:2 * HALF]).astype(BF16)
    k_ref[...] = _dot(h, w_ref[:, 2 * HALF:3 * HALF]) * ATT_SCALE
    v_ref[...] = _dot(h, w_ref[:, 3 * HALF:4 * HALF]).astype(BF16)
    gt_ref[...] = _dot(h, w_ref[:, 4 * HALF:5 * HALF])


def _in_proj_o(x, mod, g, w, mc_hi, mc_lo):
    tm = 512
    cidx = _cond_index(tm)
    row = lambda i: (i, 0)
    half_out = pl.BlockSpec((tm, HALF), row)
    sds = jax.ShapeDtypeStruct
    return pl.pallas_call(
        _inproj_o_kernel,
        grid=(T_ALL // tm,),
        in_specs=[
            pl.BlockSpec((tm, D_MODEL), row),
            pl.BlockSpec((None, 1, N_MOD), lambda i: (cidx(i), 0, 0)),
            pl.BlockSpec((1, D_MODEL), lambda i: (0, 0)),
            pl.BlockSpec((D_MODEL, 5 * HALF), lambda i: (0, 0)),
            pl.BlockSpec((HALF, 2 * HALF), lambda i: (0, 0)),
            pl.BlockSpec((HALF, 2 * HALF), lambda i: (0, 0)),
        ],
        out_specs=[half_out] * 6,
        out_shape=[
            sds((T_ALL, HALF), F32),
            sds((T_ALL, HALF), F32),
            sds((T_ALL, HALF), BF16),
            sds((T_ALL, HALF), F32),
            sds((T_ALL, HALF), BF16),
            sds((T_ALL, HALF), F32),
        ],
        compiler_params=_cparams("parallel"),
        name="in_proj_odd",
    )(x, mod, g, w, mc_hi, mc_lo)


def _dft_cos_sin(n):
    k = np.arange(n)
    ang = 2.0 * np.pi * ((k[:, None] * k[None, :]) % n) / n
    return np.cos(ang), np.sin(ang)


def _hi_lo(a):
    a = jnp.asarray(a, F32)
    hi = a.astype(BF16)
    return hi, (a - hi.astype(F32)).astype(BF16)


def _channel_dft_table():
    c, s = _dft_cos_sin(FT_GROUP_CH)
    scale = FT_GROUP_CH ** -0.5
    eye = np.eye(FT_GROUPS)
    return np.concatenate([np.kron(eye, c * scale), np.kron(eye, -s * scale)], axis=1)


def _fourier_ctx_kernel(ar_ref, ai_ref, th_ref, tl_ref, o_ref):
    a_hi, a_lo = _split(jnp.concatenate([ar_ref[...], ai_ref[...]], axis=0))
    o_ref[...] = _dot3(th_ref[...], tl_ref[...], a_hi, a_lo).astype(BF16)


def _fourier_ctx(ar, ai):
    c, s = _dft_cos_sin(SEQ)
    t_hi, t_lo = _hi_lo(np.concatenate([c, s], axis=1) * SEQ ** -0.5)
    blk = pl.BlockSpec((SEQ, HALF), lambda b: (b, 0))
    tab = pl.BlockSpec((SEQ, 2 * SEQ), lambda b: (0, 0))
    return pl.pallas_call(
        _fourier_ctx_kernel,
        grid=(BATCH,),
        in_specs=[blk, blk, tab, tab],
        out_specs=blk,
        out_shape=jax.ShapeDtypeStruct((T_CTX, HALF), BF16),
        compiler_params=_cparams("parallel"),
        name="fourier_context",
    )(ar, ai, t_hi, t_lo)


def _fourier_s1_kernel(ar_ref, ai_ref, mh_ref, ml_ref, yr_ref, yi_ref):
    a_hi, a_lo = _split(jnp.concatenate([ar_ref[...], ai_ref[...]], axis=0))
    y = _dot3(mh_ref[...], ml_ref[...], a_hi, a_lo)
    yr_ref[...] = y[0:FT_N1]
    yi_ref[...] = y[FT_N1:2 * FT_N1]


def _fourier_s3_kernel(yr_ref, yi_ref, tc_ref, ts_ref, mh_ref, ml_ref, o_ref, *, nk):
    lanes = HALF // tc_ref.shape[-1]
    for j in range(nk):
        tc = jnp.concatenate([tc_ref[j]] * lanes, axis=-1)
        ts = jnp.concatenate([ts_ref[j]] * lanes, axis=-1)
        yr = yr_ref[j]
        yi = yi_ref[j]
        z = jnp.concatenate([yr * tc + yi * ts, yi * tc - yr * ts], axis=0)
        z_hi, z_lo = _split(z)
        o_ref[:, j * HALF:(j + 1) * HALF] = _dot3(mh_ref[...], ml_ref[...], z_hi, z_lo).astype(BF16)


def _fourier_smp(ar, ai):
    n1 = FT_N1
    wide = n1 * HALF
    c, s = _dft_cos_sin(n1)
    m1_hi, m1_lo = _hi_lo(np.block([[c, s], [-s, c]]))
    m3_hi, m3_lo = _hi_lo(np.concatenate([c, s], axis=1) / n1)
    kk = np.arange(n1)
    ang = 2.0 * np.pi * (kk[:, None] * kk[None, :]) / DEC_SEQ
    tw_c = jnp.broadcast_to(jnp.asarray(np.cos(ang), F32)[:, :, None], (n1, n1, 128))
    tw_s = jnp.broadcast_to(jnp.asarray(np.sin(ang), F32)[:, :, None], (n1, n1, 128))

    ncol = 4096
    ctx_tiles = T_CTX // n1 // n1
    a_blk = pl.BlockSpec((n1, ncol), lambda b, j: (ctx_tiles + b, j))
    y_blk = pl.BlockSpec((n1, ncol), lambda b, j: (b, j))
    m1_blk = pl.BlockSpec((2 * n1, 2 * n1), lambda b, j: (0, 0))
    y_sds = jax.ShapeDtypeStruct((DEC_BATCH * n1, wide), F32)
    yr, yi = pl.pallas_call(
        _fourier_s1_kernel,
        grid=(DEC_BATCH, wide // ncol),
        in_specs=[a_blk, a_blk, m1_blk, m1_blk],
        out_specs=[y_blk, y_blk],
        out_shape=[y_sds, y_sds],
        compiler_params=_cparams("parallel", "parallel"),
        name="fourier_latent_stage1",
    )(ar.reshape(T_ALL // n1, wide), ai.reshape(T_ALL // n1, wide), m1_hi, m1_lo)

    nk = 8
    z_blk = pl.BlockSpec((nk, n1, HALF), lambda b, k: (b * (n1 // nk) + k, 0, 0))
    tw_blk = pl.BlockSpec((nk, n1, 128), lambda b, k: (k, 0, 0))
    m3_blk = pl.BlockSpec((n1, 2 * n1), lambda b, k: (0, 0))
    out = pl.pallas_call(
        functools.partial(_fourier_s3_kernel, nk=nk),
        grid=(DEC_BATCH, n1 // nk),
        in_specs=[z_blk, z_blk, tw_blk, tw_blk, m3_blk, m3_blk],
        out_specs=pl.BlockSpec((n1, nk * HALF), lambda b, k: (b, k)),
        out_shape=jax.ShapeDtypeStruct((DEC_BATCH * n1, wide), BF16),
        compiler_params=_cparams("parallel", "parallel"),
        name="fourier_latent_stage3",
    )(yr.reshape(DEC_BATCH * n1, n1, HALF), yi.reshape(DEC_BATCH * n1, n1, HALF), tw_c, tw_s, m3_hi, m3_lo)
    return out.reshape(T_SMP, HALF)


def _retention_kernel(lg_ref, q_ref, k_ref, v_ref, g_ref, *rest, seq, chunk, nh, has_s0):
    if has_s0:
        s0_ref, y_ref, of_ref, ob_ref, dec_ref, xi_ref, zeta_ref = rest
        st_ref = None
    else:
        y_ref, st_ref, of_ref, ob_ref, dec_ref, xi_ref, zeta_ref = rest
        s0_ref = None
    L = chunk
    nc = seq // L
    hb = pl.program_id(1)
    diff = (lax.broadcasted_iota(jnp.int32, (L, L), 0) - lax.broadcasted_iota(jnp.int32, (L, L), 1)).astype(F32)
    li = lax.broadcasted_iota(jnp.int32, (L, HEAD_DIM), 0).astype(F32)
    one = jnp.ones((1, 1), F32)

    gcs = []
    for hh in range(nh):
        lgf = lg_ref[0, hb * nh + hh]
        lgb = lg_ref[1, hb * nh + hh]
        dec_ref[2 * hh] = jnp.where(diff >= 0, jnp.exp(lgf * jnp.maximum(diff, 0.0)), 0.0)
        dec_ref[2 * hh + 1] = jnp.where(diff <= 0, jnp.exp(lgb * jnp.maximum(-diff, 0.0)), 0.0)
        xi_ref[2 * hh] = jnp.exp(lgf * (li + 1.0))
        xi_ref[2 * hh + 1] = jnp.exp(lgb * (L - li))
        zeta_ref[2 * hh] = jnp.exp(lgf * (L - 1.0 - li))
        zeta_ref[2 * hh + 1] = jnp.exp(lgb * li)
        gcs.append(jnp.exp(one * (lgf * L)))
        gcs.append(jnp.exp(one * (lgb * L)))

    def chunk(c, s, t, sl, o_ref):
        r0 = pl.multiple_of(c * L, L)
        qc = q_ref[pl.ds(r0, L), sl]
        kc = k_ref[pl.ds(r0, L), sl]
        vc = v_ref[pl.ds(r0, L), sl]
        inner = _dot_nt(qc, kc.astype(BF16)) * dec_ref[t]
        o_ref[pl.ds(r0, L), sl] = _dot(inner.astype(BF16), vc) + _dot(qc, s.astype(BF16)) * xi_ref[t]
        return s * gcs[t] + _dot((kc * zeta_ref[t]).T.astype(BF16), vc)

    def scan_step(i, states):
        out = []
        for hh in range(nh):
            sl = slice(hh * HEAD_DIM, (hh + 1) * HEAD_DIM)
            out.append(chunk(i, states[2 * hh], 2 * hh, sl, of_ref))
            out.append(chunk(nc - 1 - i, states[2 * hh + 1], 2 * hh + 1, sl, ob_ref))
        return tuple(out)

    if has_s0:
        init = tuple(s0_ref[t % 2, t // 2] for t in range(2 * nh))
    else:
        init = tuple(jnp.zeros((HEAD_DIM, HEAD_DIM), F32) for _ in range(2 * nh))
    final = lax.fori_loop(0, nc, scan_step, init)
    if st_ref is not None:
        for t in range(2 * nh):
            st_ref[t % 2, t // 2] = final[t]

    def finish(c, carry):
        r0 = pl.multiple_of(c * L, L)
        o_all = of_ref[pl.ds(r0, L), :] + ob_ref[pl.ds(r0, L), :]
        gate = _silu(g_ref[pl.ds(r0, L), :])
        for hh in range(nh):
            sl = slice(hh * HEAD_DIM, (hh + 1) * HEAD_DIM)
            o = o_all[:, sl]
            mu = jnp.mean(o, axis=-1, keepdims=True)
            var = jnp.mean(jnp.square(o - mu), axis=-1, keepdims=True)
            y_ref[pl.ds(r0, L), sl] = (gate[:, sl] * ((o - mu) * lax.rsqrt(var + EPS))).astype(BF16)
        return carry

    lax.fori_loop(0, nc, finish, 0)


def _retention(lg, q, k, v, g, s0, *, seq, chunk, nbatch, row0, nh):
    has_s0 = s0 is not None
    tile0 = row0 // seq
    width = nh * HEAD_DIM
    blk = pl.BlockSpec((seq, width), lambda b, hb: (tile0 + b, hb))
    st_blk = pl.BlockSpec((None, 2, nh, HEAD_DIM, HEAD_DIM), lambda b, hb: (b, 0, hb, 0, 0))
    in_specs = [pl.BlockSpec(memory_space=pltpu.SMEM), blk, blk, blk, blk]
    args = [lg, q, k, v, g]
    y_spec = pl.BlockSpec((seq, width), lambda b, hb: (b, hb))
    y_sds = jax.ShapeDtypeStruct((nbatch * seq, HALF), BF16)
    if has_s0:
        in_specs.append(st_blk)
        args.append(s0)
        out_specs, out_shape = y_spec, y_sds
    else:
        out_specs = [y_spec, st_blk]
        out_shape = [y_sds, jax.ShapeDtypeStruct((nbatch, 2, N_HEADS, HEAD_DIM, HEAD_DIM), F32)]
    return pl.pallas_call(
        functools.partial(_retention_kernel, seq=seq, chunk=chunk, nh=nh, has_s0=has_s0),
        grid=(nbatch, N_HEADS // nh),
        in_specs=in_specs,
        out_specs=out_specs,
        out_shape=out_shape,
        scratch_shapes=[
            pltpu.VMEM((seq, width), F32),
            pltpu.VMEM((seq, width), F32),
            pltpu.VMEM((2 * nh, chunk, chunk), F32),
            pltpu.VMEM((2 * nh, chunk, HEAD_DIM), F32),
            pltpu.VMEM((2 * nh, chunk, HEAD_DIM), F32),
        ],
        compiler_params=_cparams("parallel", "parallel"),
        name="retention_%d" % seq,
    )(*args)


def _outproj_o_kernel(ycc_ref, ycs_ref, ydc_ref, yds_ref, x_ref, mod_ref, w_ref, g_ref, rh_ref, rl_ref,
                      x3_ref, h_ref, e1_ref, e2_ref, ga_ref, gb_ref, *, tm):
    is_ctx = pl.program_id(0) < T_CTX // tm
    y_c = jnp.where(is_ctx, ycc_ref[...], ycs_ref[...])
    y_d = jnp.where(is_ctx, ydc_ref[...], yds_ref[...])
    y = _dot(y_c, w_ref[0:HALF, :]) + _dot(y_d, w_ref[HALF:D_MODEL, :])
    m = mod_ref[...]
    x3 = x_ref[...] + m[:, 2 * D_MODEL:3 * D_MODEL] * y
    x3_ref[...] = x3
    h = _norm_mod(x3, g_ref[...], m[:, 4 * D_MODEL:5 * D_MODEL], m[:, 3 * D_MODEL:4 * D_MODEL])
    h_ref[...] = h
    h_hi, h_lo = _split(h)
    logits = _dot3(h_hi, h_lo, rh_ref[...], rl_ref[...])
    idx = lax.broadcasted_iota(jnp.int32, logits.shape, 1).astype(F32)
    logits = jnp.where(idx < float(N_EXPERTS), logits, -jnp.inf)
    m1 = logits.max(axis=-1, keepdims=True)
    e1 = jnp.where(logits == m1, idx, float(N_EXPERTS)).min(axis=-1, keepdims=True)
    rest = jnp.where(idx == e1, -jnp.inf, logits)
    m2 = rest.max(axis=-1, keepdims=True)
    e2 = jnp.where(rest == m2, idx, float(N_EXPERTS)).min(axis=-1, keepdims=True)
    ex = jnp.exp(m2 - m1)
    den = 1.0 + ex
    wide = (tm, 128)
    e1_ref[...] = jnp.broadcast_to(e1, wide).astype(jnp.int32)
    e2_ref[...] = jnp.broadcast_to(e2, wide).astype(jnp.int32)
    ga_ref[...] = jnp.broadcast_to(1.0 / den, wide)
    gb_ref[...] = jnp.broadcast_to(ex / den, wide)


def _out_proj_o(yc_ctx, yc_smp, yd_ctx, yd_smp, x, mod, w, g, r_hi, r_lo):
    tm = 256
    n_ctx = T_CTX // tm
    cidx = _cond_index(tm)
    row = lambda i: (i, 0)
    ctx_blk = pl.BlockSpec((tm, HALF), lambda i: (jnp.minimum(i, n_ctx - 1), 0))
    smp_blk = pl.BlockSpec((tm, HALF), lambda i: (jnp.maximum(i - n_ctx, 0), 0))
    sds = jax.ShapeDtypeStruct
    rep = pl.BlockSpec((tm, 128), row)
    return pl.pallas_call(
        functools.partial(_outproj_o_kernel, tm=tm),
        grid=(T_ALL // tm,),
        in_specs=[
            ctx_blk, smp_blk, ctx_blk, smp_blk,
            pl.BlockSpec((tm, D_MODEL), row),
            pl.BlockSpec((None, 1, N_MOD), lambda i: (cidx(i), 0, 0)),
            pl.BlockSpec((D_MODEL, D_MODEL), lambda i: (0, 0)),
            pl.BlockSpec((1, D_MODEL), lambda i: (0, 0)),
            pl.BlockSpec((D_MODEL, 128), lambda i: (0, 0)),
            pl.BlockSpec((D_MODEL, 128), lambda i: (0, 0)),
        ],
        out_specs=[pl.BlockSpec((tm, D_MODEL), row), pl.BlockSpec((tm, D_MODEL), row), rep, rep, rep, rep],
        out_shape=[
            sds((T_ALL, D_MODEL), F32), sds((T_ALL, D_MODEL), F32),
            sds((T_ALL, 128), jnp.int32), sds((T_ALL, 128), jnp.int32),
            sds((T_ALL, 128), F32), sds((T_ALL, 128), F32),
        ],
        compiler_params=_cparams("parallel"),
        name="out_proj_odd_route",
    )(yc_ctx, yc_smp, yd_ctx, yd_smp, x, mod, w, g, r_hi, r_lo)


def _routing_tables(e1, e2):
    flat_e = jnp.concatenate([e1, e2])
    onehot = (flat_e[:, None] == jnp.arange(N_EXPERTS, dtype=jnp.int32)[None, :]).astype(jnp.int32)
    csum = jnp.cumsum(onehot, axis=0)
    counts = csum[-1]
    padded = (counts + MOE_TM - 1) // MOE_TM * MOE_TM
    ends = jnp.cumsum(padded)
    pos = jnp.sum(onehot * (csum - 1 + (ends - padded)[None, :]), axis=1)
    tok = jnp.tile(jnp.arange(T_ALL, dtype=jnp.int32), 2)
    slot_t = jnp.zeros((MOE_P,), jnp.int32).at[pos].set(tok)
    block_row0 = jnp.arange(MOE_NB, dtype=jnp.int32) * MOE_TM
    block_e = jnp.minimum(
        jnp.sum((ends[None, :] <= block_row0[:, None]).astype(jnp.int32), axis=1), N_EXPERTS - 1
    ).astype(jnp.int32)
    n_valid = (ends[-1] // MOE_TM).astype(jnp.int32).reshape(1)
    return slot_t, pos[:T_ALL].astype(jnp.int32), pos[T_ALL:].astype(jnp.int32), block_e, n_valid


def _row_copy(src_ref, dst_ref, sem, src_row, dst_row):
    return pltpu.make_async_copy(src_ref.at[pl.ds(src_row, 1), :], dst_ref.at[pl.ds(dst_row, 1), :], sem)


DMA_GROUP = 8


def _gather_kernel(idx_ref, src_ref, o_ref, sem, *, tm):
    def start(g, c):
        for j in range(DMA_GROUP):
            r = g * DMA_GROUP + j
            _row_copy(src_ref, o_ref, sem, idx_ref[0, r], r).start(priority=j % 2)
        return c

    def wait(r, c):
        _row_copy(src_ref, o_ref, sem, 0, r).wait()
        return c

    lax.fori_loop(0, tm // DMA_GROUP, start, 0)
    lax.fori_loop(0, tm, wait, 0, unroll=8)


def _gather_rows(src, slot_t):
    tm = GATHER_TM
    nblk = MOE_P // tm
    return pl.pallas_call(
        functools.partial(_gather_kernel, tm=tm),
        grid=(nblk,),
        in_specs=[
            pl.BlockSpec((None, 1, tm), lambda i: (i, 0, 0), memory_space=pltpu.SMEM),
            pl.BlockSpec(memory_space=pl.ANY),
        ],
        out_specs=pl.BlockSpec((tm, D_MODEL), lambda i: (i, 0)),
        out_shape=jax.ShapeDtypeStruct((MOE_P, D_MODEL), F32),
        scratch_shapes=[pltpu.SemaphoreType.DMA(())],
        compiler_params=_cparams("arbitrary"),
        name="moe_gather",
    )(slot_t.reshape(nblk, 1, tm), src)


def _experts_kernel(be_ref, nv_ref, x_ref, wg_ref, wu_ref, wd_ref, o_ref, xb_ref, acc_ref):
    i = pl.program_id(0)
    f = pl.program_id(1)
    valid = i < nv_ref[0]

    @pl.when(jnp.logical_and(valid, f == 0))
    def _():
        xb_ref[...] = x_ref[...].astype(BF16)
        acc_ref[...] = jnp.zeros_like(acc_ref)

    @pl.when(valid)
    def _():
        x = xb_ref[...]
        a = _silu(_dot(x, wg_ref[...])) * _dot(x, wu_ref[...])
        acc_ref[...] += _dot(a.astype(BF16), wd_ref[...])

    last = f == pl.num_programs(1) - 1

    @pl.when(jnp.logical_and(valid, last))
    def _():
        o_ref[...] = acc_ref[...]

    @pl.when(jnp.logical_and(jnp.logical_not(valid), last))
    def _():
        o_ref[...] = jnp.zeros_like(o_ref)


def _experts(xs, block_e, n_valid, wg, wu, wd):
    nf = D_FF_EXPERT // MOE_TF

    def f_eff(i, f, nv):
        return jnp.where(i < nv[0], f, nf - 1)

    return pl.pallas_call(
        _experts_kernel,
        grid_spec=pltpu.PrefetchScalarGridSpec(
            num_scalar_prefetch=2,
            grid=(MOE_NB, nf),
            in_specs=[
                pl.BlockSpec((MOE_TM, D_MODEL), lambda i, f, be, nv: (i, 0)),
                pl.BlockSpec((None, D_MODEL, MOE_TF), lambda i, f, be, nv: (be[i], 0, f_eff(i, f, nv))),
                pl.BlockSpec((None, D_MODEL, MOE_TF), lambda i, f, be, nv: (be[i], 0, f_eff(i, f, nv))),
                pl.BlockSpec((None, MOE_TF, D_MODEL), lambda i, f, be, nv: (be[i], f_eff(i, f, nv), 0)),
            ],
            out_specs=pl.BlockSpec((MOE_TM, D_MODEL), lambda i, f, be, nv: (i, 0)),
            scratch_shapes=[pltpu.VMEM((MOE_TM, D_MODEL), BF16), pltpu.VMEM((MOE_TM, D_MODEL), F32)],
        ),
        out_shape=jax.ShapeDtypeStruct((MOE_P, D_MODEL), F32),
        compiler_params=_cparams("arbitrary", "arbitrary"),
        name="moe_experts",
    )(block_e, n_valid, xs, wg, wu, wd)


def _combine_kernel(p1_ref, p2_ref, ys_ref, x_ref, ga_ref, gb_ref, mod_ref, g_ref, oc_ref, os_ref, a_ref, b_ref, sems,
                    *, tm):
    def start(g, c):
        for j in range(DMA_GROUP):
            r = g * DMA_GROUP + j
            _row_copy(ys_ref, a_ref, sems.at[0], p1_ref[0, r], r).start(priority=0)
            _row_copy(ys_ref, b_ref, sems.at[1], p2_ref[0, r], r).start(priority=1)
        return c

    def wait(r, c):
        _row_copy(ys_ref, a_ref, sems.at[0], 0, r).wait()
        _row_copy(ys_ref, b_ref, sems.at[1], 0, r).wait()
        return c

    lax.fori_loop(0, tm // DMA_GROUP, start, 0)
    lax.fori_loop(0, tm, wait, 0, unroll=8)
    reps = D_MODEL // 128
    ga = jnp.concatenate([ga_ref[...]] * reps, axis=-1)
    gb = jnp.concatenate([gb_ref[...]] * reps, axis=-1)
    f = a_ref[...] * ga + b_ref[...] * gb
    x = x_ref[...] + mod_ref[:, 5 * D_MODEL:6 * D_MODEL] * f
    y = (x * lax.rsqrt(jnp.mean(x * x, axis=-1, keepdims=True) + EPS)) * g_ref[...]
    is_ctx = pl.program_id(0) < T_CTX // tm

    @pl.when(is_ctx)
    def _():
        oc_ref[...] = y

    @pl.when(jnp.logical_not(is_ctx))
    def _():
        os_ref[...] = y


def _combine(ys, pos1, pos2, x, ga, gb, mod, final_g):
    tm = GATHER_TM
    nblk = T_ALL // tm
    cidx = _cond_index(tm)
    row = lambda i: (i, 0)
    idx_blk = pl.BlockSpec((None, 1, tm), lambda i: (i, 0, 0), memory_space=pltpu.SMEM)
    return pl.pallas_call(
        functools.partial(_combine_kernel, tm=tm),
        grid=(nblk,),
        in_specs=[
            idx_blk, idx_blk,
            pl.BlockSpec(memory_space=pl.ANY),
            pl.BlockSpec((tm, D_MODEL), row),
            pl.BlockSpec((tm, 128), row),
            pl.BlockSpec((tm, 128), row),
            pl.BlockSpec((None, 1, N_MOD), lambda i: (cidx(i), 0, 0)),
            pl.BlockSpec((1, D_MODEL), lambda i: (0, 0)),
        ],
        out_specs=list(_ctx_smp_specs(tm, D_MODEL)),
        out_shape=[jax.ShapeDtypeStruct((T_CTX, D_MODEL), F32), jax.ShapeDtypeStruct((T_SMP, D_MODEL), F32)],
        scratch_shapes=[
            pltpu.VMEM((tm, D_MODEL), F32),
            pltpu.VMEM((tm, D_MODEL), F32),
            pltpu.SemaphoreType.DMA((2,)),
        ],
        compiler_params=_cparams("arbitrary"),
        name="moe_combine_final_norm",
    )(pos1.reshape(nblk, 1, tm), pos2.reshape(nblk, 1, tm), ys, x, ga, gb, mod, final_g)


def kernel(x_prompt, x_sample, cache_na_k, cache_na_v, state_ret, c, c_ctx, w_ada, b_ada, norm_g, final_g, w_in_e, w_conv_e, na_rpb_e, w_out_e, ff_gate_e, ff_up_e, ff_down_e, w_in_o, ret_decay_o, w_out_o, router_o, ex_gate_o, ex_up_o, ex_down_o):
    x_ctx = x_prompt.reshape(T_CTX, D_MODEL)
    x_smp = x_sample.reshape(T_SMP, D_MODEL)
    cond = jnp.concatenate([c_ctx[None, :], c, jnp.zeros((N_COND - 1 - DEC_BATCH, D_MODEL), F32)], axis=0)
    mod = _ada(cond, w_ada, b_ada)
    mod0 = mod[0].reshape(N_COND, 1, N_MOD)
    mod1 = mod[1].reshape(N_COND, 1, N_MOD)

    bcu, q, k, v, kb, vb = _in_proj_e(x_ctx, x_smp, mod0, norm_g[0, 0][None, :], w_in_e[0].astype(BF16))
    o_ctx = _attn_ctx(q, kb, vb)
    o_smp = _attn_na(q, kb, vb,
                     cache_na_k[:, 0].reshape(DEC_BATCH, PAST_LEN, HALF),
                     cache_na_v[:, 0].reshape(DEC_BATCH, PAST_LEN, HALF),
                     _na_bias_table(na_rpb_e[0]))
    x1 = _out_proj_e(bcu, o_ctx, o_smp, x_ctx, x_smp, mod0, w_conv_e[0].T, w_out_e[0].astype(BF16))
    x2 = _ffn(x1, mod0, norm_g[0, 1][None, :], ff_gate_e[0].astype(BF16), ff_up_e[0].astype(BF16),
              ff_down_e[0].astype(BF16))

    mc_hi, mc_lo = _hi_lo(_channel_dft_table())
    ar, ai, q1, k1, v1, g1 = _in_proj_o(x2, mod1, norm_g[1, 0][None, :], w_in_o[0].astype(BF16), mc_hi, mc_lo)
    yc_ctx = _fourier_ctx(ar, ai)
    yc_smp = _fourier_smp(ar, ai)
    lg = jax.nn.log_sigmoid(ret_decay_o[0].astype(F32))
    yd_ctx, new_state = _retention(lg, q1, k1, v1, g1, None, seq=SEQ, chunk=RET_CHUNK_CTX, nbatch=BATCH, row0=0,
                                   nh=8)
    yd_smp = _retention(lg, q1, k1, v1, g1, state_ret[:, 0], seq=DEC_SEQ, chunk=RET_CHUNK_SMP, nbatch=DEC_BATCH,
                        row0=T_CTX, nh=4)
    r_hi, r_lo = _hi_lo(jnp.pad(router_o[0], ((0, 0), (0, 128 - N_EXPERTS))))
    x3, hm, e1, e2, ga, gb = _out_proj_o(yc_ctx, yc_smp, yd_ctx, yd_smp, x2, mod1, w_out_o[0].astype(BF16),
                                         norm_g[1, 1][None, :], r_hi, r_lo)
    slot_t, pos1, pos2, block_e, n_valid = _routing_tables(e1[:, 0], e2[:, 0])
    xs = _gather_rows(hm, slot_t)
    ys = _experts(xs, block_e, n_valid, ex_gate_o[0].astype(BF16), ex_up_o[0].astype(BF16),
                  ex_down_o[0].astype(BF16))
    y_ctx, y_smp = _combine(ys, pos1, pos2, x3, ga, gb, mod1, final_g[None, :])

    y_prompt = y_ctx.reshape(BATCH, SEQ, D_MODEL)
    y_sample = y_smp.reshape(DEC_BATCH, DEC_SEQ, D_MODEL)
    new_na_k = k[:T_CTX].reshape(BATCH, 1, SEQ, N_HEADS, HEAD_DIM)
    new_na_v = v[:T_CTX].reshape(BATCH, 1, SEQ, N_HEADS, HEAD_DIM)
    new_state_ret = new_state.reshape(BATCH, 1, 2, N_HEADS, HEAD_DIM, HEAD_DIM)
    return (y_prompt, y_sample, new_na_k, new_na_v, new_state_ret)
```

```python
import functools

import numpy as np
import jax
import jax.numpy as jnp
from jax import lax
from jax.experimental import pallas as pl
from jax.experimental.pallas import tpu as pltpu

D_MODEL = 1024
BATCH = 32
SEQ = 256
DEC_BATCH = 4
DEC_SEQ = 4096
PAST_LEN = 256
GRID_W = 64
HEAD_DIM = 64
HALF = D_MODEL // 2
N_HEADS = HALF // HEAD_DIM
NA_WIN_R = 8
NA_WIN_C = 16
FT_GROUPS = 4
FT_GROUP_CH = HALF // FT_GROUPS
RET_CHUNK_CTX = 256
RET_CHUNK_SMP = 512
D_FF = 2816
N_EXPERTS = 8
D_FF_EXPERT = 3584
EPS = 1e-6
NEG_INF = -1e30
ATT_SCALE = HEAD_DIM ** -0.5

T_CTX = BATCH * SEQ
T_SMP = DEC_BATCH * DEC_SEQ
T_ALL = T_CTX + T_SMP
N_COND = 8
N_MOD = 6 * D_MODEL
GRID_H = DEC_SEQ // GRID_W
FT_N1 = 64

F32 = jnp.float32
BF16 = jnp.bfloat16
VMEM_LIMIT = 56 * 1024 * 1024

MOE_TM = 512
MOE_TF = 1792
MOE_A = 2 * T_ALL
MOE_NB = MOE_A // MOE_TM + N_EXPERTS
MOE_P = MOE_NB * MOE_TM
GATHER_TM = 256


def _cparams(*sem):
    return pltpu.CompilerParams(dimension_semantics=sem, vmem_limit_bytes=VMEM_LIMIT)


def _cond_index(tm):
    n_ctx = T_CTX // tm
    per_b = DEC_SEQ // tm

    def f(i):
        return jnp.where(i < n_ctx, 0, 1 + (i - n_ctx) // per_b)

    return f


def _ctx_smp_specs(tm, width):
    n_ctx = T_CTX // tm
    return (pl.BlockSpec((tm, width), lambda i: (jnp.minimum(i, n_ctx - 1), 0)),
            pl.BlockSpec((tm, width), lambda i: (jnp.maximum(i - n_ctx, 0), 0)))


def _pick_ctx_smp(tm, c_ref, s_ref):
    return jnp.where(pl.program_id(0) < T_CTX // tm, c_ref[...], s_ref[...])


def _silu(x):
    return x * (1.0 / (1.0 + jnp.exp(-x)))


def _norm_mod(x, g, scale, shift):
    y = x * lax.rsqrt(jnp.mean(x * x, axis=-1, keepdims=True) + EPS)
    return (y * g) * (1.0 + scale) + shift


def _split(a):
    hi = a.astype(BF16)
    lo = (a - hi.astype(F32)).astype(BF16)
    return hi, lo


def _dot(a, b):
    return jnp.dot(a, b, preferred_element_type=F32)


def _dot_nt(a, b):
    return lax.dot_general(a, b, (((1,), (1,)), ((), ())), preferred_element_type=F32)


def _dot3(a_hi, a_lo, b_hi, b_lo):
    return _dot(a_hi, b_hi) + (_dot(a_hi, b_lo) + _dot(a_lo, b_hi))


def _ada_kernel(c_ref, w_ref, b_ref, o_ref):
    s = _silu(c_ref[...]).astype(BF16)
    o_ref[...] = _dot(s, w_ref[...].astype(BF16)) + b_ref[...]


def _ada(cond, w_ada, b_ada):
    depth = w_ada.shape[0]
    tn = 1536
    return pl.pallas_call(
        _ada_kernel,
        grid=(depth, N_MOD // tn),
        in_specs=[
            pl.BlockSpec((N_COND, D_MODEL), lambda l, j: (0, 0)),
            pl.BlockSpec((None, D_MODEL, tn), lambda l, j: (l, 0, j)),
            pl.BlockSpec((None, 1, tn), lambda l, j: (l, 0, j)),
        ],
        out_specs=pl.BlockSpec((None, N_COND, tn), lambda l, j: (l, 0, j)),
        out_shape=jax.ShapeDtypeStruct((depth, N_COND, N_MOD), F32),
        compiler_params=_cparams("parallel", "parallel"),
        name="ada_mod",
    )(cond, w_ada, b_ada.reshape(depth, 1, N_MOD))


def _inproj_e_kernel(xc_ref, xs_ref, mod_ref, g_ref, w_ref, bcu_ref, q_ref, k_ref, v_ref, kb_ref, vb_ref, *, tm):
    m = mod_ref[...]
    x = _pick_ctx_smp(tm, xc_ref, xs_ref)
    h = _norm_mod(x, g_ref[...], m[:, D_MODEL:2 * D_MODEL], m[:, 0:D_MODEL]).astype(BF16)
    bcu_ref[...] = _dot(h, w_ref[:, 0:3 * HALF])
    q_ref[...] = _dot(h, w_ref[:, 3 * HALF:4 * HALF]).astype(BF16)
    k = _dot(h, w_ref[:, 4 * HALF:5 * HALF])
    k_ref[...] = k
    kb_ref[...] = k.astype(BF16)
    v = _dot(h, w_ref[:, 5 * HALF:6 * HALF])
    v_ref[...] = v
    vb_ref[...] = v.astype(BF16)


def _in_proj_e(x_ctx, x_smp, mod, g, w):
    tm = 512
    cidx = _cond_index(tm)
    row = lambda i: (i, 0)
    sds = jax.ShapeDtypeStruct
    return pl.pallas_call(
        functools.partial(_inproj_e_kernel, tm=tm),
        grid=(T_ALL // tm,),
        in_specs=[
            *_ctx_smp_specs(tm, D_MODEL),
            pl.BlockSpec((None, 1, N_MOD), lambda i: (cidx(i), 0, 0)),
            pl.BlockSpec((1, D_MODEL), lambda i: (0, 0)),
            pl.BlockSpec((D_MODEL, 6 * HALF), lambda i: (0, 0)),
        ],
        out_specs=[
            pl.BlockSpec((tm, 3 * HALF), row),
            pl.BlockSpec((tm, HALF), row),
            pl.BlockSpec((tm, HALF), row),
            pl.BlockSpec((tm, HALF), row),
            pl.BlockSpec((tm, HALF), row),
            pl.BlockSpec((tm, HALF), row),
        ],
        out_shape=[
            sds((T_ALL, 3 * HALF), F32),
            sds((T_ALL, HALF), BF16),
            sds((T_ALL, HALF), F32),
            sds((T_ALL, HALF), F32),
            sds((T_ALL, HALF), BF16),
            sds((T_ALL, HALF), BF16),
        ],
        compiler_params=_cparams("parallel"),
        name="in_proj_even",
    )(x_ctx, x_smp, mod, g, w)


def _softmax_parts(parts):
    m = parts[0].max(axis=-1, keepdims=True)
    for s in parts[1:]:
        m = jnp.maximum(m, s.max(axis=-1, keepdims=True))
    es = [jnp.exp(s - m) for s in parts]
    l = es[0].sum(axis=-1, keepdims=True)
    for e in es[1:]:
        l = l + e.sum(axis=-1, keepdims=True)
    inv = 1.0 / l
    return [e * inv for e in es]


def _attn_ctx_kernel(q_ref, k_ref, v_ref, o_ref):
    for h in range(N_HEADS):
        sl = slice(h * HEAD_DIM, (h + 1) * HEAD_DIM)
        s = _dot_nt(q_ref[:, sl], k_ref[:, sl]) * ATT_SCALE
        (p,) = _softmax_parts([s])
        o_ref[:, sl] = _dot(p.astype(BF16), v_ref[:, sl]).astype(BF16)


def _attn_ctx(q, kb, vb):
    blk = pl.BlockSpec((SEQ, HALF), lambda b: (b, 0))
    return pl.pallas_call(
        _attn_ctx_kernel,
        grid=(BATCH,),
        in_specs=[blk, blk, blk],
        out_specs=blk,
        out_shape=jax.ShapeDtypeStruct((T_CTX, HALF), BF16),
        compiler_params=_cparams("parallel"),
        name="attn_context",
    )(q, kb, vb)


NA_Q_ROWS = 4
NA_K_ROWS = NA_WIN_R + NA_Q_ROWS


def _na_key_row0(rb):
    return jnp.clip(rb * NA_Q_ROWS - NA_WIN_R // 2, 0, GRID_H - NA_K_ROWS)


def _attn_na_kernel(q_ref, k_ref, v_ref, kc_ref, vc_ref, bias_ref, o_ref):
    rb = pl.program_id(1)
    start = pl.multiple_of(_na_key_row0(rb) * GRID_W, GRID_W)
    n_loc = NA_K_ROWS * GRID_W
    for h in range(N_HEADS):
        sl = slice(h * HEAD_DIM, (h + 1) * HEAD_DIM)
        q = q_ref[:, sl]
        s_loc = _dot_nt(q, k_ref[pl.ds(start, n_loc), sl]) * ATT_SCALE + bias_ref[h]
        s_ctx = _dot_nt(q, kc_ref[:, sl].astype(BF16)) * ATT_SCALE
        p_loc, p_ctx = _softmax_parts([s_loc, s_ctx])
        o = _dot(p_loc.astype(BF16), v_ref[pl.ds(start, n_loc), sl])
        o = o + _dot(p_ctx.astype(BF16), vc_ref[:, sl].astype(BF16))
        o_ref[:, sl] = o.astype(BF16)


def _na_bias_table(rpb):
    cols = np.arange(GRID_W)
    col_start = np.clip(cols - NA_WIN_C // 2, 0, GRID_W - NA_WIN_C)
    col_in = (cols[None, :] >= col_start[:, None]) & (cols[None, :] < col_start[:, None] + NA_WIN_C)
    col_off = np.clip(cols[None, :] - cols[:, None], 1 - NA_WIN_C, NA_WIN_C - 1) + (NA_WIN_C - 1)
    n_off = 2 * NA_WIN_C - 1
    pick = jnp.asarray(col_off[None] == np.arange(n_off)[:, None, None], F32)
    by_col = (rpb.astype(F32)[:, :, :, None, None] * pick[None, None]).sum(axis=2)
    by_col = jnp.pad(by_col, ((0, 0), (NA_K_ROWS, NA_K_ROWS), (0, 0), (0, 0)))
    n_blocks = GRID_H // NA_Q_ROWS
    variants = []
    for rb in (0, 1, n_blocks - 1):
        k_row0 = int(np.clip(rb * NA_Q_ROWS - NA_WIN_R // 2, 0, GRID_H - NA_K_ROWS))
        k_row = k_row0 + np.arange(NA_K_ROWS)
        per_q_row = []
        for qr in range(NA_Q_ROWS):
            q_row = rb * NA_Q_ROWS + qr
            win0 = int(np.clip(q_row - NA_WIN_R // 2, 0, GRID_H - NA_WIN_R))
            row_in = (k_row >= win0) & (k_row < win0 + NA_WIN_R)
            ro0 = k_row0 - q_row + (NA_WIN_R - 1) + NA_K_ROWS
            b = by_col[:, ro0:ro0 + NA_K_ROWS]
            keep = row_in[:, None, None] & col_in[None]
            per_q_row.append(jnp.where(keep[None], b, NEG_INF))
        b = jnp.stack(per_q_row, axis=1)
        variants.append(b.transpose(0, 1, 3, 2, 4).reshape(N_HEADS, NA_Q_ROWS * GRID_W, NA_K_ROWS * GRID_W))
    return jnp.stack(variants)


def _attn_na(q, kb, vb, cache_k, cache_v, bias):
    tq = NA_Q_ROWS * GRID_W
    n_blocks = GRID_H // NA_Q_ROWS
    ctx_q_tiles = T_CTX // tq
    ctx_b_tiles = T_CTX // DEC_SEQ

    def bias_idx(b, rb):
        return (jnp.where(rb == 0, 0, jnp.where(rb == n_blocks - 1, 2, 1)), 0, 0, 0)

    full = pl.BlockSpec((DEC_SEQ, HALF), lambda b, r: (ctx_b_tiles + b, 0))
    cache = pl.BlockSpec((None, PAST_LEN, HALF), lambda b, r: (b, 0, 0))
    return pl.pallas_call(
        _attn_na_kernel,
        grid=(DEC_BATCH, n_blocks),
        in_specs=[
            pl.BlockSpec((tq, HALF), lambda b, r: (ctx_q_tiles + b * n_blocks + r, 0)),
            full, full, cache, cache,
            pl.BlockSpec((None, N_HEADS, tq, NA_K_ROWS * GRID_W), bias_idx),
        ],
        out_specs=pl.BlockSpec((tq, HALF), lambda b, r: (b * n_blocks + r, 0)),
        out_shape=jax.ShapeDtypeStruct((T_SMP, HALF), BF16),
        compiler_params=_cparams("parallel", "arbitrary"),
        name="attn_neighbourhood",
    )(q, kb, vb, cache_k, cache_v, bias)


def _seq_edges(i, tm):
    n_ctx = T_CTX // tm
    ctx_per = SEQ // tm
    smp_per = DEC_SEQ // tm
    j = i - n_ctx
    first = jnp.where(i < n_ctx, i % ctx_per == 0, j % smp_per == 0)
    last = jnp.where(i < n_ctx, i % ctx_per == ctx_per - 1, j % smp_per == smp_per - 1)
    return first, last


def _outproj_e_kernel(bcu_ref, prev_ref, next_ref, oc_ref, os_ref, xc_ref, xs_ref, mod_ref, wc_ref, w_ref, o_ref,
                      *, tm):
    i = pl.program_id(0)
    first, last = _seq_edges(i, tm)
    bcu = bcu_ref[...]
    cu = bcu[:, HALF:2 * HALF] * bcu[:, 2 * HALF:3 * HALF]
    pv = prev_ref[7:8, :]
    nx = next_ref[0:1, :]
    cu_prev_row = jnp.where(first, 0.0, pv[:, HALF:2 * HALF] * pv[:, 2 * HALF:3 * HALF])
    cu_next_row = jnp.where(last, 0.0, nx[:, HALF:2 * HALF] * nx[:, 2 * HALF:3 * HALF])
    rows = lax.broadcasted_iota(jnp.int32, (tm, HALF), 0)
    cu_prev = jnp.where(rows == 0, cu_prev_row, pltpu.roll(cu, 1, axis=0))
    cu_next = jnp.where(rows == tm - 1, cu_next_row, pltpu.roll(cu, tm - 1, axis=0))
    wc = wc_ref[...]
    y_a = bcu[:, 0:HALF] * (cu_prev * wc[0:1] + cu * wc[1:2] + cu_next * wc[2:3])
    y_b = _pick_ctx_smp(tm, oc_ref, os_ref)
    y = _dot(y_a.astype(BF16), w_ref[0:HALF, :]) + _dot(y_b, w_ref[HALF:D_MODEL, :])
    o_ref[...] = _pick_ctx_smp(tm, xc_ref, xs_ref) + mod_ref[:, 2 * D_MODEL:3 * D_MODEL] * y


def _out_proj_e(bcu, o_ctx, o_smp, x_ctx, x_smp, mod, w_conv_t, w):
    tm = 256
    cidx = _cond_index(tm)
    row = lambda i: (i, 0)
    nb8 = T_ALL // 8
    return pl.pallas_call(
        functools.partial(_outproj_e_kernel, tm=tm),
        grid=(T_ALL // tm,),
        in_specs=[
            pl.BlockSpec((tm, 3 * HALF), row),
            pl.BlockSpec((8, 3 * HALF), lambda i: (jnp.maximum(i * (tm // 8) - 1, 0), 0)),
            pl.BlockSpec((8, 3 * HALF), lambda i: (jnp.minimum((i + 1) * (tm // 8), nb8 - 1), 0)),
            *_ctx_smp_specs(tm, HALF),
            *_ctx_smp_specs(tm, D_MODEL),
            pl.BlockSpec((None, 1, N_MOD), lambda i: (cidx(i), 0, 0)),
            pl.BlockSpec((3, HALF), lambda i: (0, 0)),
            pl.BlockSpec((D_MODEL, D_MODEL), lambda i: (0, 0)),
        ],
        out_specs=pl.BlockSpec((tm, D_MODEL), row),
        out_shape=jax.ShapeDtypeStruct((T_ALL, D_MODEL), F32),
        compiler_params=_cparams("parallel"),
        name="out_proj_even",
    )(bcu, bcu, bcu, o_ctx, o_smp, x_ctx, x_smp, mod, w_conv_t, w)


def _ffn_kernel(x_ref, mod_ref, g_ref, wg_ref, wu_ref, wd_ref, o_ref, h_ref, acc_ref):
    f = pl.program_id(1)

    @pl.when(f == 0)
    def _():
        m = mod_ref[...]
        h = _norm_mod(x_ref[...], g_ref[...], m[:, 4 * D_MODEL:5 * D_MODEL], m[:, 3 * D_MODEL:4 * D_MODEL])
        h_ref[...] = h.astype(BF16)
        acc_ref[...] = jnp.zeros_like(acc_ref)

    h = h_ref[...]
    a = _silu(_dot(h, wg_ref[...])) * _dot(h, wu_ref[...])
    acc_ref[...] += _dot(a.astype(BF16), wd_ref[...])

    @pl.when(f == pl.num_programs(1) - 1)
    def _():
        o_ref[...] = x_ref[...] + mod_ref[:, 5 * D_MODEL:6 * D_MODEL] * acc_ref[...]


def _ffn(x, mod, g, wg, wu, wd):
    tm = 512
    tf = D_FF // 2
    cidx = _cond_index(tm)
    return pl.pallas_call(
        _ffn_kernel,
        grid=(T_ALL // tm, D_FF // tf),
        in_specs=[
            pl.BlockSpec((tm, D_MODEL), lambda i, f: (i, 0)),
            pl.BlockSpec((None, 1, N_MOD), lambda i, f: (cidx(i), 0, 0)),
            pl.BlockSpec((1, D_MODEL), lambda i, f: (0, 0)),
            pl.BlockSpec((D_MODEL, tf), lambda i, f: (0, f)),
            pl.BlockSpec((D_MODEL, tf), lambda i, f: (0, f)),
            pl.BlockSpec((tf, D_MODEL), lambda i, f: (f, 0)),
        ],
        out_specs=pl.BlockSpec((tm, D_MODEL), lambda i, f: (i, 0)),
        out_shape=jax.ShapeDtypeStruct((T_ALL, D_MODEL), F32),
        scratch_shapes=[pltpu.VMEM((tm, D_MODEL), BF16), pltpu.VMEM((tm, D_MODEL), F32)],
        compiler_params=_cparams("parallel", "arbitrary"),
        name="ffn_dense",
    )(x, mod, g, wg, wu, wd)


def _inproj_o_kernel(x_ref, mod_ref, g_ref, w_ref, mch_ref, mcl_ref, ar_ref, ai_ref, q_ref, k_ref, v_ref, gt_ref):
    m = mod_ref[...]
    h = _norm_mod(x_ref[...], g_ref[...], m[:, D_MODEL:2 * D_MODEL], m[:, 0:D_MODEL]).astype(BF16)
    u_hi, u_lo = _split(_dot(h, w_ref[:, 0:HALF]))
    a = _dot3(u_hi, u_lo, mch_ref[...], mcl_ref[...])
    ar_ref[...] = a[:, 0:HALF]
    ai_ref[...] = a[:, HALF:2 * HALF]
    q_ref[...] = _dot(h, w_ref[:, HALF:2 * HALF]).astype(BF16)
    k_ref[...] = _dot(h, w_ref[:, 2 * HALF:3 * HALF]) * ATT_SCALE
    v_ref[...] = _dot(h, w_ref[:, 3 * HALF:4 * HALF]).astype(BF16)
    gt_ref[...] = _dot(h, w_ref[:, 4 * HALF:5 * HALF])


def _in_proj_o(x, mod, g, w, mc_hi, mc_lo):
    tm = 512
    cidx = _cond_index(tm)
    row = lambda i: (i, 0)
    half_out = pl.BlockSpec((tm, HALF), row)
    sds = jax.ShapeDtypeStruct
    return pl.pallas_call(
        _inproj_o_kernel,
        grid=(T_ALL // tm,),
        in_specs=[
            pl.BlockSpec((tm, D_MODEL), row),
            pl.BlockSpec((None, 1, N_MOD), lambda i: (cidx(i), 0, 0)),
            pl.BlockSpec((1, D_MODEL), lambda i: (0, 0)),
            pl.BlockSpec((D_MODEL, 5 * HALF), lambda i: (0, 0)),
            pl.BlockSpec((HALF, 2 * HALF), lambda i: (0, 0)),
            pl.BlockSpec((HALF, 2 * HALF), lambda i: (0, 0)),
        ],
        out_specs=[half_out] * 6,
        out_shape=[
            sds((T_ALL, HALF), F32),
            sds((T_ALL, HALF), F32),
            sds((T_ALL, HALF), BF16),
            sds((T_ALL, HALF), F32),
            sds((T_ALL, HALF), BF16),
            sds((T_ALL, HALF), F32),
        ],
        compiler_params=_cparams("parallel"),
        name="in_proj_odd",
    )(x, mod, g, w, mc_hi, mc_lo)


def _dft_cos_sin(n):
    k = np.arange(n)
    ang = 2.0 * np.pi * ((k[:, None] * k[None, :]) % n) / n
    return np.cos(ang), np.sin(ang)


def _hi_lo(a):
    a = jnp.asarray(a, F32)
    hi = a.astype(BF16)
    return hi, (a - hi.astype(F32)).astype(BF16)


def _channel_dft_table():
    c, s = _dft_cos_sin(FT_GROUP_CH)
    scale = FT_GROUP_CH ** -0.5
    eye = np.eye(FT_GROUPS)
    return np.concatenate([np.kron(eye, c * scale), np.kron(eye, -s * scale)], axis=1)


def _fourier_ctx_kernel(ar_ref, ai_ref, th_ref, tl_ref, o_ref):
    a_hi, a_lo = _split(jnp.concatenate([ar_ref[...], ai_ref[...]], axis=0))
    o_ref[...] = _dot3(th_ref[...], tl_ref[...], a_hi, a_lo).astype(BF16)


def _fourier_ctx(ar, ai):
    c, s = _dft_cos_sin(SEQ)
    t_hi, t_lo = _hi_lo(np.concatenate([c, s], axis=1) * SEQ ** -0.5)
    blk = pl.BlockSpec((SEQ, HALF), lambda b: (b, 0))
    tab = pl.BlockSpec((SEQ, 2 * SEQ), lambda b: (0, 0))
    return pl.pallas_call(
        _fourier_ctx_kernel,
        grid=(BATCH,),
        in_specs=[blk, blk, tab, tab],
        out_specs=blk,
        out_shape=jax.ShapeDtypeStruct((T_CTX, HALF), BF16),
        compiler_params=_cparams("parallel"),
        name="fourier_context",
    )(ar, ai, t_hi, t_lo)


def _fourier_s1_kernel(ar_ref, ai_ref, mh_ref, ml_ref, yr_ref, yi_ref):
    a_hi, a_lo = _split(jnp.concatenate([ar_ref[...], ai_ref[...]], axis=0))
    y = _dot3(mh_ref[...], ml_ref[...], a_hi, a_lo)
    yr_ref[...] = y[0:FT_N1]
    yi_ref[...] = y[FT_N1:2 * FT_N1]


def _fourier_s3_kernel(yr_ref, yi_ref, tc_ref, ts_ref, mh_ref, ml_ref, o_ref, *, nk):
    lanes = HALF // tc_ref.shape[-1]
    for j in range(nk):
        tc = jnp.concatenate([tc_ref[j]] * lanes, axis=-1)
        ts = jnp.concatenate([ts_ref[j]] * lanes, axis=-1)
        yr = yr_ref[j]
        yi = yi_ref[j]
        z = jnp.concatenate([yr * tc + yi * ts, yi * tc - yr * ts], axis=0)
        z_hi, z_lo = _split(z)
        o_ref[:, j * HALF:(j + 1) * HALF] = _dot3(mh_ref[...], ml_ref[...], z_hi, z_lo).astype(BF16)


def _fourier_smp(ar, ai):
    n1 = FT_N1
    wide = n1 * HALF
    c, s = _dft_cos_sin(n1)
    m1_hi, m1_lo = _hi_lo(np.block([[c, s], [-s, c]]))
    m3_hi, m3_lo = _hi_lo(np.concatenate([c, s], axis=1) / n1)
    kk = np.arange(n1)
    ang = 2.0 * np.pi * (kk[:, None] * kk[None, :]) / DEC_SEQ
    tw_c = jnp.broadcast_to(jnp.asarray(np.cos(ang), F32)[:, :, None], (n1, n1, 128))
    tw_s = jnp.broadcast_to(jnp.asarray(np.sin(ang), F32)[:, :, None], (n1, n1, 128))

    ncol = 4096
    ctx_tiles = T_CTX // n1 // n1
    a_blk = pl.BlockSpec((n1, ncol), lambda b, j: (ctx_tiles + b, j))
    y_blk = pl.BlockSpec((n1, ncol), lambda b, j: (b, j))
    m1_blk = pl.BlockSpec((2 * n1, 2 * n1), lambda b, j: (0, 0))
    y_sds = jax.ShapeDtypeStruct((DEC_BATCH * n1, wide), F32)
    yr, yi = pl.pallas_call(
        _fourier_s1_kernel,
        grid=(DEC_BATCH, wide // ncol),
        in_specs=[a_blk, a_blk, m1_blk, m1_blk],
        out_specs=[y_blk, y_blk],
        out_shape=[y_sds, y_sds],
        compiler_params=_cparams("parallel", "parallel"),
        name="fourier_latent_stage1",
    )(ar.reshape(T_ALL // n1, wide), ai.reshape(T_ALL // n1, wide), m1_hi, m1_lo)

    nk = 8
    z_blk = pl.BlockSpec((nk, n1, HALF), lambda b, k: (b * (n1 // nk) + k, 0, 0))
    tw_blk = pl.BlockSpec((nk, n1, 128), lambda b, k: (k, 0, 0))
    m3_blk = pl.BlockSpec((n1, 2 * n1), lambda b, k: (0, 0))
    out = pl.pallas_call(
        functools.partial(_fourier_s3_kernel, nk=nk),
        grid=(DEC_BATCH, n1 // nk),
        in_specs=[z_blk, z_blk, tw_blk, tw_blk, m3_blk, m3_blk],
        out_specs=pl.BlockSpec((n1, nk * HALF), lambda b, k: (b, k)),
        out_shape=jax.ShapeDtypeStruct((DEC_BATCH * n1, wide), BF16),
        compiler_params=_cparams("parallel", "parallel"),
        name="fourier_latent_stage3",
    )(yr.reshape(DEC_BATCH * n1, n1, HALF), yi.reshape(DEC_BATCH * n1, n1, HALF), tw_c, tw_s, m3_hi, m3_lo)
    return out.reshape(T_SMP, HALF)


def _retention_kernel(lg_ref, q_ref, k_ref, v_ref, g_ref, *rest, seq, chunk, nh, has_s0):
    if has_s0:
        s0_ref, y_ref, of_ref, ob_ref, dec_ref, xi_ref, zeta_ref = rest
        st_ref = None
    else:
        y_ref, st_ref, of_ref, ob_ref, dec_ref, xi_ref, zeta_ref = rest
        s0_ref = None
    L = chunk
    nc = seq // L
    hb = pl.program_id(1)
    diff = (lax.broadcasted_iota(jnp.int32, (L, L), 0) - lax.broadcasted_iota(jnp.int32, (L, L), 1)).astype(F32)
    li = lax.broadcasted_iota(jnp.int32, (L, HEAD_DIM), 0).astype(F32)
    one = jnp.ones((1, 1), F32)

    gcs = []
    for hh in range(nh):
        lgf = lg_ref[0, hb * nh + hh]
        lgb = lg_ref[1, hb * nh + hh]
        dec_ref[2 * hh] = jnp.where(diff >= 0, jnp.exp(lgf * jnp.maximum(diff, 0.0)), 0.0)
        dec_ref[2 * hh + 1] = jnp.where(diff <= 0, jnp.exp(lgb * jnp.maximum(-diff, 0.0)), 0.0)
        xi_ref[2 * hh] = jnp.exp(lgf * (li + 1.0))
        xi_ref[2 * hh + 1] = jnp.exp(lgb * (L - li))
        zeta_ref[2 * hh] = jnp.exp(lgf * (L - 1.0 - li))
        zeta_ref[2 * hh + 1] = jnp.exp(lgb * li)
        gcs.append(jnp.exp(one * (lgf * L)))
        gcs.append(jnp.exp(one * (lgb * L)))

    def chunk(c, s, t, sl, o_ref):
        r0 = pl.multiple_of(c * L, L)
        qc = q_ref[pl.ds(r0, L), sl]
        kc = k_ref[pl.ds(r0, L), sl]
        vc = v_ref[pl.ds(r0, L), sl]
        inner = _dot_nt(qc, kc.astype(BF16)) * dec_ref[t]
        o_ref[pl.ds(r0, L), sl] = _dot(inner.astype(BF16), vc) + _dot(qc, s.astype(BF16)) * xi_ref[t]
        return s * gcs[t] + _dot((kc * zeta_ref[t]).T.astype(BF16), vc)

    def scan_step(i, states):
        out = []
        for hh in range(nh):
            sl = slice(hh * HEAD_DIM, (hh + 1) * HEAD_DIM)
            out.append(chunk(i, states[2 * hh], 2 * hh, sl, of_ref))
            out.append(chunk(nc - 1 - i, states[2 * hh + 1], 2 * hh + 1, sl, ob_ref))
        return tuple(out)

    if has_s0:
        init = tuple(s0_ref[t % 2, t // 2] for t in range(2 * nh))
    else:
        init = tuple(jnp.zeros((HEAD_DIM, HEAD_DIM), F32) for _ in range(2 * nh))
    final = lax.fori_loop(0, nc, scan_step, init)
    if st_ref is not None:
        for t in range(2 * nh):
            st_ref[t % 2, t // 2] = final[t]

    def finish(c, carry):
        r0 = pl.multiple_of(c * L, L)
        o_all = of_ref[pl.ds(r0, L), :] + ob_ref[pl.ds(r0, L), :]
        gate = _silu(g_ref[pl.ds(r0, L), :])
        for hh in range(nh):
            sl = slice(hh * HEAD_DIM, (hh + 1) * HEAD_DIM)
            o = o_all[:, sl]
            mu = jnp.mean(o, axis=-1, keepdims=True)
            var = jnp.mean(jnp.square(o - mu), axis=-1, keepdims=True)
            y_ref[pl.ds(r0, L), sl] = (gate[:, sl] * ((o - mu) * lax.rsqrt(var + EPS))).astype(BF16)
        return carry

    lax.fori_loop(0, nc, finish, 0)


def _retention(lg, q, k, v, g, s0, *, seq, chunk, nbatch, row0, nh):
    has_s0 = s0 is not None
    tile0 = row0 // seq
    width = nh * HEAD_DIM
    blk = pl.BlockSpec((seq, width), lambda b, hb: (tile0 + b, hb))
    st_blk = pl.BlockSpec((None, 2, nh, HEAD_DIM, HEAD_DIM), lambda b, hb: (b, 0, hb, 0, 0))
    in_specs = [pl.BlockSpec(memory_space=pltpu.SMEM), blk, blk, blk, blk]
    args = [lg, q, k, v, g]
    y_spec = pl.BlockSpec((seq, width), lambda b, hb: (b, hb))
    y_sds = jax.ShapeDtypeStruct((nbatch * seq, HALF), BF16)
    if has_s0:
        in_specs.append(st_blk)
        args.append(s0)
        out_specs, out_shape = y_spec, y_sds
    else:
        out_specs = [y_spec, st_blk]
        out_shape = [y_sds, jax.ShapeDtypeStruct((nbatch, 2, N_HEADS, HEAD_DIM, HEAD_DIM), F32)]
    return pl.pallas_call(
        functools.partial(_retention_kernel, seq=seq, chunk=chunk, nh=nh, has_s0=has_s0),
        grid=(nbatch, N_HEADS // nh),
        in_specs=in_specs,
        out_specs=out_specs,
        out_shape=out_shape,
        scratch_shapes=[
            pltpu.VMEM((seq, width), F32),
            pltpu.VMEM((seq, width), F32),
            pltpu.VMEM((2 * nh, chunk, chunk), F32),
            pltpu.VMEM((2 * nh, chunk, HEAD_DIM), F32),
            pltpu.VMEM((2 * nh, chunk, HEAD_DIM), F32),
        ],
        compiler_params=_cparams("parallel", "parallel"),
        name="retention_%d" % seq,
    )(*args)


def _outproj_o_kernel(ycc_ref, ycs_ref, ydc_ref, yds_ref, x_ref, mod_ref, w_ref, g_ref, rh_ref, rl_ref,
                      x3_ref, h_ref, e1_ref, e2_ref, ga_ref, gb_ref, *, tm):
    is_ctx = pl.program_id(0) < T_CTX // tm
    y_c = jnp.where(is_ctx, ycc_ref[...], ycs_ref[...])
    y_d = jnp.where(is_ctx, ydc_ref[...], yds_ref[...])
    y = _dot(y_c, w_ref[0:HALF, :]) + _dot(y_d, w_ref[HALF:D_MODEL, :])
    m = mod_ref[...]
    x3 = x_ref[...] + m[:, 2 * D_MODEL:3 * D_MODEL] * y
    x3_ref[...] = x3
    h = _norm_mod(x3, g_ref[...], m[:, 4 * D_MODEL:5 * D_MODEL], m[:, 3 * D_MODEL:4 * D_MODEL])
    h_ref[...] = h
    h_hi, h_lo = _split(h)
    logits = _dot3(h_hi, h_lo, rh_ref[...], rl_ref[...])
    idx = lax.broadcasted_iota(jnp.int32, logits.shape, 1).astype(F32)
    logits = jnp.where(idx < float(N_EXPERTS), logits, -jnp.inf)
    m1 = logits.max(axis=-1, keepdims=True)
    e1 = jnp.where(logits == m1, idx, float(N_EXPERTS)).min(axis=-1, keepdims=True)
    rest = jnp.where(idx == e1, -jnp.inf, logits)
    m2 = rest.max(axis=-1, keepdims=True)
    e2 = jnp.where(rest == m2, idx, float(N_EXPERTS)).min(axis=-1, keepdims=True)
    ex = jnp.exp(m2 - m1)
    den = 1.0 + ex
    wide = (tm, 128)
    e1_ref[...] = jnp.broadcast_to(e1, wide).astype(jnp.int32)
    e2_ref[...] = jnp.broadcast_to(e2, wide).astype(jnp.int32)
    ga_ref[...] = jnp.broadcast_to(1.0 / den, wide)
    gb_ref[...] = jnp.broadcast_to(ex / den, wide)


def _out_proj_o(yc_ctx, yc_smp, yd_ctx, yd_smp, x, mod, w, g, r_hi, r_lo):
    tm = 256
    n_ctx = T_CTX // tm
    cidx = _cond_index(tm)
    row = lambda i: (i, 0)
    ctx_blk = pl.BlockSpec((tm, HALF), lambda i: (jnp.minimum(i, n_ctx - 1), 0))
    smp_blk = pl.BlockSpec((tm, HALF), lambda i: (jnp.maximum(i - n_ctx, 0), 0))
    sds = jax.ShapeDtypeStruct
    rep = pl.BlockSpec((tm, 128), row)
    return pl.pallas_call(
        functools.partial(_outproj_o_kernel, tm=tm),
        grid=(T_ALL // tm,),
        in_specs=[
            ctx_blk, smp_blk, ctx_blk, smp_blk,
            pl.BlockSpec((tm, D_MODEL), row),
            pl.BlockSpec((None, 1, N_MOD), lambda i: (cidx(i), 0, 0)),
            pl.BlockSpec((D_MODEL, D_MODEL), lambda i: (0, 0)),
            pl.BlockSpec((1, D_MODEL), lambda i: (0, 0)),
            pl.BlockSpec((D_MODEL, 128), lambda i: (0, 0)),
            pl.BlockSpec((D_MODEL, 128), lambda i: (0, 0)),
        ],
        out_specs=[pl.BlockSpec((tm, D_MODEL), row), pl.BlockSpec((tm, D_MODEL), row), rep, rep, rep, rep],
        out_shape=[
            sds((T_ALL, D_MODEL), F32), sds((T_ALL, D_MODEL), F32),
            sds((T_ALL, 128), jnp.int32), sds((T_ALL, 128), jnp.int32),
            sds((T_ALL, 128), F32), sds((T_ALL, 128), F32),
        ],
        compiler_params=_cparams("parallel"),
        name="out_proj_odd_route",
    )(yc_ctx, yc_smp, yd_ctx, yd_smp, x, mod, w, g, r_hi, r_lo)


def _routing_tables(e1, e2):
    flat_e = jnp.concatenate([e1, e2])
    onehot = (flat_e[:, None] == jnp.arange(N_EXPERTS, dtype=jnp.int32)[None, :]).astype(jnp.int32)
    csum = jnp.cumsum(onehot, axis=0)
    counts = csum[-1]
    padded = (counts + MOE_TM - 1) // MOE_TM * MOE_TM
    ends = jnp.cumsum(padded)
    pos = jnp.sum(onehot * (csum - 1 + (ends - padded)[None, :]), axis=1)
    tok = jnp.tile(jnp.arange(T_ALL, dtype=jnp.int32), 2)
    slot_t = jnp.zeros((MOE_P,), jnp.int32).at[pos].set(tok)
    block_row0 = jnp.arange(MOE_NB, dtype=jnp.int32) * MOE_TM
    block_e = jnp.minimum(
        jnp.sum((ends[None, :] <= block_row0[:, None]).astype(jnp.int32), axis=1), N_EXPERTS - 1
    ).astype(jnp.int32)
    n_valid = (ends[-1] // MOE_TM).astype(jnp.int32).reshape(1)
    return slot_t, pos[:T_ALL].astype(jnp.int32), pos[T_ALL:].astype(jnp.int32), block_e, n_valid


def _row_copy(src_ref, dst_ref, sem, src_row, dst_row):
    return pltpu.make_async_copy(src_ref.at[pl.ds(src_row, 1), :], dst_ref.at[pl.ds(dst_row, 1), :], sem)


def _start_rows(src_ref, dst_ref, sem, idx_ref, tm):
    def start(r, c):
        _row_copy(src_ref, dst_ref, sem, idx_ref[0, r], r).start()
        return c

    lax.fori_loop(0, tm, start, 0, unroll=8)


def _wait_rows(src_ref, dst_ref, sem, tm):
    def wait(r, c):
        _row_copy(src_ref, dst_ref, sem, 0, r).wait()
        return c

    lax.fori_loop(0, tm, wait, 0, unroll=8)


def _gather_kernel(idx_ref, nxt_ref, src_ref, o_ref, buf_ref, sems, *, tm):
    i = pl.program_id(0)
    slot = i % 2

    @pl.when(i == 0)
    def _():
        _start_rows(src_ref, buf_ref.at[0], sems.at[0], idx_ref, tm)

    @pl.when(i + 1 < pl.num_programs(0))
    def _():
        _start_rows(src_ref, buf_ref.at[1 - slot], sems.at[1 - slot], nxt_ref, tm)

    _wait_rows(src_ref, buf_ref.at[slot], sems.at[slot], tm)
    o_ref[...] = buf_ref[slot]


def _next_idx_spec(tm, nblk):
    return pl.BlockSpec((None, 1, tm), lambda i: (jnp.minimum(i + 1, nblk - 1), 0, 0), memory_space=pltpu.SMEM)


def _gather_rows(src, slot_t):
    tm = GATHER_TM
    nblk = MOE_P // tm
    idx = slot_t.reshape(nblk, 1, tm)
    return pl.pallas_call(
        functools.partial(_gather_kernel, tm=tm),
        grid=(nblk,),
        in_specs=[
            pl.BlockSpec((None, 1, tm), lambda i: (i, 0, 0), memory_space=pltpu.SMEM),
            _next_idx_spec(tm, nblk),
            pl.BlockSpec(memory_space=pl.ANY),
        ],
        out_specs=pl.BlockSpec((tm, D_MODEL), lambda i: (i, 0)),
        out_shape=jax.ShapeDtypeStruct((MOE_P, D_MODEL), F32),
        scratch_shapes=[pltpu.VMEM((2, tm, D_MODEL), F32), pltpu.SemaphoreType.DMA((2,))],
        compiler_params=_cparams("arbitrary"),
        name="moe_gather",
    )(idx, idx, src)


def _experts_kernel(be_ref, nv_ref, x_ref, wg_ref, wu_ref, wd_ref, o_ref, xb_ref, acc_ref):
    i = pl.program_id(0)
    f = pl.program_id(1)
    valid = i < nv_ref[0]

    @pl.when(jnp.logical_and(valid, f == 0))
    def _():
        xb_ref[...] = x_ref[...].astype(BF16)
        acc_ref[...] = jnp.zeros_like(acc_ref)

    @pl.when(valid)
    def _():
        x = xb_ref[...]
        a = _silu(_dot(x, wg_ref[...])) * _dot(x, wu_ref[...])
        acc_ref[...] += _dot(a.astype(BF16), wd_ref[...])

    last = f == pl.num_programs(1) - 1

    @pl.when(jnp.logical_and(valid, last))
    def _():
        o_ref[...] = acc_ref[...]

    @pl.when(jnp.logical_and(jnp.logical_not(valid), last))
    def _():
        o_ref[...] = jnp.zeros_like(o_ref)


def _experts(xs, block_e, n_valid, wg, wu, wd):
    nf = D_FF_EXPERT // MOE_TF

    def f_eff(i, f, nv):
        return jnp.where(i < nv[0], f, nf - 1)

    return pl.pallas_call(
        _experts_kernel,
        grid_spec=pltpu.PrefetchScalarGridSpec(
            num_scalar_prefetch=2,
            grid=(MOE_NB, nf),
            in_specs=[
                pl.BlockSpec((MOE_TM, D_MODEL), lambda i, f, be, nv: (i, 0)),
                pl.BlockSpec((None, D_MODEL, MOE_TF), lambda i, f, be, nv: (be[i], 0, f_eff(i, f, nv))),
                pl.BlockSpec((None, D_MODEL, MOE_TF), lambda i, f, be, nv: (be[i], 0, f_eff(i, f, nv))),
                pl.BlockSpec((None, MOE_TF, D_MODEL), lambda i, f, be, nv: (be[i], f_eff(i, f, nv), 0)),
            ],
            out_specs=pl.BlockSpec((MOE_TM, D_MODEL), lambda i, f, be, nv: (i, 0)),
            scratch_shapes=[pltpu.VMEM((MOE_TM, D_MODEL), BF16), pltpu.VMEM((MOE_TM, D_MODEL), F32)],
        ),
        out_shape=jax.ShapeDtypeStruct((MOE_P, D_MODEL), F32),
        compiler_params=_cparams("arbitrary", "arbitrary"),
        name="moe_experts",
    )(block_e, n_valid, xs, wg, wu, wd)


def _combine_kernel(p1_ref, p2_ref, n1_ref, n2_ref, ys_ref, x_ref, ga_ref, gb_ref, mod_ref, g_ref, oc_ref, os_ref,
                    a_ref, b_ref, sems, *, tm):
    i = pl.program_id(0)
    slot = i % 2

    def start_tile(s, i1_ref, i2_ref):
        _start_rows(ys_ref, a_ref.at[s], sems.at[0, s], i1_ref, tm)
        _start_rows(ys_ref, b_ref.at[s], sems.at[1, s], i2_ref, tm)

    @pl.when(i == 0)
    def _():
        start_tile(0, p1_ref, p2_ref)

    @pl.when(i + 1 < pl.num_programs(0))
    def _():
        start_tile(1 - slot, n1_ref, n2_ref)

    _wait_rows(ys_ref, a_ref.at[slot], sems.at[0, slot], tm)
    _wait_rows(ys_ref, b_ref.at[slot], sems.at[1, slot], tm)
    reps = D_MODEL // 128
    ga = jnp.concatenate([ga_ref[...]] * reps, axis=-1)
    gb = jnp.concatenate([gb_ref[...]] * reps, axis=-1)
    f = a_ref[slot] * ga + b_ref[slot] * gb
    x = x_ref[...] + mod_ref[:, 5 * D_MODEL:6 * D_MODEL] * f
    y = (x * lax.rsqrt(jnp.mean(x * x, axis=-1, keepdims=True) + EPS)) * g_ref[...]
    is_ctx = pl.program_id(0) < T_CTX // tm

    @pl.when(is_ctx)
    def _():
        oc_ref[...] = y

    @pl.when(jnp.logical_not(is_ctx))
    def _():
        os_ref[...] = y


def _combine(ys, pos1, pos2, x, ga, gb, mod, final_g):
    tm = GATHER_TM
    nblk = T_ALL // tm
    cidx = _cond_index(tm)
    row = lambda i: (i, 0)
    idx_blk = pl.BlockSpec((None, 1, tm), lambda i: (i, 0, 0), memory_space=pltpu.SMEM)
    idx1 = pos1.reshape(nblk, 1, tm)
    idx2 = pos2.reshape(nblk, 1, tm)
    return pl.pallas_call(
        functools.partial(_combine_kernel, tm=tm),
        grid=(nblk,),
        in_specs=[
            idx_blk, idx_blk, _next_idx_spec(tm, nblk), _next_idx_spec(tm, nblk),
            pl.BlockSpec(memory_space=pl.ANY),
            pl.BlockSpec((tm, D_MODEL), row),
            pl.BlockSpec((tm, 128), row),
            pl.BlockSpec((tm, 128), row),
            pl.BlockSpec((None, 1, N_MOD), lambda i: (cidx(i), 0, 0)),
            pl.BlockSpec((1, D_MODEL), lambda i: (0, 0)),
        ],
        out_specs=list(_ctx_smp_specs(tm, D_MODEL)),
        out_shape=[jax.ShapeDtypeStruct((T_CTX, D_MODEL), F32), jax.ShapeDtypeStruct((T_SMP, D_MODEL), F32)],
        scratch_shapes=[
            pltpu.VMEM((2, tm, D_MODEL), F32),
            pltpu.VMEM((2, tm, D_MODEL), F32),
            pltpu.SemaphoreType.DMA((2, 2)),
        ],
        compiler_params=_cparams("arbitrary"),
        name="moe_combine_final_norm",
    )(idx1, idx2, idx1, idx2, ys, x, ga, gb, mod, final_g)


def kernel(x_prompt, x_sample, cache_na_k, cache_na_v, state_ret, c, c_ctx, w_ada, b_ada, norm_g, final_g, w_in_e, w_conv_e, na_rpb_e, w_out_e, ff_gate_e, ff_up_e, ff_down_e, w_in_o, ret_decay_o, w_out_o, router_o, ex_gate_o, ex_up_o, ex_down_o):
    x_ctx = x_prompt.reshape(T_CTX, D_MODEL)
    x_smp = x_sample.reshape(T_SMP, D_MODEL)
    cond = jnp.concatenate([c_ctx[None, :], c, jnp.zeros((N_COND - 1 - DEC_BATCH, D_MODEL), F32)], axis=0)
    mod = _ada(cond, w_ada, b_ada)
    mod0 = mod[0].reshape(N_COND, 1, N_MOD)
    mod1 = mod[1].reshape(N_COND, 1, N_MOD)

    bcu, q, k, v, kb, vb = _in_proj_e(x_ctx, x_smp, mod0, norm_g[0, 0][None, :], w_in_e[0].astype(BF16))
    o_ctx = _attn_ctx(q, kb, vb)
    o_smp = _attn_na(q, kb, vb,
                     cache_na_k[:, 0].reshape(DEC_BATCH, PAST_LEN, HALF),
                     cache_na_v[:, 0].reshape(DEC_BATCH, PAST_LEN, HALF),
                     _na_bias_table(na_rpb_e[0]))
    x1 = _out_proj_e(bcu, o_ctx, o_smp, x_ctx, x_smp, mod0, w_conv_e[0].T, w_out_e[0].astype(BF16))
    x2 = _ffn(x1, mod0, norm_g[0, 1][None, :], ff_gate_e[0].astype(BF16), ff_up_e[0].astype(BF16),
              ff_down_e[0].astype(BF16))

    mc_hi, mc_lo = _hi_lo(_channel_dft_table())
    ar, ai, q1, k1, v1, g1 = _in_proj_o(x2, mod1, norm_g[1, 0][None, :], w_in_o[0].astype(BF16), mc_hi, mc_lo)
    yc_ctx = _fourier_ctx(ar, ai)
    yc_smp = _fourier_smp(ar, ai)
    lg = jax.nn.log_sigmoid(ret_decay_o[0].astype(F32))
    yd_ctx, new_state = _retention(lg, q1, k1, v1, g1, None, seq=SEQ, chunk=RET_CHUNK_CTX, nbatch=BATCH, row0=0,
                                   nh=8)
    yd_smp = _retention(lg, q1, k1, v1, g1, state_ret[:, 0], seq=DEC_SEQ, chunk=RET_CHUNK_SMP, nbatch=DEC_BATCH,
                        row0=T_CTX, nh=4)
    r_hi, r_lo = _hi_lo(jnp.pad(router_o[0], ((0, 0), (0, 128 - N_EXPERTS))))
    x3, hm, e1, e2, ga, gb = _out_proj_o(yc_ctx, yc_smp, yd_ctx, yd_smp, x2, mod1, w_out_o[0].astype(BF16),
                                         norm_g[1, 1][None, :], r_hi, r_lo)
    slot_t, pos1, pos2, block_e, n_valid = _routing_tables(e1[:, 0], e2[:, 0])
    xs = _gather_rows(hm, slot_t)
    ys = _experts(xs, block_e, n_valid, ex_gate_o[0].astype(BF16), ex_up_o[0].astype(BF16),
                  ex_down_o[0].astype(BF16))
    y_ctx, y_smp = _combine(ys, pos1, pos2, x3, ga, gb, mod1, final_g[None, :])

    y_prompt = y_ctx.reshape(BATCH, SEQ, D_MODEL)
    y_sample = y_smp.reshape(DEC_BATCH, DEC_SEQ, D_MODEL)
    new_na_k = k[:T_CTX].reshape(BATCH, 1, SEQ, N_HEADS, HEAD_DIM)
    new_na_v = v[:T_CTX].reshape(BATCH, 1, SEQ, N_HEADS, HEAD_DIM)
    new_state_ret = new_state.reshape(BATCH, 1, 2, N_HEADS, HEAD_DIM, HEAD_DIM)
    return (y_prompt, y_sample, new_na_k, new_na_v, new_state_ret)
```

```python
import functools

import numpy as np
import jax
import jax.numpy as jnp
from jax import lax
from jax.experimental import pallas as pl
from jax.experimental.pallas import tpu as pltpu

D_MODEL = 1024
BATCH = 32
SEQ = 256
DEC_BATCH = 4
DEC_SEQ = 4096
PAST_LEN = 256
GRID_W = 64
HEAD_DIM = 64
HALF = D_MODEL // 2
N_HEADS = HALF // HEAD_DIM
NA_WIN_R = 8
NA_WIN_C = 16
FT_GROUPS = 4
FT_GROUP_CH = HALF // FT_GROUPS
RET_CHUNK_CTX = 256
RET_CHUNK_SMP = 512
D_FF = 2816
N_EXPERTS = 8
D_FF_EXPERT = 3584
EPS = 1e-6
NEG_INF = -1e30
ATT_SCALE = HEAD_DIM ** -0.5

T_CTX = BATCH * SEQ
T_SMP = DEC_BATCH * DEC_SEQ
T_ALL = T_CTX + T_SMP
N_COND = 8
N_MOD = 6 * D_MODEL
GRID_H = DEC_SEQ // GRID_W
FT_N1 = 64

F32 = jnp.float32
BF16 = jnp.bfloat16
VMEM_LIMIT = 56 * 1024 * 1024

MOE_TM = 512
MOE_TF = 1792
MOE_A = 2 * T_ALL
MOE_NB = MOE_A // MOE_TM + N_EXPERTS
MOE_P = MOE_NB * MOE_TM
GATHER_TM = 256


def _cparams(*sem):
    return pltpu.CompilerParams(dimension_semantics=sem, vmem_limit_bytes=VMEM_LIMIT)


def _cond_index(tm):
    n_ctx = T_CTX // tm
    per_b = DEC_SEQ // tm

    def f(i):
        return jnp.where(i < n_ctx, 0, 1 + (i - n_ctx) // per_b)

    return f


def _ctx_smp_specs(tm, width):
    n_ctx = T_CTX // tm
    return (pl.BlockSpec((tm, width), lambda i: (jnp.minimum(i, n_ctx - 1), 0)),
            pl.BlockSpec((tm, width), lambda i: (jnp.maximum(i - n_ctx, 0), 0)))


def _pick_ctx_smp(tm, c_ref, s_ref):
    return jnp.where(pl.program_id(0) < T_CTX // tm, c_ref[...], s_ref[...])


def _silu(x):
    return x * (1.0 / (1.0 + jnp.exp(-x)))


def _norm_mod(x, g, scale, shift):
    y = x * lax.rsqrt(jnp.mean(x * x, axis=-1, keepdims=True) + EPS)
    return (y * g) * (1.0 + scale) + shift


def _split(a):
    hi = a.astype(BF16)
    lo = (a - hi.astype(F32)).astype(BF16)
    return hi, lo


def _dot(a, b):
    return jnp.dot(a, b, preferred_element_type=F32)


def _dot_nt(a, b):
    return lax.dot_general(a, b, (((1,), (1,)), ((), ())), preferred_element_type=F32)


def _dot3(a_hi, a_lo, b_hi, b_lo):
    return _dot(a_hi, b_hi) + (_dot(a_hi, b_lo) + _dot(a_lo, b_hi))


def _ada_kernel(c_ref, w_ref, b_ref, o_ref):
    s = _silu(c_ref[...]).astype(BF16)
    o_ref[...] = _dot(s, w_ref[...].astype(BF16)) + b_ref[...]


def _ada(cond, w_ada, b_ada):
    depth = w_ada.shape[0]
    tn = 1536
    return pl.pallas_call(
        _ada_kernel,
        grid=(depth, N_MOD // tn),
        in_specs=[
            pl.BlockSpec((N_COND, D_MODEL), lambda l, j: (0, 0)),
            pl.BlockSpec((None, D_MODEL, tn), lambda l, j: (l, 0, j)),
            pl.BlockSpec((None, 1, tn), lambda l, j: (l, 0, j)),
        ],
        out_specs=pl.BlockSpec((None, N_COND, tn), lambda l, j: (l, 0, j)),
        out_shape=jax.ShapeDtypeStruct((depth, N_COND, N_MOD), F32),
        compiler_params=_cparams("parallel", "parallel"),
        name="ada_mod",
    )(cond, w_ada, b_ada.reshape(depth, 1, N_MOD))


def _inproj_e_kernel(xc_ref, xs_ref, mod_ref, g_ref, w_ref, bcu_ref, q_ref, k_ref, v_ref, kb_ref, vb_ref, *, tm):
    m = mod_ref[...]
    x = _pick_ctx_smp(tm, xc_ref, xs_ref)
    h = _norm_mod(x, g_ref[...], m[:, D_MODEL:2 * D_MODEL], m[:, 0:D_MODEL]).astype(BF16)
    bcu_ref[...] = _dot(h, w_ref[:, 0:3 * HALF])
    q_ref[...] = _dot(h, w_ref[:, 3 * HALF:4 * HALF]).astype(BF16)
    k = _dot(h, w_ref[:, 4 * HALF:5 * HALF])
    k_ref[...] = k
    kb_ref[...] = k.astype(BF16)
    v = _dot(h, w_ref[:, 5 * HALF:6 * HALF])
    v_ref[...] = v
    vb_ref[...] = v.astype(BF16)


def _in_proj_e(x_ctx, x_smp, mod, g, w):
    tm = 512
    cidx = _cond_index(tm)
    row = lambda i: (i, 0)
    sds = jax.ShapeDtypeStruct
    return pl.pallas_call(
        functools.partial(_inproj_e_kernel, tm=tm),
        grid=(T_ALL // tm,),
        in_specs=[
            *_ctx_smp_specs(tm, D_MODEL),
            pl.BlockSpec((None, 1, N_MOD), lambda i: (cidx(i), 0, 0)),
            pl.BlockSpec((1, D_MODEL), lambda i: (0, 0)),
            pl.BlockSpec((D_MODEL, 6 * HALF), lambda i: (0, 0)),
        ],
        out_specs=[
            pl.BlockSpec((tm, 3 * HALF), row),
            pl.BlockSpec((tm, HALF), row),
            pl.BlockSpec((tm, HALF), row),
            pl.BlockSpec((tm, HALF), row),
            pl.BlockSpec((tm, HALF), row),
            pl.BlockSpec((tm, HALF), row),
        ],
        out_shape=[
            sds((T_ALL, 3 * HALF), F32),
            sds((T_ALL, HALF), BF16),
            sds((T_ALL, HALF), F32),
            sds((T_ALL, HALF), F32),
            sds((T_ALL, HALF), BF16),
            sds((T_ALL, HALF), BF16),
        ],
        compiler_params=_cparams("parallel"),
        name="in_proj_even",
    )(x_ctx, x_smp, mod, g, w)


def _softmax_parts(parts):
    m = parts[0].max(axis=-1, keepdims=True)
    for s in parts[1:]:
        m = jnp.maximum(m, s.max(axis=-1, keepdims=True))
    es = [jnp.exp(s - m) for s in parts]
    l = es[0].sum(axis=-1, keepdims=True)
    for e in es[1:]:
        l = l + e.sum(axis=-1, keepdims=True)
    inv = 1.0 / l
    return [e * inv for e in es]


def _attn_ctx_kernel(q_ref, k_ref, v_ref, o_ref):
    for h in range(N_HEADS):
        sl = slice(h * HEAD_DIM, (h + 1) * HEAD_DIM)
        s = _dot_nt(q_ref[:, sl], k_ref[:, sl]) * ATT_SCALE
        (p,) = _softmax_parts([s])
        o_ref[:, sl] = _dot(p.astype(BF16), v_ref[:, sl]).astype(BF16)


def _attn_ctx(q, kb, vb):
    blk = pl.BlockSpec((SEQ, HALF), lambda b: (b, 0))
    return pl.pallas_call(
        _attn_ctx_kernel,
        grid=(BATCH,),
        in_specs=[blk, blk, blk],
        out_specs=blk,
        out_shape=jax.ShapeDtypeStruct((T_CTX, HALF), BF16),
        compiler_params=_cparams("parallel"),
        name="attn_context",
    )(q, kb, vb)


NA_Q_ROWS = 4
NA_K_ROWS = NA_WIN_R + NA_Q_ROWS


def _na_key_row0(rb):
    return jnp.clip(rb * NA_Q_ROWS - NA_WIN_R // 2, 0, GRID_H - NA_K_ROWS)


def _attn_na_kernel(q_ref, k_ref, v_ref, kc_ref, vc_ref, bias_ref, o_ref):
    rb = pl.program_id(1)
    start = pl.multiple_of(_na_key_row0(rb) * GRID_W, GRID_W)
    n_loc = NA_K_ROWS * GRID_W
    for h in range(N_HEADS):
        sl = slice(h * HEAD_DIM, (h + 1) * HEAD_DIM)
        q = q_ref[:, sl]
        s_loc = _dot_nt(q, k_ref[pl.ds(start, n_loc), sl]) * ATT_SCALE + bias_ref[h]
        s_ctx = _dot_nt(q, kc_ref[:, sl].astype(BF16)) * ATT_SCALE
        p_loc, p_ctx = _softmax_parts([s_loc, s_ctx])
        o = _dot(p_loc.astype(BF16), v_ref[pl.ds(start, n_loc), sl])
        o = o + _dot(p_ctx.astype(BF16), vc_ref[:, sl].astype(BF16))
        o_ref[:, sl] = o.astype(BF16)


def _na_bias_table(rpb):
    cols = np.arange(GRID_W)
    col_start = np.clip(cols - NA_WIN_C // 2, 0, GRID_W - NA_WIN_C)
    col_in = (cols[None, :] >= col_start[:, None]) & (cols[None, :] < col_start[:, None] + NA_WIN_C)
    col_off = np.clip(cols[None, :] - cols[:, None], 1 - NA_WIN_C, NA_WIN_C - 1) + (NA_WIN_C - 1)
    n_off = 2 * NA_WIN_C - 1
    pick = jnp.asarray(col_off[None] == np.arange(n_off)[:, None, None], F32)
    by_col = (rpb.astype(F32)[:, :, :, None, None] * pick[None, None]).sum(axis=2)
    by_col = jnp.pad(by_col, ((0, 0), (NA_K_ROWS, NA_K_ROWS), (0, 0), (0, 0)))
    n_blocks = GRID_H // NA_Q_ROWS
    variants = []
    for rb in (0, 1, n_blocks - 1):
        k_row0 = int(np.clip(rb * NA_Q_ROWS - NA_WIN_R // 2, 0, GRID_H - NA_K_ROWS))
        k_row = k_row0 + np.arange(NA_K_ROWS)
        per_q_row = []
        for qr in range(NA_Q_ROWS):
            q_row = rb * NA_Q_ROWS + qr
            win0 = int(np.clip(q_row - NA_WIN_R // 2, 0, GRID_H - NA_WIN_R))
            row_in = (k_row >= win0) & (k_row < win0 + NA_WIN_R)
            ro0 = k_row0 - q_row + (NA_WIN_R - 1) + NA_K_ROWS
            b = by_col[:, ro0:ro0 + NA_K_ROWS]
            keep = row_in[:, None, None] & col_in[None]
            per_q_row.append(jnp.where(keep[None], b, NEG_INF))
        b = jnp.stack(per_q_row, axis=1)
        variants.append(b.transpose(0, 1, 3, 2, 4).reshape(N_HEADS, NA_Q_ROWS * GRID_W, NA_K_ROWS * GRID_W))
    return jnp.stack(variants)


def _attn_na(q, kb, vb, cache_k, cache_v, bias):
    tq = NA_Q_ROWS * GRID_W
    n_blocks = GRID_H // NA_Q_ROWS
    ctx_q_tiles = T_CTX // tq
    ctx_b_tiles = T_CTX // DEC_SEQ

    def bias_idx(b, rb):
        return (jnp.where(rb == 0, 0, jnp.where(rb == n_blocks - 1, 2, 1)), 0, 0, 0)

    full = pl.BlockSpec((DEC_SEQ, HALF), lambda b, r: (ctx_b_tiles + b, 0))
    cache = pl.BlockSpec((None, PAST_LEN, HALF), lambda b, r: (b, 0, 0))
    return pl.pallas_call(
        _attn_na_kernel,
        grid=(DEC_BATCH, n_blocks),
        in_specs=[
            pl.BlockSpec((tq, HALF), lambda b, r: (ctx_q_tiles + b * n_blocks + r, 0)),
            full, full, cache, cache,
            pl.BlockSpec((None, N_HEADS, tq, NA_K_ROWS * GRID_W), bias_idx),
        ],
        out_specs=pl.BlockSpec((tq, HALF), lambda b, r: (b * n_blocks + r, 0)),
        out_shape=jax.ShapeDtypeStruct((T_SMP, HALF), BF16),
        compiler_params=_cparams("parallel", "arbitrary"),
        name="attn_neighbourhood",
    )(q, kb, vb, cache_k, cache_v, bias)


def _seq_edges(i, tm):
    n_ctx = T_CTX // tm
    ctx_per = SEQ // tm
    smp_per = DEC_SEQ // tm
    j = i - n_ctx
    first = jnp.where(i < n_ctx, i % ctx_per == 0, j % smp_per == 0)
    last = jnp.where(i < n_ctx, i % ctx_per == ctx_per - 1, j % smp_per == smp_per - 1)
    return first, last


def _outproj_e_kernel(bcu_ref, prev_ref, next_ref, oc_ref, os_ref, xc_ref, xs_ref, mod_ref, wc_ref, w_ref, o_ref,
                      *, tm):
    i = pl.program_id(0)
    first, last = _seq_edges(i, tm)
    bcu = bcu_ref[...]
    cu = bcu[:, HALF:2 * HALF] * bcu[:, 2 * HALF:3 * HALF]
    pv = prev_ref[7:8, :]
    nx = next_ref[0:1, :]
    cu_prev_row = jnp.where(first, 0.0, pv[:, HALF:2 * HALF] * pv[:, 2 * HALF:3 * HALF])
    cu_next_row = jnp.where(last, 0.0, nx[:, HALF:2 * HALF] * nx[:, 2 * HALF:3 * HALF])
    rows = lax.broadcasted_iota(jnp.int32, (tm, HALF), 0)
    cu_prev = jnp.where(rows == 0, cu_prev_row, pltpu.roll(cu, 1, axis=0))
    cu_next = jnp.where(rows == tm - 1, cu_next_row, pltpu.roll(cu, tm - 1, axis=0))
    wc = wc_ref[...]
    y_a = bcu[:, 0:HALF] * (cu_prev * wc[0:1] + cu * wc[1:2] + cu_next * wc[2:3])
    y_b = _pick_ctx_smp(tm, oc_ref, os_ref)
    y = _dot(y_a.astype(BF16), w_ref[0:HALF, :]) + _dot(y_b, w_ref[HALF:D_MODEL, :])
    o_ref[...] = _pick_ctx_smp(tm, xc_ref, xs_ref) + mod_ref[:, 2 * D_MODEL:3 * D_MODEL] * y


def _out_proj_e(bcu, o_ctx, o_smp, x_ctx, x_smp, mod, w_conv_t, w):
    tm = 256
    cidx = _cond_index(tm)
    row = lambda i: (i, 0)
    nb8 = T_ALL // 8
    return pl.pallas_call(
        functools.partial(_outproj_e_kernel, tm=tm),
        grid=(T_ALL // tm,),
        in_specs=[
            pl.BlockSpec((tm, 3 * HALF), row),
            pl.BlockSpec((8, 3 * HALF), lambda i: (jnp.maximum(i * (tm // 8) - 1, 0), 0)),
            pl.BlockSpec((8, 3 * HALF), lambda i: (jnp.minimum((i + 1) * (tm // 8), nb8 - 1), 0)),
            *_ctx_smp_specs(tm, HALF),
            *_ctx_smp_specs(tm, D_MODEL),
            pl.BlockSpec((None, 1, N_MOD), lambda i: (cidx(i), 0, 0)),
            pl.BlockSpec((3, HALF), lambda i: (0, 0)),
            pl.BlockSpec((D_MODEL, D_MODEL), lambda i: (0, 0)),
        ],
        out_specs=pl.BlockSpec((tm, D_MODEL), row),
        out_shape=jax.ShapeDtypeStruct((T_ALL, D_MODEL), F32),
        compiler_params=_cparams("parallel"),
        name="out_proj_even",
    )(bcu, bcu, bcu, o_ctx, o_smp, x_ctx, x_smp, mod, w_conv_t, w)


def _ffn_kernel(x_ref, mod_ref, g_ref, wg_ref, wu_ref, wd_ref, o_ref, h_ref, acc_ref):
    f = pl.program_id(1)

    @pl.when(f == 0)
    def _():
        m = mod_ref[...]
        h = _norm_mod(x_ref[...], g_ref[...], m[:, 4 * D_MODEL:5 * D_MODEL], m[:, 3 * D_MODEL:4 * D_MODEL])
        h_ref[...] = h.astype(BF16)
        acc_ref[...] = jnp.zeros_like(acc_ref)

    h = h_ref[...]
    a = _silu(_dot(h, wg_ref[...])) * _dot(h, wu_ref[...])
    acc_ref[...] += _dot(a.astype(BF16), wd_ref[...])

    @pl.when(f == pl.num_programs(1) - 1)
    def _():
        o_ref[...] = x_ref[...] + mod_ref[:, 5 * D_MODEL:6 * D_MODEL] * acc_ref[...]


def _ffn(x, mod, g, wg, wu, wd):
    tm = 512
    tf = D_FF // 2
    cidx = _cond_index(tm)
    return pl.pallas_call(
        _ffn_kernel,
        grid=(T_ALL // tm, D_FF // tf),
        in_specs=[
            pl.BlockSpec((tm, D_MODEL), lambda i, f: (i, 0)),
            pl.BlockSpec((None, 1, N_MOD), lambda i, f: (cidx(i), 0, 0)),
            pl.BlockSpec((1, D_MODEL), lambda i, f: (0, 0)),
            pl.BlockSpec((D_MODEL, tf), lambda i, f: (0, f)),
            pl.BlockSpec((D_MODEL, tf), lambda i, f: (0, f)),
            pl.BlockSpec((tf, D_MODEL), lambda i, f: (f, 0)),
        ],
        out_specs=pl.BlockSpec((tm, D_MODEL), lambda i, f: (i, 0)),
        out_shape=jax.ShapeDtypeStruct((T_ALL, D_MODEL), F32),
        scratch_shapes=[pltpu.VMEM((tm, D_MODEL), BF16), pltpu.VMEM((tm, D_MODEL), F32)],
        compiler_params=_cparams("parallel", "arbitrary"),
        name="ffn_dense",
    )(x, mod, g, wg, wu, wd)


def _inproj_o_kernel(x_ref, mod_ref, g_ref, w_ref, mch_ref, mcl_ref, ar_ref, ai_ref, q_ref, k_ref, v_ref, gt_ref):
    m = mod_ref[...]
    h = _norm_mod(x_ref[...], g_ref[...], m[:, D_MODEL:2 * D_MODEL], m[:, 0:D_MODEL]).astype(BF16)
    u_hi, u_lo = _split(_dot(h, w_ref[:, 0:HALF]))
    for grp in range(FT_GROUPS):
        sl = slice(grp * FT_GROUP_CH, (grp + 1) * FT_GROUP_CH)
        a = _dot3(u_hi[:, sl], u_lo[:, sl], mch_ref[...], mcl_ref[...])
        ar_ref[:, sl] = a[:, 0:FT_GROUP_CH]
        ai_ref[:, sl] = a[:, FT_GROUP_CH:2 * FT_GROUP_CH]
    q_ref[...] = _dot(h, w_ref[:, HALF:2 * HALF]).astype(BF16)
    k_ref[...] = _dot(h, w_ref[:, 2 * HALF:3 * HALF]) * ATT_SCALE
    v_ref[...] = _dot(h, w_ref[:, 3 * HALF:4 * HALF]).astype(BF16)
    gt_ref[...] = _dot(h, w_ref[:, 4 * HALF:5 * HALF])


def _in_proj_o(x, mod, g, w, mc_hi, mc_lo):
    tm = 512
    cidx = _cond_index(tm)
    row = lambda i: (i, 0)
    half_out = pl.BlockSpec((tm, HALF), row)
    sds = jax.ShapeDtypeStruct
    return pl.pallas_call(
        _inproj_o_kernel,
        grid=(T_ALL // tm,),
        in_specs=[
            pl.BlockSpec((tm, D_MODEL), row),
            pl.BlockSpec((None, 1, N_MOD), lambda i: (cidx(i), 0, 0)),
            pl.BlockSpec((1, D_MODEL), lambda i: (0, 0)),
            pl.BlockSpec((D_MODEL, 5 * HALF), lambda i: (0, 0)),
            pl.BlockSpec((FT_GROUP_CH, 2 * FT_GROUP_CH), lambda i: (0, 0)),
            pl.BlockSpec((FT_GROUP_CH, 2 * FT_GROUP_CH), lambda i: (0, 0)),
        ],
        out_specs=[half_out] * 6,
        out_shape=[
            sds((T_ALL, HALF), F32),
            sds((T_ALL, HALF), F32),
            sds((T_ALL, HALF), BF16),
            sds((T_ALL, HALF), F32),
            sds((T_ALL, HALF), BF16),
            sds((T_ALL, HALF), F32),
        ],
        compiler_params=_cparams("parallel"),
        name="in_proj_odd",
    )(x, mod, g, w, mc_hi, mc_lo)


def _dft_cos_sin(n):
    k = np.arange(n)
    ang = 2.0 * np.pi * ((k[:, None] * k[None, :]) % n) / n
    return np.cos(ang), np.sin(ang)


def _hi_lo(a):
    a = jnp.asarray(a, F32)
    hi = a.astype(BF16)
    return hi, (a - hi.astype(F32)).astype(BF16)


def _channel_dft_table():
    c, s = _dft_cos_sin(FT_GROUP_CH)
    scale = FT_GROUP_CH ** -0.5
    return np.concatenate([c * scale, -s * scale], axis=1)


def _fourier_ctx_kernel(ar_ref, ai_ref, th_ref, tl_ref, o_ref):
    a_hi, a_lo = _split(jnp.concatenate([ar_ref[...], ai_ref[...]], axis=0))
    o_ref[...] = _dot3(th_ref[...], tl_ref[...], a_hi, a_lo).astype(BF16)


def _fourier_ctx(ar, ai):
    c, s = _dft_cos_sin(SEQ)
    t_hi, t_lo = _hi_lo(np.concatenate([c, s], axis=1) * SEQ ** -0.5)
    blk = pl.BlockSpec((SEQ, HALF), lambda b: (b, 0))
    tab = pl.BlockSpec((SEQ, 2 * SEQ), lambda b: (0, 0))
    return pl.pallas_call(
        _fourier_ctx_kernel,
        grid=(BATCH,),
        in_specs=[blk, blk, tab, tab],
        out_specs=blk,
        out_shape=jax.ShapeDtypeStruct((T_CTX, HALF), BF16),
        compiler_params=_cparams("parallel"),
        name="fourier_context",
    )(ar, ai, t_hi, t_lo)


def _fourier_s1_kernel(ar_ref, ai_ref, mh_ref, ml_ref, yr_ref, yi_ref):
    def one_p2(p2, c):
        rows = pl.ds(p2, FT_N1, stride=FT_N1)
        a_hi, a_lo = _split(jnp.concatenate([ar_ref[rows, :], ai_ref[rows, :]], axis=0))
        y = _dot3(mh_ref[...], ml_ref[...], a_hi, a_lo)
        yr_ref[rows, :] = y[0:FT_N1]
        yi_ref[rows, :] = y[FT_N1:2 * FT_N1]
        return c

    lax.fori_loop(0, FT_N1, one_p2, 0)


def _fourier_s3_kernel(yr_ref, yi_ref, tc_ref, ts_ref, mh_ref, ml_ref, o_ref, *, nk):
    lanes = HALF // tc_ref.shape[-1]
    for j in range(nk):
        tc = jnp.concatenate([tc_ref[j]] * lanes, axis=-1)
        ts = jnp.concatenate([ts_ref[j]] * lanes, axis=-1)
        yr = yr_ref[j * FT_N1:(j + 1) * FT_N1, :]
        yi = yi_ref[j * FT_N1:(j + 1) * FT_N1, :]
        z = jnp.concatenate([yr * tc + yi * ts, yi * tc - yr * ts], axis=0)
        z_hi, z_lo = _split(z)
        o_ref[:, j * HALF:(j + 1) * HALF] = _dot3(mh_ref[...], ml_ref[...], z_hi, z_lo).astype(BF16)


def _fourier_smp(ar, ai):
    n1 = FT_N1
    wide = n1 * HALF
    c, s = _dft_cos_sin(n1)
    m1_hi, m1_lo = _hi_lo(np.block([[c, s], [-s, c]]))
    m3_hi, m3_lo = _hi_lo(np.concatenate([c, s], axis=1) / n1)
    kk = np.arange(n1)
    ang = 2.0 * np.pi * (kk[:, None] * kk[None, :]) / DEC_SEQ
    tw_c = jnp.broadcast_to(jnp.asarray(np.cos(ang), F32)[:, :, None], (n1, n1, 128))
    tw_s = jnp.broadcast_to(jnp.asarray(np.sin(ang), F32)[:, :, None], (n1, n1, 128))

    ncol = 128
    ctx_tiles = T_CTX // DEC_SEQ
    a_blk = pl.BlockSpec((DEC_SEQ, ncol), lambda b, j: (ctx_tiles + b, j))
    y_blk = pl.BlockSpec((DEC_SEQ, ncol), lambda b, j: (b, j))
    m1_blk = pl.BlockSpec((2 * n1, 2 * n1), lambda b, j: (0, 0))
    y_sds = jax.ShapeDtypeStruct((T_SMP, HALF), F32)
    yr, yi = pl.pallas_call(
        _fourier_s1_kernel,
        grid=(DEC_BATCH, HALF // ncol),
        in_specs=[a_blk, a_blk, m1_blk, m1_blk],
        out_specs=[y_blk, y_blk],
        out_shape=[y_sds, y_sds],
        compiler_params=_cparams("parallel", "parallel"),
        name="fourier_latent_stage1",
    )(ar, ai, m1_hi, m1_lo)

    nk = 8
    z_blk = pl.BlockSpec((nk * n1, HALF), lambda b, k: (b * (n1 // nk) + k, 0))
    tw_blk = pl.BlockSpec((nk, n1, 128), lambda b, k: (k, 0, 0))
    m3_blk = pl.BlockSpec((n1, 2 * n1), lambda b, k: (0, 0))
    out = pl.pallas_call(
        functools.partial(_fourier_s3_kernel, nk=nk),
        grid=(DEC_BATCH, n1 // nk),
        in_specs=[z_blk, z_blk, tw_blk, tw_blk, m3_blk, m3_blk],
        out_specs=pl.BlockSpec((n1, nk * HALF), lambda b, k: (b, k)),
        out_shape=jax.ShapeDtypeStruct((DEC_BATCH * n1, wide), BF16),
        compiler_params=_cparams("parallel", "parallel"),
        name="fourier_latent_stage3",
    )(yr, yi, tw_c, tw_s, m3_hi, m3_lo)
    return out.reshape(T_SMP, HALF)


def _retention_kernel(lg_ref, q_ref, k_ref, v_ref, g_ref, *rest, seq, chunk, nh, has_s0):
    if has_s0:
        s0_ref, y_ref, of_ref, ob_ref, dec_ref, xi_ref, zeta_ref = rest
        st_ref = None
    else:
        y_ref, st_ref, of_ref, ob_ref, dec_ref, xi_ref, zeta_ref = rest
        s0_ref = None
    L = chunk
    nc = seq // L
    hb = pl.program_id(1)
    diff = (lax.broadcasted_iota(jnp.int32, (L, L), 0) - lax.broadcasted_iota(jnp.int32, (L, L), 1)).astype(F32)
    li = lax.broadcasted_iota(jnp.int32, (L, HEAD_DIM), 0).astype(F32)
    one = jnp.ones((1, 1), F32)

    gcs = []
    for hh in range(nh):
        lgf = lg_ref[0, hb * nh + hh]
        lgb = lg_ref[1, hb * nh + hh]
        dec_ref[2 * hh] = jnp.where(diff >= 0, jnp.exp(lgf * jnp.maximum(diff, 0.0)), 0.0)
        dec_ref[2 * hh + 1] = jnp.where(diff <= 0, jnp.exp(lgb * jnp.maximum(-diff, 0.0)), 0.0)
        xi_ref[2 * hh] = jnp.exp(lgf * (li + 1.0))
        xi_ref[2 * hh + 1] = jnp.exp(lgb * (L - li))
        zeta_ref[2 * hh] = jnp.exp(lgf * (L - 1.0 - li))
        zeta_ref[2 * hh + 1] = jnp.exp(lgb * li)
        gcs.append(jnp.exp(one * (lgf * L)))
        gcs.append(jnp.exp(one * (lgb * L)))

    def chunk(c, s, t, sl, o_ref):
        r0 = pl.multiple_of(c * L, L)
        qc = q_ref[pl.ds(r0, L), sl]
        kc = k_ref[pl.ds(r0, L), sl]
        vc = v_ref[pl.ds(r0, L), sl]
        inner = _dot_nt(qc, kc.astype(BF16)) * dec_ref[t]
        o_ref[pl.ds(r0, L), sl] = _dot(inner.astype(BF16), vc) + _dot(qc, s.astype(BF16)) * xi_ref[t]
        return s * gcs[t] + _dot((kc * zeta_ref[t]).T.astype(BF16), vc)

    def scan_step(i, states):
        out = []
        for hh in range(nh):
            sl = slice(hh * HEAD_DIM, (hh + 1) * HEAD_DIM)
            out.append(chunk(i, states[2 * hh], 2 * hh, sl, of_ref))
            out.append(chunk(nc - 1 - i, states[2 * hh + 1], 2 * hh + 1, sl, ob_ref))
        return tuple(out)

    if has_s0:
        init = tuple(s0_ref[t % 2, t // 2] for t in range(2 * nh))
    else:
        init = tuple(jnp.zeros((HEAD_DIM, HEAD_DIM), F32) for _ in range(2 * nh))
    final = lax.fori_loop(0, nc, scan_step, init)
    if st_ref is not None:
        for t in range(2 * nh):
            st_ref[t % 2, t // 2] = final[t]

    def finish(c, carry):
        r0 = pl.multiple_of(c * L, L)
        o_all = of_ref[pl.ds(r0, L), :] + ob_ref[pl.ds(r0, L), :]
        gate = _silu(g_ref[pl.ds(r0, L), :])
        for hh in range(nh):
            sl = slice(hh * HEAD_DIM, (hh + 1) * HEAD_DIM)
            o = o_all[:, sl]
            mu = jnp.mean(o, axis=-1, keepdims=True)
            var = jnp.mean(jnp.square(o - mu), axis=-1, keepdims=True)
            y_ref[pl.ds(r0, L), sl] = (gate[:, sl] * ((o - mu) * lax.rsqrt(var + EPS))).astype(BF16)
        return carry

    lax.fori_loop(0, nc, finish, 0)


def _retention(lg, q, k, v, g, s0, *, seq, chunk, nbatch, row0, nh):
    has_s0 = s0 is not None
    tile0 = row0 // seq
    width = nh * HEAD_DIM
    blk = pl.BlockSpec((seq, width), lambda b, hb: (tile0 + b, hb))
    st_blk = pl.BlockSpec((None, 2, nh, HEAD_DIM, HEAD_DIM), lambda b, hb: (b, 0, hb, 0, 0))
    in_specs = [pl.BlockSpec(memory_space=pltpu.SMEM), blk, blk, blk, blk]
    args = [lg, q, k, v, g]
    y_spec = pl.BlockSpec((seq, width), lambda b, hb: (b, hb))
    y_sds = jax.ShapeDtypeStruct((nbatch * seq, HALF), BF16)
    if has_s0:
        in_specs.append(st_blk)
        args.append(s0)
        out_specs, out_shape = y_spec, y_sds
    else:
        out_specs = [y_spec, st_blk]
        out_shape = [y_sds, jax.ShapeDtypeStruct((nbatch, 2, N_HEADS, HEAD_DIM, HEAD_DIM), F32)]
    return pl.pallas_call(
        functools.partial(_retention_kernel, seq=seq, chunk=chunk, nh=nh, has_s0=has_s0),
        grid=(nbatch, N_HEADS // nh),
        in_specs=in_specs,
        out_specs=out_specs,
        out_shape=out_shape,
        scratch_shapes=[
            pltpu.VMEM((seq, width), F32),
            pltpu.VMEM((seq, width), F32),
            pltpu.VMEM((2 * nh, chunk, chunk), F32),
            pltpu.VMEM((2 * nh, chunk, HEAD_DIM), F32),
            pltpu.VMEM((2 * nh, chunk, HEAD_DIM), F32),
        ],
        compiler_params=_cparams("parallel", "parallel"),
        name="retention_%d" % seq,
    )(*args)


def _outproj_o_kernel(ycc_ref, ycs_ref, ydc_ref, yds_ref, x_ref, mod_ref, w_ref, g_ref, rh_ref, rl_ref,
                      x3_ref, h_ref, e1_ref, e2_ref, ga_ref, gb_ref, *, tm):
    is_ctx = pl.program_id(0) < T_CTX // tm
    y_c = jnp.where(is_ctx, ycc_ref[...], ycs_ref[...])
    y_d = jnp.where(is_ctx, ydc_ref[...], yds_ref[...])
    y = _dot(y_c, w_ref[0:HALF, :]) + _dot(y_d, w_ref[HALF:D_MODEL, :])
    m = mod_ref[...]
    x3 = x_ref[...] + m[:, 2 * D_MODEL:3 * D_MODEL] * y
    x3_ref[...] = x3
    h = _norm_mod(x3, g_ref[...], m[:, 4 * D_MODEL:5 * D_MODEL], m[:, 3 * D_MODEL:4 * D_MODEL])
    h_ref[...] = h
    h_hi, h_lo = _split(h)
    logits = _dot3(h_hi, h_lo, rh_ref[...], rl_ref[...])
    idx = lax.broadcasted_iota(jnp.int32, logits.shape, 1).astype(F32)
    logits = jnp.where(idx < float(N_EXPERTS), logits, -jnp.inf)
    m1 = logits.max(axis=-1, keepdims=True)
    e1 = jnp.where(logits == m1, idx, float(N_EXPERTS)).min(axis=-1, keepdims=True)
    rest = jnp.where(idx == e1, -jnp.inf, logits)
    m2 = rest.max(axis=-1, keepdims=True)
    e2 = jnp.where(rest == m2, idx, float(N_EXPERTS)).min(axis=-1, keepdims=True)
    ex = jnp.exp(m2 - m1)
    den = 1.0 + ex
    wide = (tm, 128)
    e1_ref[...] = jnp.broadcast_to(e1, wide).astype(jnp.int32)
    e2_ref[...] = jnp.broadcast_to(e2, wide).astype(jnp.int32)
    ga_ref[...] = jnp.broadcast_to(1.0 / den, wide)
    gb_ref[...] = jnp.broadcast_to(ex / den, wide)


def _out_proj_o(yc_ctx, yc_smp, yd_ctx, yd_smp, x, mod, w, g, r_hi, r_lo):
    tm = 256
    n_ctx = T_CTX // tm
    cidx = _cond_index(tm)
    row = lambda i: (i, 0)
    ctx_blk = pl.BlockSpec((tm, HALF), lambda i: (jnp.minimum(i, n_ctx - 1), 0))
    smp_blk = pl.BlockSpec((tm, HALF), lambda i: (jnp.maximum(i - n_ctx, 0), 0))
    sds = jax.ShapeDtypeStruct
    rep = pl.BlockSpec((tm, 128), row)
    return pl.pallas_call(
        functools.partial(_outproj_o_kernel, tm=tm),
        grid=(T_ALL // tm,),
        in_specs=[
            ctx_blk, smp_blk, ctx_blk, smp_blk,
            pl.BlockSpec((tm, D_MODEL), row),
            pl.BlockSpec((None, 1, N_MOD), lambda i: (cidx(i), 0, 0)),
            pl.BlockSpec((D_MODEL, D_MODEL), lambda i: (0, 0)),
            pl.BlockSpec((1, D_MODEL), lambda i: (0, 0)),
            pl.BlockSpec((D_MODEL, 128), lambda i: (0, 0)),
            pl.BlockSpec((D_MODEL, 128), lambda i: (0, 0)),
        ],
        out_specs=[pl.BlockSpec((tm, D_MODEL), row), pl.BlockSpec((tm, D_MODEL), row), rep, rep, rep, rep],
        out_shape=[
            sds((T_ALL, D_MODEL), F32), sds((T_ALL, D_MODEL), F32),
            sds((T_ALL, 128), jnp.int32), sds((T_ALL, 128), jnp.int32),
            sds((T_ALL, 128), F32), sds((T_ALL, 128), F32),
        ],
        compiler_params=_cparams("parallel"),
        name="out_proj_odd_route",
    )(yc_ctx, yc_smp, yd_ctx, yd_smp, x, mod, w, g, r_hi, r_lo)


def _routing_tables(e1, e2):
    flat_e = jnp.concatenate([e1, e2])
    onehot = (flat_e[:, None] == jnp.arange(N_EXPERTS, dtype=jnp.int32)[None, :]).astype(jnp.int32)
    csum = jnp.cumsum(onehot, axis=0)
    counts = csum[-1]
    padded = (counts + MOE_TM - 1) // MOE_TM * MOE_TM
    ends = jnp.cumsum(padded)
    pos = jnp.sum(onehot * (csum - 1 + (ends - padded)[None, :]), axis=1)
    tok = jnp.tile(jnp.arange(T_ALL, dtype=jnp.int32), 2)
    slot_t = jnp.zeros((MOE_P,), jnp.int32).at[pos].set(tok)
    block_row0 = jnp.arange(MOE_NB, dtype=jnp.int32) * MOE_TM
    block_e = jnp.minimum(
        jnp.sum((ends[None, :] <= block_row0[:, None]).astype(jnp.int32), axis=1), N_EXPERTS - 1
    ).astype(jnp.int32)
    n_valid = (ends[-1] // MOE_TM).astype(jnp.int32).reshape(1)
    return slot_t, pos[:T_ALL].astype(jnp.int32), pos[T_ALL:].astype(jnp.int32), block_e, n_valid


def _row_copy(src_ref, dst_ref, sem, src_row, dst_row):
    return pltpu.make_async_copy(src_ref.at[pl.ds(src_row, 1), :], dst_ref.at[pl.ds(dst_row, 1), :], sem)


def _start_rows(src_ref, dst_ref, sem, idx_ref, tm):
    def start(r, c):
        _row_copy(src_ref, dst_ref, sem, idx_ref[0, r], r).start()
        return c

    lax.fori_loop(0, tm, start, 0, unroll=8)


def _wait_rows(src_ref, dst_ref, sem, tm):
    pltpu.make_async_copy(src_ref.at[pl.ds(0, tm), :], dst_ref, sem).wait()


def _gather_kernel(idx_ref, nxt_ref, src_ref, o_ref, buf_ref, sems, *, tm):
    i = pl.program_id(0)
    slot = i % 2

    @pl.when(i == 0)
    def _():
        _start_rows(src_ref, buf_ref.at[0], sems.at[0], idx_ref, tm)

    @pl.when(i + 1 < pl.num_programs(0))
    def _():
        _start_rows(src_ref, buf_ref.at[1 - slot], sems.at[1 - slot], nxt_ref, tm)

    _wait_rows(src_ref, buf_ref.at[slot], sems.at[slot], tm)
    o_ref[...] = buf_ref[slot]


def _next_idx_spec(tm, nblk):
    return pl.BlockSpec((None, 1, tm), lambda i: (jnp.minimum(i + 1, nblk - 1), 0, 0), memory_space=pltpu.SMEM)


def _gather_rows(src, slot_t):
    tm = GATHER_TM
    nblk = MOE_P // tm
    idx = slot_t.reshape(nblk, 1, tm)
    return pl.pallas_call(
        functools.partial(_gather_kernel, tm=tm),
        grid=(nblk,),
        in_specs=[
            pl.BlockSpec((None, 1, tm), lambda i: (i, 0, 0), memory_space=pltpu.SMEM),
            _next_idx_spec(tm, nblk),
            pl.BlockSpec(memory_space=pl.ANY),
        ],
        out_specs=pl.BlockSpec((tm, D_MODEL), lambda i: (i, 0)),
        out_shape=jax.ShapeDtypeStruct((MOE_P, D_MODEL), F32),
        scratch_shapes=[pltpu.VMEM((2, tm, D_MODEL), F32), pltpu.SemaphoreType.DMA((2,))],
        compiler_params=_cparams("arbitrary"),
        name="moe_gather",
    )(idx, idx, src)


def _experts_kernel(be_ref, nv_ref, x_ref, wg_ref, wu_ref, wd_ref, o_ref, xb_ref, acc_ref):
    i = pl.program_id(0)
    f = pl.program_id(1)
    valid = i < nv_ref[0]

    @pl.when(jnp.logical_and(valid, f == 0))
    def _():
        xb_ref[...] = x_ref[...].astype(BF16)
        acc_ref[...] = jnp.zeros_like(acc_ref)

    @pl.when(valid)
    def _():
        x = xb_ref[...]
        a = _silu(_dot(x, wg_ref[...])) * _dot(x, wu_ref[...])
        acc_ref[...] += _dot(a.astype(BF16), wd_ref[...])

    last = f == pl.num_programs(1) - 1

    @pl.when(jnp.logical_and(valid, last))
    def _():
        o_ref[...] = acc_ref[...]

    @pl.when(jnp.logical_and(jnp.logical_not(valid), last))
    def _():
        o_ref[...] = jnp.zeros_like(o_ref)


def _experts(xs, block_e, n_valid, wg, wu, wd):
    nf = D_FF_EXPERT // MOE_TF

    def f_eff(i, f, nv):
        return jnp.where(i < nv[0], f, nf - 1)

    return pl.pallas_call(
        _experts_kernel,
        grid_spec=pltpu.PrefetchScalarGridSpec(
            num_scalar_prefetch=2,
            grid=(MOE_NB, nf),
            in_specs=[
                pl.BlockSpec((MOE_TM, D_MODEL), lambda i, f, be, nv: (i, 0)),
                pl.BlockSpec((None, D_MODEL, MOE_TF), lambda i, f, be, nv: (be[i], 0, f_eff(i, f, nv))),
                pl.BlockSpec((None, D_MODEL, MOE_TF), lambda i, f, be, nv: (be[i], 0, f_eff(i, f, nv))),
                pl.BlockSpec((None, MOE_TF, D_MODEL), lambda i, f, be, nv: (be[i], f_eff(i, f, nv), 0)),
            ],
            out_specs=pl.BlockSpec((MOE_TM, D_MODEL), lambda i, f, be, nv: (i, 0)),
            scratch_shapes=[pltpu.VMEM((MOE_TM, D_MODEL), BF16), pltpu.VMEM((MOE_TM, D_MODEL), F32)],
        ),
        out_shape=jax.ShapeDtypeStruct((MOE_P, D_MODEL), F32),
        compiler_params=_cparams("arbitrary", "arbitrary"),
        name="moe_experts",
    )(block_e, n_valid, xs, wg, wu, wd)


def _combine_kernel(p1_ref, p2_ref, n1_ref, n2_ref, ys_ref, x_ref, ga_ref, gb_ref, mod_ref, g_ref, oc_ref, os_ref,
                    a_ref, b_ref, sems, *, tm):
    i = pl.program_id(0)
    slot = i % 2

    def start_tile(s, i1_ref, i2_ref):
        _start_rows(ys_ref, a_ref.at[s], sems.at[0, s], i1_ref, tm)
        _start_rows(ys_ref, b_ref.at[s], sems.at[1, s], i2_ref, tm)

    @pl.when(i == 0)
    def _():
        start_tile(0, p1_ref, p2_ref)

    @pl.when(i + 1 < pl.num_programs(0))
    def _():
        start_tile(1 - slot, n1_ref, n2_ref)

    _wait_rows(ys_ref, a_ref.at[slot], sems.at[0, slot], tm)
    _wait_rows(ys_ref, b_ref.at[slot], sems.at[1, slot], tm)
    reps = D_MODEL // 128
    ga = jnp.concatenate([ga_ref[...]] * reps, axis=-1)
    gb = jnp.concatenate([gb_ref[...]] * reps, axis=-1)
    f = a_ref[slot] * ga + b_ref[slot] * gb
    x = x_ref[...] + mod_ref[:, 5 * D_MODEL:6 * D_MODEL] * f
    y = (x * lax.rsqrt(jnp.mean(x * x, axis=-1, keepdims=True) + EPS)) * g_ref[...]
    is_ctx = pl.program_id(0) < T_CTX // tm

    @pl.when(is_ctx)
    def _():
        oc_ref[...] = y

    @pl.when(jnp.logical_not(is_ctx))
    def _():
        os_ref[...] = y


def _combine(ys, pos1, pos2, x, ga, gb, mod, final_g):
    tm = GATHER_TM
    nblk = T_ALL // tm
    cidx = _cond_index(tm)
    row = lambda i: (i, 0)
    idx_blk = pl.BlockSpec((None, 1, tm), lambda i: (i, 0, 0), memory_space=pltpu.SMEM)
    idx1 = pos1.reshape(nblk, 1, tm)
    idx2 = pos2.reshape(nblk, 1, tm)
    return pl.pallas_call(
        functools.partial(_combine_kernel, tm=tm),
        grid=(nblk,),
        in_specs=[
            idx_blk, idx_blk, _next_idx_spec(tm, nblk), _next_idx_spec(tm, nblk),
            pl.BlockSpec(memory_space=pl.ANY),
            pl.BlockSpec((tm, D_MODEL), row),
            pl.BlockSpec((tm, 128), row),
            pl.BlockSpec((tm, 128), row),
            pl.BlockSpec((None, 1, N_MOD), lambda i: (cidx(i), 0, 0)),
            pl.BlockSpec((1, D_MODEL), lambda i: (0, 0)),
        ],
        out_specs=list(_ctx_smp_specs(tm, D_MODEL)),
        out_shape=[jax.ShapeDtypeStruct((T_CTX, D_MODEL), F32), jax.ShapeDtypeStruct((T_SMP, D_MODEL), F32)],
        scratch_shapes=[
            pltpu.VMEM((2, tm, D_MODEL), F32),
            pltpu.VMEM((2, tm, D_MODEL), F32),
            pltpu.SemaphoreType.DMA((2, 2)),
        ],
        compiler_params=_cparams("arbitrary"),
        name="moe_combine_final_norm",
    )(idx1, idx2, idx1, idx2, ys, x, ga, gb, mod, final_g)


def kernel(x_prompt, x_sample, cache_na_k, cache_na_v, state_ret, c, c_ctx, w_ada, b_ada, norm_g, final_g, w_in_e, w_conv_e, na_rpb_e, w_out_e, ff_gate_e, ff_up_e, ff_down_e, w_in_o, ret_decay_o, w_out_o, router_o, ex_gate_o, ex_up_o, ex_down_o):
    x_ctx = x_prompt.reshape(T_CTX, D_MODEL)
    x_smp = x_sample.reshape(T_SMP, D_MODEL)
    cond = jnp.concatenate([c_ctx[None, :], c, jnp.zeros((N_COND - 1 - DEC_BATCH, D_MODEL), F32)], axis=0)
    mod = _ada(cond, w_ada, b_ada)
    mod0 = mod[0].reshape(N_COND, 1, N_MOD)
    mod1 = mod[1].reshape(N_COND, 1, N_MOD)

    bcu, q, k, v, kb, vb = _in_proj_e(x_ctx, x_smp, mod0, norm_g[0, 0][None, :], w_in_e[0].astype(BF16))
    o_ctx = _attn_ctx(q, kb, vb)
    o_smp = _attn_na(q, kb, vb,
                     cache_na_k[:, 0].reshape(DEC_BATCH, PAST_LEN, HALF),
                     cache_na_v[:, 0].reshape(DEC_BATCH, PAST_LEN, HALF),
                     _na_bias_table(na_rpb_e[0]))
    x1 = _out_proj_e(bcu, o_ctx, o_smp, x_ctx, x_smp, mod0, w_conv_e[0].T, w_out_e[0].astype(BF16))
    x2 = _ffn(x1, mod0, norm_g[0, 1][None, :], ff_gate_e[0].astype(BF16), ff_up_e[0].astype(BF16),
              ff_down_e[0].astype(BF16))

    mc_hi, mc_lo = _hi_lo(_channel_dft_table())
    ar, ai, q1, k1, v1, g1 = _in_proj_o(x2, mod1, norm_g[1, 0][None, :], w_in_o[0].astype(BF16), mc_hi, mc_lo)
    yc_ctx = _fourier_ctx(ar, ai)
    yc_smp = _fourier_smp(ar, ai)
    lg = jax.nn.log_sigmoid(ret_decay_o[0].astype(F32))
    yd_ctx, new_state = _retention(lg, q1, k1, v1, g1, None, seq=SEQ, chunk=RET_CHUNK_CTX, nbatch=BATCH, row0=0,
                                   nh=8)
    yd_smp = _retention(lg, q1, k1, v1, g1, state_ret[:, 0], seq=DEC_SEQ, chunk=RET_CHUNK_SMP, nbatch=DEC_BATCH,
                        row0=T_CTX, nh=4)
    r_hi, r_lo = _hi_lo(jnp.pad(router_o[0], ((0, 0), (0, 128 - N_EXPERTS))))
    x3, hm, e1, e2, ga, gb = _out_proj_o(yc_ctx, yc_smp, yd_ctx, yd_smp, x2, mod1, w_out_o[0].astype(BF16),
                                         norm_g[1, 1][None, :], r_hi, r_lo)
    slot_t, pos1, pos2, block_e, n_valid = _routing_tables(e1[:, 0], e2[:, 0])
    xs = _gather_rows(hm, slot_t)
    ys = _experts(xs, block_e, n_valid, ex_gate_o[0].astype(BF16), ex_up_o[0].astype(BF16),
                  ex_down_o[0].astype(BF16))
    y_ctx, y_smp = _combine(ys, pos1, pos2, x3, ga, gb, mod1, final_g[None, :])

    y_prompt = y_ctx.reshape(BATCH, SEQ, D_MODEL)
    y_sample = y_smp.reshape(DEC_BATCH, DEC_SEQ, D_MODEL)
    new_na_k = k[:T_CTX].reshape(BATCH, 1, SEQ, N_HEADS, HEAD_DIM)
    new_na_v = v[:T_CTX].reshape(BATCH, 1, SEQ, N_HEADS, HEAD_DIM)
    new_state_ret = new_state.reshape(BATCH, 1, 2, N_HEADS, HEAD_DIM, HEAD_DIM)
    return (y_prompt, y_sample, new_na_k, new_na_v, new_state_ret)
```

```python
import functools

import numpy as np
import jax
import jax.numpy as jnp
from jax import lax
from jax.experimental import pallas as pl
from jax.experimental.pallas import tpu as pltpu

D_MODEL = 1024
BATCH = 32
SEQ = 256
DEC_BATCH = 4
DEC_SEQ = 4096
PAST_LEN = 256
GRID_W = 64
HEAD_DIM = 64
HALF = D_MODEL // 2
N_HEADS = HALF // HEAD_DIM
NA_WIN_R = 8
NA_WIN_C = 16
FT_GROUPS = 4
FT_GROUP_CH = HALF // FT_GROUPS
RET_CHUNK_CTX = 256
RET_CHUNK_SMP = 512
D_FF = 2816
N_EXPERTS = 8
D_FF_EXPERT = 3584
EPS = 1e-6
NEG_INF = -1e30
ATT_SCALE = HEAD_DIM ** -0.5

T_CTX = BATCH * SEQ
T_SMP = DEC_BATCH * DEC_SEQ
T_ALL = T_CTX + T_SMP
N_COND = 8
N_MOD = 6 * D_MODEL
GRID_H = DEC_SEQ // GRID_W
FT_N1 = 64

F32 = jnp.float32
BF16 = jnp.bfloat16
VMEM_LIMIT = 56 * 1024 * 1024

MOE_TM = 512
MOE_TF = 1792
MOE_A = 2 * T_ALL
MOE_NB = MOE_A // MOE_TM + N_EXPERTS
MOE_P = MOE_NB * MOE_TM
GATHER_TM = 256


def _cparams(*sem):
    return pltpu.CompilerParams(dimension_semantics=sem, vmem_limit_bytes=VMEM_LIMIT)


def _cond_index(tm):
    n_ctx = T_CTX // tm
    per_b = DEC_SEQ // tm

    def f(i):
        return jnp.where(i < n_ctx, 0, 1 + (i - n_ctx) // per_b)

    return f


def _ctx_smp_specs(tm, width):
    n_ctx = T_CTX // tm
    return (pl.BlockSpec((tm, width), lambda i: (jnp.minimum(i, n_ctx - 1), 0)),
            pl.BlockSpec((tm, width), lambda i: (jnp.maximum(i - n_ctx, 0), 0)))


def _pick_ctx_smp(tm, c_ref, s_ref):
    return jnp.where(pl.program_id(0) < T_CTX // tm, c_ref[...], s_ref[...])


def _silu(x):
    return x * (1.0 / (1.0 + jnp.exp(-x)))


def _norm_mod(x, g, scale, shift):
    y = x * lax.rsqrt(jnp.mean(x * x, axis=-1, keepdims=True) + EPS)
    return (y * g) * (1.0 + scale) + shift


def _split(a):
    hi = a.astype(BF16)
    lo = (a - hi.astype(F32)).astype(BF16)
    return hi, lo


def _dot(a, b):
    return jnp.dot(a, b, preferred_element_type=F32)


def _dot_nt(a, b):
    return lax.dot_general(a, b, (((1,), (1,)), ((), ())), preferred_element_type=F32)


def _dot3(a_hi, a_lo, b_hi, b_lo):
    return _dot(a_hi, b_hi) + (_dot(a_hi, b_lo) + _dot(a_lo, b_hi))


def _ada_kernel(c_ref, w_ref, b_ref, o_ref):
    s = _silu(c_ref[...]).astype(BF16)
    o_ref[...] = _dot(s, w_ref[...].astype(BF16)) + b_ref[...]


def _ada(cond, w_ada, b_ada):
    depth = w_ada.shape[0]
    tn = 1536
    return pl.pallas_call(
        _ada_kernel,
        grid=(depth, N_MOD // tn),
        in_specs=[
            pl.BlockSpec((N_COND, D_MODEL), lambda l, j: (0, 0)),
            pl.BlockSpec((None, D_MODEL, tn), lambda l, j: (l, 0, j)),
            pl.BlockSpec((None, 1, tn), lambda l, j: (l, 0, j)),
        ],
        out_specs=pl.BlockSpec((None, N_COND, tn), lambda l, j: (l, 0, j)),
        out_shape=jax.ShapeDtypeStruct((depth, N_COND, N_MOD), F32),
        compiler_params=_cparams("parallel", "parallel"),
        name="ada_mod",
    )(cond, w_ada, b_ada.reshape(depth, 1, N_MOD))


def _inproj_e_kernel(xc_ref, xs_ref, mod_ref, g_ref, w_ref, bcu_ref, q_ref, k_ref, v_ref, kb_ref, vb_ref, *, tm):
    m = mod_ref[...]
    x = _pick_ctx_smp(tm, xc_ref, xs_ref)
    h = _norm_mod(x, g_ref[...], m[:, D_MODEL:2 * D_MODEL], m[:, 0:D_MODEL]).astype(BF16)
    bcu_ref[...] = _dot(h, w_ref[:, 0:3 * HALF])
    q_ref[...] = _dot(h, w_ref[:, 3 * HALF:4 * HALF]).astype(BF16)
    k = _dot(h, w_ref[:, 4 * HALF:5 * HALF])
    kb_ref[...] = k.astype(BF16)
    v = _dot(h, w_ref[:, 5 * HALF:6 * HALF])
    vb_ref[...] = v.astype(BF16)

    @pl.when(pl.program_id(0) < T_CTX // tm)
    def _():
        k_ref[...] = k
        v_ref[...] = v


def _in_proj_e(x_ctx, x_smp, mod, g, w):
    tm = 512
    cidx = _cond_index(tm)
    row = lambda i: (i, 0)
    sds = jax.ShapeDtypeStruct
    return pl.pallas_call(
        functools.partial(_inproj_e_kernel, tm=tm),
        grid=(T_ALL // tm,),
        in_specs=[
            *_ctx_smp_specs(tm, D_MODEL),
            pl.BlockSpec((None, 1, N_MOD), lambda i: (cidx(i), 0, 0)),
            pl.BlockSpec((1, D_MODEL), lambda i: (0, 0)),
            pl.BlockSpec((D_MODEL, 6 * HALF), lambda i: (0, 0)),
        ],
        out_specs=[
            pl.BlockSpec((tm, 3 * HALF), row),
            pl.BlockSpec((tm, HALF), row),
            _ctx_smp_specs(tm, HALF)[0],
            _ctx_smp_specs(tm, HALF)[0],
            pl.BlockSpec((tm, HALF), row),
            pl.BlockSpec((tm, HALF), row),
        ],
        out_shape=[
            sds((T_ALL, 3 * HALF), F32),
            sds((T_ALL, HALF), BF16),
            sds((T_CTX, HALF), F32),
            sds((T_CTX, HALF), F32),
            sds((T_ALL, HALF), BF16),
            sds((T_ALL, HALF), BF16),
        ],
        compiler_params=_cparams("parallel"),
        name="in_proj_even",
    )(x_ctx, x_smp, mod, g, w)


def _softmax_parts(parts):
    m = parts[0].max(axis=-1, keepdims=True)
    for s in parts[1:]:
        m = jnp.maximum(m, s.max(axis=-1, keepdims=True))
    es = [jnp.exp(s - m) for s in parts]
    l = es[0].sum(axis=-1, keepdims=True)
    for e in es[1:]:
        l = l + e.sum(axis=-1, keepdims=True)
    inv = 1.0 / l
    return [e * inv for e in es]


def _attn_ctx_kernel(q_ref, k_ref, v_ref, o_ref):
    for h in range(N_HEADS):
        sl = slice(h * HEAD_DIM, (h + 1) * HEAD_DIM)
        s = _dot_nt(q_ref[:, sl], k_ref[:, sl]) * ATT_SCALE
        (p,) = _softmax_parts([s])
        o_ref[:, sl] = _dot(p.astype(BF16), v_ref[:, sl]).astype(BF16)


def _attn_ctx(q, kb, vb):
    blk = pl.BlockSpec((SEQ, HALF), lambda b: (b, 0))
    return pl.pallas_call(
        _attn_ctx_kernel,
        grid=(BATCH,),
        in_specs=[blk, blk, blk],
        out_specs=blk,
        out_shape=jax.ShapeDtypeStruct((T_CTX, HALF), BF16),
        compiler_params=_cparams("parallel"),
        name="attn_context",
    )(q, kb, vb)


NA_Q_ROWS = 4
NA_K_ROWS = NA_WIN_R + NA_Q_ROWS


def _na_key_row0(rb):
    return jnp.clip(rb * NA_Q_ROWS - NA_WIN_R // 2, 0, GRID_H - NA_K_ROWS)


def _attn_na_kernel(q_ref, k_ref, v_ref, kc_ref, vc_ref, bias_ref, o_ref):
    rb = pl.program_id(1)
    start = pl.multiple_of(_na_key_row0(rb) * GRID_W, GRID_W)
    n_loc = NA_K_ROWS * GRID_W
    for h in range(N_HEADS):
        sl = slice(h * HEAD_DIM, (h + 1) * HEAD_DIM)
        q = q_ref[:, sl]
        s_loc = _dot_nt(q, k_ref[pl.ds(start, n_loc), sl]) * ATT_SCALE + bias_ref[h]
        s_ctx = _dot_nt(q, kc_ref[:, sl].astype(BF16)) * ATT_SCALE
        p_loc, p_ctx = _softmax_parts([s_loc, s_ctx])
        o = _dot(p_loc.astype(BF16), v_ref[pl.ds(start, n_loc), sl])
        o = o + _dot(p_ctx.astype(BF16), vc_ref[:, sl].astype(BF16))
        o_ref[:, sl] = o.astype(BF16)


def _na_bias_table(rpb):
    cols = np.arange(GRID_W)
    col_start = np.clip(cols - NA_WIN_C // 2, 0, GRID_W - NA_WIN_C)
    col_in = (cols[None, :] >= col_start[:, None]) & (cols[None, :] < col_start[:, None] + NA_WIN_C)
    col_off = np.clip(cols[None, :] - cols[:, None], 1 - NA_WIN_C, NA_WIN_C - 1) + (NA_WIN_C - 1)
    n_off = 2 * NA_WIN_C - 1
    pick = jnp.asarray(col_off[None] == np.arange(n_off)[:, None, None], F32)
    by_col = (rpb.astype(F32)[:, :, :, None, None] * pick[None, None]).sum(axis=2)
    by_col = jnp.pad(by_col, ((0, 0), (NA_K_ROWS, NA_K_ROWS), (0, 0), (0, 0)))
    n_blocks = GRID_H // NA_Q_ROWS
    variants = []
    for rb in (0, 1, n_blocks - 1):
        k_row0 = int(np.clip(rb * NA_Q_ROWS - NA_WIN_R // 2, 0, GRID_H - NA_K_ROWS))
        k_row = k_row0 + np.arange(NA_K_ROWS)
        per_q_row = []
        for qr in range(NA_Q_ROWS):
            q_row = rb * NA_Q_ROWS + qr
            win0 = int(np.clip(q_row - NA_WIN_R // 2, 0, GRID_H - NA_WIN_R))
            row_in = (k_row >= win0) & (k_row < win0 + NA_WIN_R)
            ro0 = k_row0 - q_row + (NA_WIN_R - 1) + NA_K_ROWS
            b = by_col[:, ro0:ro0 + NA_K_ROWS]
            keep = row_in[:, None, None] & col_in[None]
            per_q_row.append(jnp.where(keep[None], b, NEG_INF))
        b = jnp.stack(per_q_row, axis=1)
        variants.append(b.transpose(0, 1, 3, 2, 4).reshape(N_HEADS, NA_Q_ROWS * GRID_W, NA_K_ROWS * GRID_W))
    return jnp.stack(variants)


def _attn_na(q, kb, vb, cache_k, cache_v, bias):
    tq = NA_Q_ROWS * GRID_W
    n_blocks = GRID_H // NA_Q_ROWS
    ctx_q_tiles = T_CTX // tq
    ctx_b_tiles = T_CTX // DEC_SEQ

    def bias_idx(b, rb):
        return (jnp.where(rb == 0, 0, jnp.where(rb == n_blocks - 1, 2, 1)), 0, 0, 0)

    full = pl.BlockSpec((DEC_SEQ, HALF), lambda b, r: (ctx_b_tiles + b, 0))
    cache = pl.BlockSpec((None, PAST_LEN, HALF), lambda b, r: (b, 0, 0))
    return pl.pallas_call(
        _attn_na_kernel,
        grid=(DEC_BATCH, n_blocks),
        in_specs=[
            pl.BlockSpec((tq, HALF), lambda b, r: (ctx_q_tiles + b * n_blocks + r, 0)),
            full, full, cache, cache,
            pl.BlockSpec((None, N_HEADS, tq, NA_K_ROWS * GRID_W), bias_idx),
        ],
        out_specs=pl.BlockSpec((tq, HALF), lambda b, r: (b * n_blocks + r, 0)),
        out_shape=jax.ShapeDtypeStruct((T_SMP, HALF), BF16),
        compiler_params=_cparams("parallel", "arbitrary"),
        name="attn_neighbourhood",
    )(q, kb, vb, cache_k, cache_v, bias)


def _seq_edges(i, tm):
    n_ctx = T_CTX // tm
    ctx_per = SEQ // tm
    smp_per = DEC_SEQ // tm
    j = i - n_ctx
    first = jnp.where(i < n_ctx, i % ctx_per == 0, j % smp_per == 0)
    last = jnp.where(i < n_ctx, i % ctx_per == ctx_per - 1, j % smp_per == smp_per - 1)
    return first, last


def _outproj_e_kernel(bcu_ref, prev_ref, next_ref, oc_ref, os_ref, xc_ref, xs_ref, mod_ref, wc_ref, w_ref, o_ref,
                      *, tm):
    i = pl.program_id(0)
    first, last = _seq_edges(i, tm)
    bcu = bcu_ref[...]
    cu = bcu[:, HALF:2 * HALF] * bcu[:, 2 * HALF:3 * HALF]
    pv = prev_ref[7:8, :]
    nx = next_ref[0:1, :]
    cu_prev_row = jnp.where(first, 0.0, pv[:, HALF:2 * HALF] * pv[:, 2 * HALF:3 * HALF])
    cu_next_row = jnp.where(last, 0.0, nx[:, HALF:2 * HALF] * nx[:, 2 * HALF:3 * HALF])
    rows = lax.broadcasted_iota(jnp.int32, (tm, HALF), 0)
    cu_prev = jnp.where(rows == 0, cu_prev_row, pltpu.roll(cu, 1, axis=0))
    cu_next = jnp.where(rows == tm - 1, cu_next_row, pltpu.roll(cu, tm - 1, axis=0))
    wc = wc_ref[...]
    y_a = bcu[:, 0:HALF] * (cu_prev * wc[0:1] + cu * wc[1:2] + cu_next * wc[2:3])
    y_b = _pick_ctx_smp(tm, oc_ref, os_ref)
    y = _dot(y_a.astype(BF16), w_ref[0:HALF, :]) + _dot(y_b, w_ref[HALF:D_MODEL, :])
    o_ref[...] = _pick_ctx_smp(tm, xc_ref, xs_ref) + mod_ref[:, 2 * D_MODEL:3 * D_MODEL] * y


def _out_proj_e(bcu, o_ctx, o_smp, x_ctx, x_smp, mod, w_conv_t, w):
    tm = 256
    cidx = _cond_index(tm)
    row = lambda i: (i, 0)
    nb8 = T_ALL // 8
    return pl.pallas_call(
        functools.partial(_outproj_e_kernel, tm=tm),
        grid=(T_ALL // tm,),
        in_specs=[
            pl.BlockSpec((tm, 3 * HALF), row),
            pl.BlockSpec((8, 3 * HALF), lambda i: (jnp.maximum(i * (tm // 8) - 1, 0), 0)),
            pl.BlockSpec((8, 3 * HALF), lambda i: (jnp.minimum((i + 1) * (tm // 8), nb8 - 1), 0)),
            *_ctx_smp_specs(tm, HALF),
            *_ctx_smp_specs(tm, D_MODEL),
            pl.BlockSpec((None, 1, N_MOD), lambda i: (cidx(i), 0, 0)),
            pl.BlockSpec((3, HALF), lambda i: (0, 0)),
            pl.BlockSpec((D_MODEL, D_MODEL), lambda i: (0, 0)),
        ],
        out_specs=pl.BlockSpec((tm, D_MODEL), row),
        out_shape=jax.ShapeDtypeStruct((T_ALL, D_MODEL), F32),
        compiler_params=_cparams("parallel"),
        name="out_proj_even",
    )(bcu, bcu, bcu, o_ctx, o_smp, x_ctx, x_smp, mod, w_conv_t, w)


def _ffn_kernel(x_ref, mod_ref, g_ref, wg_ref, wu_ref, wd_ref, o_ref, h_ref, acc_ref):
    f = pl.program_id(1)

    @pl.when(f == 0)
    def _():
        m = mod_ref[...]
        h = _norm_mod(x_ref[...], g_ref[...], m[:, 4 * D_MODEL:5 * D_MODEL], m[:, 3 * D_MODEL:4 * D_MODEL])
        h_ref[...] = h.astype(BF16)
        acc_ref[...] = jnp.zeros_like(acc_ref)

    h = h_ref[...]
    a = _silu(_dot(h, wg_ref[...])) * _dot(h, wu_ref[...])
    acc_ref[...] += _dot(a.astype(BF16), wd_ref[...])

    @pl.when(f == pl.num_programs(1) - 1)
    def _():
        o_ref[...] = x_ref[...] + mod_ref[:, 5 * D_MODEL:6 * D_MODEL] * acc_ref[...]


def _ffn(x, mod, g, wg, wu, wd):
    tm = 512
    tf = D_FF // 2
    cidx = _cond_index(tm)
    return pl.pallas_call(
        _ffn_kernel,
        grid=(T_ALL // tm, D_FF // tf),
        in_specs=[
            pl.BlockSpec((tm, D_MODEL), lambda i, f: (i, 0)),
            pl.BlockSpec((None, 1, N_MOD), lambda i, f: (cidx(i), 0, 0)),
            pl.BlockSpec((1, D_MODEL), lambda i, f: (0, 0)),
            pl.BlockSpec((D_MODEL, tf), lambda i, f: (0, f)),
            pl.BlockSpec((D_MODEL, tf), lambda i, f: (0, f)),
            pl.BlockSpec((tf, D_MODEL), lambda i, f: (f, 0)),
        ],
        out_specs=pl.BlockSpec((tm, D_MODEL), lambda i, f: (i, 0)),
        out_shape=jax.ShapeDtypeStruct((T_ALL, D_MODEL), F32),
        scratch_shapes=[pltpu.VMEM((tm, D_MODEL), BF16), pltpu.VMEM((tm, D_MODEL), F32)],
        compiler_params=_cparams("parallel", "arbitrary"),
        name="ffn_dense",
    )(x, mod, g, wg, wu, wd)


def _inproj_o_kernel(x_ref, mod_ref, g_ref, w_ref, mch_ref, mcl_ref, ar_ref, ai_ref, q_ref, k_ref, v_ref, gt_ref):
    m = mod_ref[...]
    h = _norm_mod(x_ref[...], g_ref[...], m[:, D_MODEL:2 * D_MODEL], m[:, 0:D_MODEL]).astype(BF16)
    u_hi, u_lo = _split(_dot(h, w_ref[:, 0:HALF]))
    for grp in range(FT_GROUPS):
        sl = slice(grp * FT_GROUP_CH, (grp + 1) * FT_GROUP_CH)
        a = _dot3(u_hi[:, sl], u_lo[:, sl], mch_ref[...], mcl_ref[...])
        ar_ref[:, sl] = a[:, 0:FT_GROUP_CH]
        ai_ref[:, sl] = a[:, FT_GROUP_CH:2 * FT_GROUP_CH]
    q_ref[...] = _dot(h, w_ref[:, HALF:2 * HALF]).astype(BF16)
    k_ref[...] = _dot(h, w_ref[:, 2 * HALF:3 * HALF]) * ATT_SCALE
    v_ref[...] = _dot(h, w_ref[:, 3 * HALF:4 * HALF]).astype(BF16)
    gt_ref[...] = _dot(h, w_ref[:, 4 * HALF:5 * HALF])


def _in_proj_o(x, mod, g, w, mc_hi, mc_lo):
    tm = 512
    cidx = _cond_index(tm)
    row = lambda i: (i, 0)
    half_out = pl.BlockSpec((tm, HALF), row)
    sds = jax.ShapeDtypeStruct
    return pl.pallas_call(
        _inproj_o_kernel,
        grid=(T_ALL // tm,),
        in_specs=[
            pl.BlockSpec((tm, D_MODEL), row),
            pl.BlockSpec((None, 1, N_MOD), lambda i: (cidx(i), 0, 0)),
            pl.BlockSpec((1, D_MODEL), lambda i: (0, 0)),
            pl.BlockSpec((D_MODEL, 5 * HALF), lambda i: (0, 0)),
            pl.BlockSpec((FT_GROUP_CH, 2 * FT_GROUP_CH), lambda i: (0, 0)),
            pl.BlockSpec((FT_GROUP_CH, 2 * FT_GROUP_CH), lambda i: (0, 0)),
        ],
        out_specs=[half_out] * 6,
        out_shape=[
            sds((T_ALL, HALF), F32),
            sds((T_ALL, HALF), F32),
            sds((T_ALL, HALF), BF16),
            sds((T_ALL, HALF), F32),
            sds((T_ALL, HALF), BF16),
            sds((T_ALL, HALF), F32),
        ],
        compiler_params=_cparams("parallel"),
        name="in_proj_odd",
    )(x, mod, g, w, mc_hi, mc_lo)


def _dft_cos_sin(n):
    k = np.arange(n)
    ang = 2.0 * np.pi * ((k[:, None] * k[None, :]) % n) / n
    return np.cos(ang), np.sin(ang)


def _hi_lo(a):
    a = jnp.asarray(a, F32)
    hi = a.astype(BF16)
    return hi, (a - hi.astype(F32)).astype(BF16)


def _channel_dft_table():
    c, s = _dft_cos_sin(FT_GROUP_CH)
    scale = FT_GROUP_CH ** -0.5
    return np.concatenate([c * scale, -s * scale], axis=1)


def _fourier_ctx_kernel(ar_ref, ai_ref, th_ref, tl_ref, o_ref):
    a_hi, a_lo = _split(jnp.concatenate([ar_ref[...], ai_ref[...]], axis=0))
    o_ref[...] = _dot3(th_ref[...], tl_ref[...], a_hi, a_lo).astype(BF16)


def _fourier_ctx(ar, ai):
    c, s = _dft_cos_sin(SEQ)
    t_hi, t_lo = _hi_lo(np.concatenate([c, s], axis=1) * SEQ ** -0.5)
    blk = pl.BlockSpec((SEQ, HALF), lambda b: (b, 0))
    tab = pl.BlockSpec((SEQ, 2 * SEQ), lambda b: (0, 0))
    return pl.pallas_call(
        _fourier_ctx_kernel,
        grid=(BATCH,),
        in_specs=[blk, blk, tab, tab],
        out_specs=blk,
        out_shape=jax.ShapeDtypeStruct((T_CTX, HALF), BF16),
        compiler_params=_cparams("parallel"),
        name="fourier_context",
    )(ar, ai, t_hi, t_lo)


def _fourier_s1_kernel(ar_ref, ai_ref, mh_ref, ml_ref, yr_ref, yi_ref):
    group = 8
    lanes = ar_ref.shape[-1]

    def p2_group(g, c):
        rows = [pl.ds(g * group + j, FT_N1, stride=FT_N1) for j in range(group)]
        a = jnp.concatenate([jnp.concatenate([ar_ref[r, :] for r in rows], axis=-1),
                             jnp.concatenate([ai_ref[r, :] for r in rows], axis=-1)], axis=0)
        a_hi, a_lo = _split(a)
        y = _dot3(mh_ref[...], ml_ref[...], a_hi, a_lo)
        for j, r in enumerate(rows):
            yr_ref[r, :] = y[0:FT_N1, j * lanes:(j + 1) * lanes]
            yi_ref[r, :] = y[FT_N1:2 * FT_N1, j * lanes:(j + 1) * lanes]
        return c

    lax.fori_loop(0, FT_N1 // group, p2_group, 0)


def _fourier_s3_kernel(yr_ref, yi_ref, tc_ref, ts_ref, mh_ref, ml_ref, o_ref, *, nk):
    lanes = HALF // tc_ref.shape[-1]
    for j in range(nk):
        tc = jnp.concatenate([tc_ref[j]] * lanes, axis=-1)
        ts = jnp.concatenate([ts_ref[j]] * lanes, axis=-1)
        yr = yr_ref[j * FT_N1:(j + 1) * FT_N1, :]
        yi = yi_ref[j * FT_N1:(j + 1) * FT_N1, :]
        z = jnp.concatenate([yr * tc + yi * ts, yi * tc - yr * ts], axis=0)
        z_hi, z_lo = _split(z)
        o_ref[:, j * HALF:(j + 1) * HALF] = _dot3(mh_ref[...], ml_ref[...], z_hi, z_lo).astype(BF16)


def _fourier_smp(ar, ai):
    n1 = FT_N1
    wide = n1 * HALF
    c, s = _dft_cos_sin(n1)
    m1_hi, m1_lo = _hi_lo(np.block([[c, s], [-s, c]]))
    m3_hi, m3_lo = _hi_lo(np.concatenate([c, s], axis=1) / n1)
    kk = np.arange(n1)
    ang = 2.0 * np.pi * (kk[:, None] * kk[None, :]) / DEC_SEQ
    tw_c = jnp.broadcast_to(jnp.asarray(np.cos(ang), F32)[:, :, None], (n1, n1, 128))
    tw_s = jnp.broadcast_to(jnp.asarray(np.sin(ang), F32)[:, :, None], (n1, n1, 128))

    ncol = 128
    ctx_tiles = T_CTX // DEC_SEQ
    a_blk = pl.BlockSpec((DEC_SEQ, ncol), lambda b, j: (ctx_tiles + b, j))
    y_blk = pl.BlockSpec((DEC_SEQ, ncol), lambda b, j: (b, j))
    m1_blk = pl.BlockSpec((2 * n1, 2 * n1), lambda b, j: (0, 0))
    y_sds = jax.ShapeDtypeStruct((T_SMP, HALF), F32)
    yr, yi = pl.pallas_call(
        _fourier_s1_kernel,
        grid=(DEC_BATCH, HALF // ncol),
        in_specs=[a_blk, a_blk, m1_blk, m1_blk],
        out_specs=[y_blk, y_blk],
        out_shape=[y_sds, y_sds],
        compiler_params=_cparams("parallel", "parallel"),
        name="fourier_latent_stage1",
    )(ar, ai, m1_hi, m1_lo)

    nk = 8
    z_blk = pl.BlockSpec((nk * n1, HALF), lambda b, k: (b * (n1 // nk) + k, 0))
    tw_blk = pl.BlockSpec((nk, n1, 128), lambda b, k: (k, 0, 0))
    m3_blk = pl.BlockSpec((n1, 2 * n1), lambda b, k: (0, 0))
    out = pl.pallas_call(
        functools.partial(_fourier_s3_kernel, nk=nk),
        grid=(DEC_BATCH, n1 // nk),
        in_specs=[z_blk, z_blk, tw_blk, tw_blk, m3_blk, m3_blk],
        out_specs=pl.BlockSpec((n1, nk * HALF), lambda b, k: (b, k)),
        out_shape=jax.ShapeDtypeStruct((DEC_BATCH * n1, wide), BF16),
        compiler_params=_cparams("parallel", "parallel"),
        name="fourier_latent_stage3",
    )(yr, yi, tw_c, tw_s, m3_hi, m3_lo)
    return out.reshape(T_SMP, HALF)


def _retention_kernel(lg_ref, q_ref, k_ref, v_ref, g_ref, *rest, seq, chunk, nh, has_s0):
    if has_s0:
        s0_ref, y_ref, of_ref, ob_ref, dec_ref, xi_ref, zeta_ref = rest
        st_ref = None
    else:
        y_ref, st_ref, of_ref, ob_ref, dec_ref, xi_ref, zeta_ref = rest
        s0_ref = None
    L = chunk
    nc = seq // L
    hb = pl.program_id(1)
    diff = (lax.broadcasted_iota(jnp.int32, (L, L), 0) - lax.broadcasted_iota(jnp.int32, (L, L), 1)).astype(F32)
    li = lax.broadcasted_iota(jnp.int32, (L, HEAD_DIM), 0).astype(F32)
    one = jnp.ones((1, 1), F32)

    gcs = []
    for hh in range(nh):
        lgf = lg_ref[0, hb * nh + hh]
        lgb = lg_ref[1, hb * nh + hh]
        dec_ref[2 * hh] = jnp.where(diff >= 0, jnp.exp(lgf * jnp.maximum(diff, 0.0)), 0.0)
        dec_ref[2 * hh + 1] = jnp.where(diff <= 0, jnp.exp(lgb * jnp.maximum(-diff, 0.0)), 0.0)
        xi_ref[2 * hh] = jnp.exp(lgf * (li + 1.0))
        xi_ref[2 * hh + 1] = jnp.exp(lgb * (L - li))
        zeta_ref[2 * hh] = jnp.exp(lgf * (L - 1.0 - li))
        zeta_ref[2 * hh + 1] = jnp.exp(lgb * li)
        gcs.append(jnp.exp(one * (lgf * L)))
        gcs.append(jnp.exp(one * (lgb * L)))

    def chunk(c, s, t, sl, o_ref):
        r0 = pl.multiple_of(c * L, L)
        qc = q_ref[pl.ds(r0, L), sl]
        kc = k_ref[pl.ds(r0, L), sl]
        vc = v_ref[pl.ds(r0, L), sl]
        inner = _dot_nt(qc, kc.astype(BF16)) * dec_ref[t]
        o_ref[pl.ds(r0, L), sl] = _dot(inner.astype(BF16), vc) + _dot(qc, s.astype(BF16)) * xi_ref[t]
        return s * gcs[t] + _dot((kc * zeta_ref[t]).T.astype(BF16), vc)

    def scan_step(i, states):
        out = []
        for hh in range(nh):
            sl = slice(hh * HEAD_DIM, (hh + 1) * HEAD_DIM)
            out.append(chunk(i, states[2 * hh], 2 * hh, sl, of_ref))
            out.append(chunk(nc - 1 - i, states[2 * hh + 1], 2 * hh + 1, sl, ob_ref))
        return tuple(out)

    if has_s0:
        init = tuple(s0_ref[t % 2, t // 2] for t in range(2 * nh))
    else:
        init = tuple(jnp.zeros((HEAD_DIM, HEAD_DIM), F32) for _ in range(2 * nh))
    final = lax.fori_loop(0, nc, scan_step, init)
    if st_ref is not None:
        for t in range(2 * nh):
            st_ref[t % 2, t // 2] = final[t]

    def finish(c, carry):
        r0 = pl.multiple_of(c * L, L)
        o_all = of_ref[pl.ds(r0, L), :] + ob_ref[pl.ds(r0, L), :]
        gate = _silu(g_ref[pl.ds(r0, L), :])
        for hh in range(nh):
            sl = slice(hh * HEAD_DIM, (hh + 1) * HEAD_DIM)
            o = o_all[:, sl]
            mu = jnp.mean(o, axis=-1, keepdims=True)
            var = jnp.mean(jnp.square(o - mu), axis=-1, keepdims=True)
            y_ref[pl.ds(r0, L), sl] = (gate[:, sl] * ((o - mu) * lax.rsqrt(var + EPS))).astype(BF16)
        return carry

    lax.fori_loop(0, nc, finish, 0)


def _retention(lg, q, k, v, g, s0, *, seq, chunk, nbatch, row0, nh):
    has_s0 = s0 is not None
    tile0 = row0 // seq
    width = nh * HEAD_DIM
    blk = pl.BlockSpec((seq, width), lambda b, hb: (tile0 + b, hb))
    st_blk = pl.BlockSpec((None, 2, nh, HEAD_DIM, HEAD_DIM), lambda b, hb: (b, 0, hb, 0, 0))
    in_specs = [pl.BlockSpec(memory_space=pltpu.SMEM), blk, blk, blk, blk]
    args = [lg, q, k, v, g]
    y_spec = pl.BlockSpec((seq, width), lambda b, hb: (b, hb))
    y_sds = jax.ShapeDtypeStruct((nbatch * seq, HALF), BF16)
    if has_s0:
        in_specs.append(st_blk)
        args.append(s0)
        out_specs, out_shape = y_spec, y_sds
    else:
        out_specs = [y_spec, st_blk]
        out_shape = [y_sds, jax.ShapeDtypeStruct((nbatch, 2, N_HEADS, HEAD_DIM, HEAD_DIM), F32)]
    return pl.pallas_call(
        functools.partial(_retention_kernel, seq=seq, chunk=chunk, nh=nh, has_s0=has_s0),
        grid=(nbatch, N_HEADS // nh),
        in_specs=in_specs,
        out_specs=out_specs,
        out_shape=out_shape,
        scratch_shapes=[
            pltpu.VMEM((seq, width), F32),
            pltpu.VMEM((seq, width), F32),
            pltpu.VMEM((2 * nh, chunk, chunk), F32),
            pltpu.VMEM((2 * nh, chunk, HEAD_DIM), F32),
            pltpu.VMEM((2 * nh, chunk, HEAD_DIM), F32),
        ],
        compiler_params=_cparams("parallel", "parallel"),
        name="retention_%d" % seq,
    )(*args)


def _outproj_o_kernel(ycc_ref, ycs_ref, ydc_ref, yds_ref, x_ref, mod_ref, w_ref, g_ref, rh_ref, rl_ref,
                      x3_ref, h_ref, e1_ref, e2_ref, ga_ref, gb_ref, *, tm):
    is_ctx = pl.program_id(0) < T_CTX // tm
    y_c = jnp.where(is_ctx, ycc_ref[...], ycs_ref[...])
    y_d = jnp.where(is_ctx, ydc_ref[...], yds_ref[...])
    y = _dot(y_c, w_ref[0:HALF, :]) + _dot(y_d, w_ref[HALF:D_MODEL, :])
    m = mod_ref[...]
    x3 = x_ref[...] + m[:, 2 * D_MODEL:3 * D_MODEL] * y
    x3_ref[...] = x3
    h = _norm_mod(x3, g_ref[...], m[:, 4 * D_MODEL:5 * D_MODEL], m[:, 3 * D_MODEL:4 * D_MODEL])
    h_ref[...] = h
    h_hi, h_lo = _split(h)
    logits = _dot3(h_hi, h_lo, rh_ref[...], rl_ref[...])
    idx = lax.broadcasted_iota(jnp.int32, logits.shape, 1).astype(F32)
    logits = jnp.where(idx < float(N_EXPERTS), logits, -jnp.inf)
    m1 = logits.max(axis=-1, keepdims=True)
    e1 = jnp.where(logits == m1, idx, float(N_EXPERTS)).min(axis=-1, keepdims=True)
    rest = jnp.where(idx == e1, -jnp.inf, logits)
    m2 = rest.max(axis=-1, keepdims=True)
    e2 = jnp.where(rest == m2, idx, float(N_EXPERTS)).min(axis=-1, keepdims=True)
    ex = jnp.exp(m2 - m1)
    den = 1.0 + ex
    wide = (tm, 128)
    e1_ref[...] = jnp.broadcast_to(e1, wide).astype(jnp.int32)
    e2_ref[...] = jnp.broadcast_to(e2, wide).astype(jnp.int32)
    ga_ref[...] = jnp.broadcast_to(1.0 / den, wide)
    gb_ref[...] = jnp.broadcast_to(ex / den, wide)


def _out_proj_o(yc_ctx, yc_smp, yd_ctx, yd_smp, x, mod, w, g, r_hi, r_lo):
    tm = 256
    n_ctx = T_CTX // tm
    cidx = _cond_index(tm)
    row = lambda i: (i, 0)
    ctx_blk = pl.BlockSpec((tm, HALF), lambda i: (jnp.minimum(i, n_ctx - 1), 0))
    smp_blk = pl.BlockSpec((tm, HALF), lambda i: (jnp.maximum(i - n_ctx, 0), 0))
    sds = jax.ShapeDtypeStruct
    rep = pl.BlockSpec((tm, 128), row)
    return pl.pallas_call(
        functools.partial(_outproj_o_kernel, tm=tm),
        grid=(T_ALL // tm,),
        in_specs=[
            ctx_blk, smp_blk, ctx_blk, smp_blk,
            pl.BlockSpec((tm, D_MODEL), row),
            pl.BlockSpec((None, 1, N_MOD), lambda i: (cidx(i), 0, 0)),
            pl.BlockSpec((D_MODEL, D_MODEL), lambda i: (0, 0)),
            pl.BlockSpec((1, D_MODEL), lambda i: (0, 0)),
            pl.BlockSpec((D_MODEL, 128), lambda i: (0, 0)),
            pl.BlockSpec((D_MODEL, 128), lambda i: (0, 0)),
        ],
        out_specs=[pl.BlockSpec((tm, D_MODEL), row), pl.BlockSpec((tm, D_MODEL), row), rep, rep, rep, rep],
        out_shape=[
            sds((T_ALL, D_MODEL), F32), sds((T_ALL, D_MODEL), F32),
            sds((T_ALL, 128), jnp.int32), sds((T_ALL, 128), jnp.int32),
            sds((T_ALL, 128), F32), sds((T_ALL, 128), F32),
        ],
        compiler_params=_cparams("parallel"),
        name="out_proj_odd_route",
    )(yc_ctx, yc_smp, yd_ctx, yd_smp, x, mod, w, g, r_hi, r_lo)


def _routing_tables(e1, e2):
    flat_e = jnp.concatenate([e1, e2])
    onehot = (flat_e[:, None] == jnp.arange(N_EXPERTS, dtype=jnp.int32)[None, :]).astype(jnp.int32)
    csum = jnp.cumsum(onehot, axis=0)
    counts = csum[-1]
    padded = (counts + MOE_TM - 1) // MOE_TM * MOE_TM
    ends = jnp.cumsum(padded)
    pos = jnp.sum(onehot * (csum - 1 + (ends - padded)[None, :]), axis=1)
    tok = jnp.tile(jnp.arange(T_ALL, dtype=jnp.int32), 2)
    slot_t = jnp.zeros((MOE_P,), jnp.int32).at[pos].set(tok)
    block_row0 = jnp.arange(MOE_NB, dtype=jnp.int32) * MOE_TM
    block_e = jnp.minimum(
        jnp.sum((ends[None, :] <= block_row0[:, None]).astype(jnp.int32), axis=1), N_EXPERTS - 1
    ).astype(jnp.int32)
    n_valid = (ends[-1] // MOE_TM).astype(jnp.int32).reshape(1)
    return slot_t, pos[:T_ALL].astype(jnp.int32), pos[T_ALL:].astype(jnp.int32), block_e, n_valid


def _row_copy(src_ref, dst_ref, sem, src_row, dst_row):
    return pltpu.make_async_copy(src_ref.at[pl.ds(src_row, 1), :], dst_ref.at[pl.ds(dst_row, 1), :], sem)


def _start_rows(src_ref, dst_ref, sem, idx_ref, tm):
    def start(r, c):
        _row_copy(src_ref, dst_ref, sem, idx_ref[0, r], r).start()
        return c

    lax.fori_loop(0, tm, start, 0, unroll=8)


def _wait_rows(src_ref, dst_ref, sem, tm):
    pltpu.make_async_copy(src_ref.at[pl.ds(0, tm), :], dst_ref, sem).wait()


def _gather_kernel(idx_ref, nxt_ref, src_ref, o_ref, buf_ref, sems, *, tm):
    i = pl.program_id(0)
    slot = i % 2

    @pl.when(i == 0)
    def _():
        _start_rows(src_ref, buf_ref.at[0], sems.at[0], idx_ref, tm)

    @pl.when(i + 1 < pl.num_programs(0))
    def _():
        _start_rows(src_ref, buf_ref.at[1 - slot], sems.at[1 - slot], nxt_ref, tm)

    _wait_rows(src_ref, buf_ref.at[slot], sems.at[slot], tm)
    o_ref[...] = buf_ref[slot]


def _next_idx_spec(tm, nblk):
    return pl.BlockSpec((None, 1, tm), lambda i: (jnp.minimum(i + 1, nblk - 1), 0, 0), memory_space=pltpu.SMEM)


def _gather_rows(src, slot_t):
    tm = GATHER_TM
    nblk = MOE_P // tm
    idx = slot_t.reshape(nblk, 1, tm)
    return pl.pallas_call(
        functools.partial(_gather_kernel, tm=tm),
        grid=(nblk,),
        in_specs=[
            pl.BlockSpec((None, 1, tm), lambda i: (i, 0, 0), memory_space=pltpu.SMEM),
            _next_idx_spec(tm, nblk),
            pl.BlockSpec(memory_space=pl.ANY),
        ],
        out_specs=pl.BlockSpec((tm, D_MODEL), lambda i: (i, 0)),
        out_shape=jax.ShapeDtypeStruct((MOE_P, D_MODEL), F32),
        scratch_shapes=[pltpu.VMEM((2, tm, D_MODEL), F32), pltpu.SemaphoreType.DMA((2,))],
        compiler_params=_cparams("arbitrary"),
        name="moe_gather",
    )(idx, idx, src)


def _experts_kernel(be_ref, nv_ref, x_ref, wg_ref, wu_ref, wd_ref, o_ref, xb_ref, acc_ref):
    i = pl.program_id(0)
    f = pl.program_id(1)
    valid = i < nv_ref[0]

    @pl.when(jnp.logical_and(valid, f == 0))
    def _():
        xb_ref[...] = x_ref[...].astype(BF16)
        acc_ref[...] = jnp.zeros_like(acc_ref)

    @pl.when(valid)
    def _():
        x = xb_ref[...]
        a = _silu(_dot(x, wg_ref[...])) * _dot(x, wu_ref[...])
        acc_ref[...] += _dot(a.astype(BF16), wd_ref[...])

    last = f == pl.num_programs(1) - 1

    @pl.when(jnp.logical_and(valid, last))
    def _():
        o_ref[...] = acc_ref[...]

    @pl.when(jnp.logical_and(jnp.logical_not(valid), last))
    def _():
        o_ref[...] = jnp.zeros_like(o_ref)


def _experts(xs, block_e, n_valid, wg, wu, wd):
    nf = D_FF_EXPERT // MOE_TF

    def f_eff(i, f, nv):
        return jnp.where(i < nv[0], f, nf - 1)

    return pl.pallas_call(
        _experts_kernel,
        grid_spec=pltpu.PrefetchScalarGridSpec(
            num_scalar_prefetch=2,
            grid=(MOE_NB, nf),
            in_specs=[
                pl.BlockSpec((MOE_TM, D_MODEL), lambda i, f, be, nv: (i, 0)),
                pl.BlockSpec((None, D_MODEL, MOE_TF), lambda i, f, be, nv: (be[i], 0, f_eff(i, f, nv))),
                pl.BlockSpec((None, D_MODEL, MOE_TF), lambda i, f, be, nv: (be[i], 0, f_eff(i, f, nv))),
                pl.BlockSpec((None, MOE_TF, D_MODEL), lambda i, f, be, nv: (be[i], f_eff(i, f, nv), 0)),
            ],
            out_specs=pl.BlockSpec((MOE_TM, D_MODEL), lambda i, f, be, nv: (i, 0)),
            scratch_shapes=[pltpu.VMEM((MOE_TM, D_MODEL), BF16), pltpu.VMEM((MOE_TM, D_MODEL), F32)],
        ),
        out_shape=jax.ShapeDtypeStruct((MOE_P, D_MODEL), F32),
        compiler_params=_cparams("arbitrary", "arbitrary"),
        name="moe_experts",
    )(block_e, n_valid, xs, wg, wu, wd)


def _combine_kernel(p1_ref, p2_ref, n1_ref, n2_ref, ys_ref, x_ref, ga_ref, gb_ref, mod_ref, g_ref, oc_ref, os_ref,
                    a_ref, b_ref, sems, *, tm):
    i = pl.program_id(0)
    slot = i % 2

    def start_tile(s, i1_ref, i2_ref):
        _start_rows(ys_ref, a_ref.at[s], sems.at[0, s], i1_ref, tm)
        _start_rows(ys_ref, b_ref.at[s], sems.at[1, s], i2_ref, tm)

    @pl.when(i == 0)
    def _():
        start_tile(0, p1_ref, p2_ref)

    @pl.when(i + 1 < pl.num_programs(0))
    def _():
        start_tile(1 - slot, n1_ref, n2_ref)

    _wait_rows(ys_ref, a_ref.at[slot], sems.at[0, slot], tm)
    _wait_rows(ys_ref, b_ref.at[slot], sems.at[1, slot], tm)
    reps = D_MODEL // 128
    ga = jnp.concatenate([ga_ref[...]] * reps, axis=-1)
    gb = jnp.concatenate([gb_ref[...]] * reps, axis=-1)
    f = a_ref[slot] * ga + b_ref[slot] * gb
    x = x_ref[...] + mod_ref[:, 5 * D_MODEL:6 * D_MODEL] * f
    y = (x * lax.rsqrt(jnp.mean(x * x, axis=-1, keepdims=True) + EPS)) * g_ref[...]
    is_ctx = pl.program_id(0) < T_CTX // tm

    @pl.when(is_ctx)
    def _():
        oc_ref[...] = y

    @pl.when(jnp.logical_not(is_ctx))
    def _():
        os_ref[...] = y


def _combine(ys, pos1, pos2, x, ga, gb, mod, final_g):
    tm = GATHER_TM
    nblk = T_ALL // tm
    cidx = _cond_index(tm)
    row = lambda i: (i, 0)
    idx_blk = pl.BlockSpec((None, 1, tm), lambda i: (i, 0, 0), memory_space=pltpu.SMEM)
    idx1 = pos1.reshape(nblk, 1, tm)
    idx2 = pos2.reshape(nblk, 1, tm)
    return pl.pallas_call(
        functools.partial(_combine_kernel, tm=tm),
        grid=(nblk,),
        in_specs=[
            idx_blk, idx_blk, _next_idx_spec(tm, nblk), _next_idx_spec(tm, nblk),
            pl.BlockSpec(memory_space=pl.ANY),
            pl.BlockSpec((tm, D_MODEL), row),
            pl.BlockSpec((tm, 128), row),
            pl.BlockSpec((tm, 128), row),
            pl.BlockSpec((None, 1, N_MOD), lambda i: (cidx(i), 0, 0)),
            pl.BlockSpec((1, D_MODEL), lambda i: (0, 0)),
        ],
        out_specs=list(_ctx_smp_specs(tm, D_MODEL)),
        out_shape=[jax.ShapeDtypeStruct((T_CTX, D_MODEL), F32), jax.ShapeDtypeStruct((T_SMP, D_MODEL), F32)],
        scratch_shapes=[
            pltpu.VMEM((2, tm, D_MODEL), F32),
            pltpu.VMEM((2, tm, D_MODEL), F32),
            pltpu.SemaphoreType.DMA((2, 2)),
        ],
        compiler_params=_cparams("arbitrary"),
        name="moe_combine_final_norm",
    )(idx1, idx2, idx1, idx2, ys, x, ga, gb, mod, final_g)


def kernel(x_prompt, x_sample, cache_na_k, cache_na_v, state_ret, c, c_ctx, w_ada, b_ada, norm_g, final_g, w_in_e, w_conv_e, na_rpb_e, w_out_e, ff_gate_e, ff_up_e, ff_down_e, w_in_o, ret_decay_o, w_out_o, router_o, ex_gate_o, ex_up_o, ex_down_o):
    x_ctx = x_prompt.reshape(T_CTX, D_MODEL)
    x_smp = x_sample.reshape(T_SMP, D_MODEL)
    cond = jnp.concatenate([c_ctx[None, :], c, jnp.zeros((N_COND - 1 - DEC_BATCH, D_MODEL), F32)], axis=0)
    mod = _ada(cond, w_ada, b_ada)
    mod0 = mod[0].reshape(N_COND, 1, N_MOD)
    mod1 = mod[1].reshape(N_COND, 1, N_MOD)

    bcu, q, k, v, kb, vb = _in_proj_e(x_ctx, x_smp, mod0, norm_g[0, 0][None, :], w_in_e[0].astype(BF16))
    o_ctx = _attn_ctx(q, kb, vb)
    o_smp = _attn_na(q, kb, vb,
                     cache_na_k[:, 0].reshape(DEC_BATCH, PAST_LEN, HALF),
                     cache_na_v[:, 0].reshape(DEC_BATCH, PAST_LEN, HALF),
                     _na_bias_table(na_rpb_e[0]))
    x1 = _out_proj_e(bcu, o_ctx, o_smp, x_ctx, x_smp, mod0, w_conv_e[0].T, w_out_e[0].astype(BF16))
    x2 = _ffn(x1, mod0, norm_g[0, 1][None, :], ff_gate_e[0].astype(BF16), ff_up_e[0].astype(BF16),
              ff_down_e[0].astype(BF16))

    mc_hi, mc_lo = _hi_lo(_channel_dft_table())
    ar, ai, q1, k1, v1, g1 = _in_proj_o(x2, mod1, norm_g[1, 0][None, :], w_in_o[0].astype(BF16), mc_hi, mc_lo)
    yc_ctx = _fourier_ctx(ar, ai)
    yc_smp = _fourier_smp(ar, ai)
    lg = jax.nn.log_sigmoid(ret_decay_o[0].astype(F32))
    yd_ctx, new_state = _retention(lg, q1, k1, v1, g1, None, seq=SEQ, chunk=RET_CHUNK_CTX, nbatch=BATCH, row0=0,
                                   nh=8)
    yd_smp = _retention(lg, q1, k1, v1, g1, state_ret[:, 0], seq=DEC_SEQ, chunk=RET_CHUNK_SMP, nbatch=DEC_BATCH,
                        row0=T_CTX, nh=4)
    r_hi, r_lo = _hi_lo(jnp.pad(router_o[0], ((0, 0), (0, 128 - N_EXPERTS))))
    x3, hm, e1, e2, ga, gb = _out_proj_o(yc_ctx, yc_smp, yd_ctx, yd_smp, x2, mod1, w_out_o[0].astype(BF16),
                                         norm_g[1, 1][None, :], r_hi, r_lo)
    slot_t, pos1, pos2, block_e, n_valid = _routing_tables(e1[:, 0], e2[:, 0])
    xs = _gather_rows(hm, slot_t)
    ys = _experts(xs, block_e, n_valid, ex_gate_o[0].astype(BF16), ex_up_o[0].astype(BF16),
                  ex_down_o[0].astype(BF16))
    y_ctx, y_smp = _combine(ys, pos1, pos2, x3, ga, gb, mod1, final_g[None, :])

    y_prompt = y_ctx.reshape(BATCH, SEQ, D_MODEL)
    y_sample = y_smp.reshape(DEC_BATCH, DEC_SEQ, D_MODEL)
    new_na_k = k.reshape(BATCH, 1, SEQ, N_HEADS, HEAD_DIM)
    new_na_v = v.reshape(BATCH, 1, SEQ, N_HEADS, HEAD_DIM)
    new_state_ret = new_state.reshape(BATCH, 1, 2, N_HEADS, HEAD_DIM, HEAD_DIM)
    return (y_prompt, y_sample, new_na_k, new_na_v, new_state_ret)
```

```python
import functools

import numpy as np
import jax
import jax.numpy as jnp
from jax import lax
from jax.experimental import pallas as pl
from jax.experimental.pallas import tpu as pltpu

D_MODEL = 1024
BATCH = 32
SEQ = 256
DEC_BATCH = 4
DEC_SEQ = 4096
PAST_LEN = 256
GRID_W = 64
HEAD_DIM = 64
HALF = D_MODEL // 2
N_HEADS = HALF // HEAD_DIM
NA_WIN_R = 8
NA_WIN_C = 16
FT_GROUPS = 4
FT_GROUP_CH = HALF // FT_GROUPS
RET_CHUNK_CTX = 256
RET_CHUNK_SMP = 512
D_FF = 2816
N_EXPERTS = 8
D_FF_EXPERT = 3584
EPS = 1e-6
NEG_INF = -1e30
ATT_SCALE = HEAD_DIM ** -0.5

T_CTX = BATCH * SEQ
T_SMP = DEC_BATCH * DEC_SEQ
T_ALL = T_CTX + T_SMP
N_COND = 8
N_MOD = 6 * D_MODEL
GRID_H = DEC_SEQ // GRID_W
FT_N1 = 64

F32 = jnp.float32
BF16 = jnp.bfloat16
VMEM_LIMIT = 56 * 1024 * 1024

MOE_TM = 512
MOE_TF = 1792
MOE_A = 2 * T_ALL
MOE_NB = MOE_A // MOE_TM + N_EXPERTS
MOE_P = MOE_NB * MOE_TM
GATHER_TM = 256


def _cparams(*sem):
    return pltpu.CompilerParams(dimension_semantics=sem, vmem_limit_bytes=VMEM_LIMIT)


def _cond_index(tm):
    n_ctx = T_CTX // tm
    per_b = DEC_SEQ // tm

    def f(i):
        return jnp.where(i < n_ctx, 0, 1 + (i - n_ctx) // per_b)

    return f


def _ctx_smp_specs(tm, width):
    n_ctx = T_CTX // tm
    return (pl.BlockSpec((tm, width), lambda i: (jnp.minimum(i, n_ctx - 1), 0)),
            pl.BlockSpec((tm, width), lambda i: (jnp.maximum(i - n_ctx, 0), 0)))


def _pick_ctx_smp(tm, c_ref, s_ref):
    return jnp.where(pl.program_id(0) < T_CTX // tm, c_ref[...], s_ref[...])


def _silu(x):
    return x * (1.0 / (1.0 + jnp.exp(-x)))


def _norm_mod(x, g, scale, shift):
    y = x * lax.rsqrt(jnp.mean(x * x, axis=-1, keepdims=True) + EPS)
    return (y * g) * (1.0 + scale) + shift


def _split(a):
    hi = a.astype(BF16)
    lo = (a - hi.astype(F32)).astype(BF16)
    return hi, lo


def _dot(a, b):
    return jnp.dot(a, b, preferred_element_type=F32)


def _dot_nt(a, b):
    return lax.dot_general(a, b, (((1,), (1,)), ((), ())), preferred_element_type=F32)


def _dot3(a_hi, a_lo, b_hi, b_lo):
    return _dot(a_hi, b_hi) + (_dot(a_hi, b_lo) + _dot(a_lo, b_hi))


def _ada_kernel(c_ref, w_ref, b_ref, o_ref):
    s = _silu(c_ref[...]).astype(BF16)
    o_ref[...] = _dot(s, w_ref[...].astype(BF16)) + b_ref[...]


def _ada(cond, w_ada, b_ada):
    depth = w_ada.shape[0]
    tn = 1536
    return pl.pallas_call(
        _ada_kernel,
        grid=(depth, N_MOD // tn),
        in_specs=[
            pl.BlockSpec((N_COND, D_MODEL), lambda l, j: (0, 0)),
            pl.BlockSpec((None, D_MODEL, tn), lambda l, j: (l, 0, j)),
            pl.BlockSpec((None, 1, tn), lambda l, j: (l, 0, j)),
        ],
        out_specs=pl.BlockSpec((None, N_COND, tn), lambda l, j: (l, 0, j)),
        out_shape=jax.ShapeDtypeStruct((depth, N_COND, N_MOD), F32),
        compiler_params=_cparams("parallel", "parallel"),
        name="ada_mod",
    )(cond, w_ada, b_ada.reshape(depth, 1, N_MOD))


def _inproj_e_kernel(xc_ref, xs_ref, mod_ref, g_ref, w_ref, bcu_ref, q_ref, k_ref, v_ref, kb_ref, vb_ref, *, tm):
    m = mod_ref[...]
    x = _pick_ctx_smp(tm, xc_ref, xs_ref)
    h = _norm_mod(x, g_ref[...], m[:, D_MODEL:2 * D_MODEL], m[:, 0:D_MODEL]).astype(BF16)
    bcu_ref[...] = _dot(h, w_ref[:, 0:3 * HALF])
    q_ref[...] = _dot(h, w_ref[:, 3 * HALF:4 * HALF]).astype(BF16)
    k = _dot(h, w_ref[:, 4 * HALF:5 * HALF])
    kb_ref[...] = k.astype(BF16)
    v = _dot(h, w_ref[:, 5 * HALF:6 * HALF])
    vb_ref[...] = v.astype(BF16)

    @pl.when(pl.program_id(0) < T_CTX // tm)
    def _():
        k_ref[...] = k
        v_ref[...] = v


def _in_proj_e(x_ctx, x_smp, mod, g, w):
    tm = 512
    cidx = _cond_index(tm)
    row = lambda i: (i, 0)
    sds = jax.ShapeDtypeStruct
    return pl.pallas_call(
        functools.partial(_inproj_e_kernel, tm=tm),
        grid=(T_ALL // tm,),
        in_specs=[
            *_ctx_smp_specs(tm, D_MODEL),
            pl.BlockSpec((None, 1, N_MOD), lambda i: (cidx(i), 0, 0)),
            pl.BlockSpec((1, D_MODEL), lambda i: (0, 0)),
            pl.BlockSpec((D_MODEL, 6 * HALF), lambda i: (0, 0)),
        ],
        out_specs=[
            pl.BlockSpec((tm, 3 * HALF), row),
            pl.BlockSpec((tm, HALF), row),
            _ctx_smp_specs(tm, HALF)[0],
            _ctx_smp_specs(tm, HALF)[0],
            pl.BlockSpec((tm, HALF), row),
            pl.BlockSpec((tm, HALF), row),
        ],
        out_shape=[
            sds((T_ALL, 3 * HALF), F32),
            sds((T_ALL, HALF), BF16),
            sds((T_CTX, HALF), F32),
            sds((T_CTX, HALF), F32),
            sds((T_ALL, HALF), BF16),
            sds((T_ALL, HALF), BF16),
        ],
        compiler_params=_cparams("parallel"),
        name="in_proj_even",
    )(x_ctx, x_smp, mod, g, w)


def _softmax_parts(parts):
    m = parts[0].max(axis=-1, keepdims=True)
    for s in parts[1:]:
        m = jnp.maximum(m, s.max(axis=-1, keepdims=True))
    es = [jnp.exp(s - m) for s in parts]
    l = es[0].sum(axis=-1, keepdims=True)
    for e in es[1:]:
        l = l + e.sum(axis=-1, keepdims=True)
    inv = 1.0 / l
    return [e * inv for e in es]


def _attn_ctx_kernel(q_ref, k_ref, v_ref, o_ref):
    for h in range(N_HEADS):
        sl = slice(h * HEAD_DIM, (h + 1) * HEAD_DIM)
        s = _dot_nt(q_ref[:, sl], k_ref[:, sl]) * ATT_SCALE
        (p,) = _softmax_parts([s])
        o_ref[:, sl] = _dot(p.astype(BF16), v_ref[:, sl]).astype(BF16)


def _attn_ctx(q, kb, vb):
    blk = pl.BlockSpec((SEQ, HALF), lambda b: (b, 0))
    return pl.pallas_call(
        _attn_ctx_kernel,
        grid=(BATCH,),
        in_specs=[blk, blk, blk],
        out_specs=blk,
        out_shape=jax.ShapeDtypeStruct((T_CTX, HALF), BF16),
        compiler_params=_cparams("parallel"),
        name="attn_context",
    )(q, kb, vb)


NA_Q_ROWS = 4
NA_K_ROWS = NA_WIN_R + NA_Q_ROWS


def _na_key_row0(rb):
    return jnp.clip(rb * NA_Q_ROWS - NA_WIN_R // 2, 0, GRID_H - NA_K_ROWS)


def _attn_na_kernel(q_ref, k_ref, v_ref, kc_ref, vc_ref, bias_ref, o_ref):
    rb = pl.program_id(1)
    start = pl.multiple_of(_na_key_row0(rb) * GRID_W, GRID_W)
    n_loc = NA_K_ROWS * GRID_W
    for h in range(N_HEADS):
        sl = slice(h * HEAD_DIM, (h + 1) * HEAD_DIM)
        q = q_ref[:, sl]
        s_loc = _dot_nt(q, k_ref[pl.ds(start, n_loc), sl]) * ATT_SCALE + bias_ref[h]
        s_ctx = _dot_nt(q, kc_ref[:, sl].astype(BF16)) * ATT_SCALE
        p_loc, p_ctx = _softmax_parts([s_loc, s_ctx])
        o = _dot(p_loc.astype(BF16), v_ref[pl.ds(start, n_loc), sl])
        o = o + _dot(p_ctx.astype(BF16), vc_ref[:, sl].astype(BF16))
        o_ref[:, sl] = o.astype(BF16)


def _na_bias_table(rpb):
    cols = np.arange(GRID_W)
    col_start = np.clip(cols - NA_WIN_C // 2, 0, GRID_W - NA_WIN_C)
    col_in = (cols[None, :] >= col_start[:, None]) & (cols[None, :] < col_start[:, None] + NA_WIN_C)
    col_off = np.clip(cols[None, :] - cols[:, None], 1 - NA_WIN_C, NA_WIN_C - 1) + (NA_WIN_C - 1)
    n_off = 2 * NA_WIN_C - 1
    pick = jnp.asarray(col_off[None] == np.arange(n_off)[:, None, None], F32)
    by_col = (rpb.astype(F32)[:, :, :, None, None] * pick[None, None]).sum(axis=2)
    by_col = jnp.pad(by_col, ((0, 0), (NA_K_ROWS, NA_K_ROWS), (0, 0), (0, 0)))
    n_blocks = GRID_H // NA_Q_ROWS
    variants = []
    for rb in (0, 1, n_blocks - 1):
        k_row0 = int(np.clip(rb * NA_Q_ROWS - NA_WIN_R // 2, 0, GRID_H - NA_K_ROWS))
        k_row = k_row0 + np.arange(NA_K_ROWS)
        per_q_row = []
        for qr in range(NA_Q_ROWS):
            q_row = rb * NA_Q_ROWS + qr
            win0 = int(np.clip(q_row - NA_WIN_R // 2, 0, GRID_H - NA_WIN_R))
            row_in = (k_row >= win0) & (k_row < win0 + NA_WIN_R)
            ro0 = k_row0 - q_row + (NA_WIN_R - 1) + NA_K_ROWS
            b = by_col[:, ro0:ro0 + NA_K_ROWS]
            keep = row_in[:, None, None] & col_in[None]
            per_q_row.append(jnp.where(keep[None], b, NEG_INF))
        b = jnp.stack(per_q_row, axis=1)
        variants.append(b.transpose(0, 1, 3, 2, 4).reshape(N_HEADS, NA_Q_ROWS * GRID_W, NA_K_ROWS * GRID_W))
    return jnp.stack(variants)


def _attn_na(q, kb, vb, cache_k, cache_v, bias):
    tq = NA_Q_ROWS * GRID_W
    n_blocks = GRID_H // NA_Q_ROWS
    ctx_q_tiles = T_CTX // tq
    ctx_b_tiles = T_CTX // DEC_SEQ

    def bias_idx(b, rb):
        return (jnp.where(rb == 0, 0, jnp.where(rb == n_blocks - 1, 2, 1)), 0, 0, 0)

    full = pl.BlockSpec((DEC_SEQ, HALF), lambda b, r: (ctx_b_tiles + b, 0))
    cache = pl.BlockSpec((None, PAST_LEN, HALF), lambda b, r: (b, 0, 0))
    return pl.pallas_call(
        _attn_na_kernel,
        grid=(DEC_BATCH, n_blocks),
        in_specs=[
            pl.BlockSpec((tq, HALF), lambda b, r: (ctx_q_tiles + b * n_blocks + r, 0)),
            full, full, cache, cache,
            pl.BlockSpec((None, N_HEADS, tq, NA_K_ROWS * GRID_W), bias_idx),
        ],
        out_specs=pl.BlockSpec((tq, HALF), lambda b, r: (b * n_blocks + r, 0)),
        out_shape=jax.ShapeDtypeStruct((T_SMP, HALF), BF16),
        compiler_params=_cparams("parallel", "arbitrary"),
        name="attn_neighbourhood",
    )(q, kb, vb, cache_k, cache_v, bias)


def _seq_edges(i, tm):
    n_ctx = T_CTX // tm
    ctx_per = SEQ // tm
    smp_per = DEC_SEQ // tm
    j = i - n_ctx
    first = jnp.where(i < n_ctx, i % ctx_per == 0, j % smp_per == 0)
    last = jnp.where(i < n_ctx, i % ctx_per == ctx_per - 1, j % smp_per == smp_per - 1)
    return first, last


def _outproj_e_kernel(bcu_ref, prev_ref, next_ref, oc_ref, os_ref, xc_ref, xs_ref, mod_ref, wc_ref, w_ref, o_ref,
                      *, tm):
    i = pl.program_id(0)
    first, last = _seq_edges(i, tm)
    bcu = bcu_ref[...]
    cu = bcu[:, HALF:2 * HALF] * bcu[:, 2 * HALF:3 * HALF]
    pv = prev_ref[7:8, :]
    nx = next_ref[0:1, :]
    cu_prev_row = jnp.where(first, 0.0, pv[:, HALF:2 * HALF] * pv[:, 2 * HALF:3 * HALF])
    cu_next_row = jnp.where(last, 0.0, nx[:, HALF:2 * HALF] * nx[:, 2 * HALF:3 * HALF])
    rows = lax.broadcasted_iota(jnp.int32, (tm, HALF), 0)
    cu_prev = jnp.where(rows == 0, cu_prev_row, pltpu.roll(cu, 1, axis=0))
    cu_next = jnp.where(rows == tm - 1, cu_next_row, pltpu.roll(cu, tm - 1, axis=0))
    wc = wc_ref[...]
    y_a = bcu[:, 0:HALF] * (cu_prev * wc[0:1] + cu * wc[1:2] + cu_next * wc[2:3])
    y_b = _pick_ctx_smp(tm, oc_ref, os_ref)
    y = _dot(y_a.astype(BF16), w_ref[0:HALF, :]) + _dot(y_b, w_ref[HALF:D_MODEL, :])
    o_ref[...] = _pick_ctx_smp(tm, xc_ref, xs_ref) + mod_ref[:, 2 * D_MODEL:3 * D_MODEL] * y


def _out_proj_e(bcu, o_ctx, o_smp, x_ctx, x_smp, mod, w_conv_t, w):
    tm = 256
    cidx = _cond_index(tm)
    row = lambda i: (i, 0)
    nb8 = T_ALL // 8
    return pl.pallas_call(
        functools.partial(_outproj_e_kernel, tm=tm),
        grid=(T_ALL // tm,),
        in_specs=[
            pl.BlockSpec((tm, 3 * HALF), row),
            pl.BlockSpec((8, 3 * HALF), lambda i: (jnp.maximum(i * (tm // 8) - 1, 0), 0)),
            pl.BlockSpec((8, 3 * HALF), lambda i: (jnp.minimum((i + 1) * (tm // 8), nb8 - 1), 0)),
            *_ctx_smp_specs(tm, HALF),
            *_ctx_smp_specs(tm, D_MODEL),
            pl.BlockSpec((None, 1, N_MOD), lambda i: (cidx(i), 0, 0)),
            pl.BlockSpec((3, HALF), lambda i: (0, 0)),
            pl.BlockSpec((D_MODEL, D_MODEL), lambda i: (0, 0)),
        ],
        out_specs=pl.BlockSpec((tm, D_MODEL), row),
        out_shape=jax.ShapeDtypeStruct((T_ALL, D_MODEL), F32),
        compiler_params=_cparams("parallel"),
        name="out_proj_even",
    )(bcu, bcu, bcu, o_ctx, o_smp, x_ctx, x_smp, mod, w_conv_t, w)


def _ffn_kernel(x_ref, mod_ref, g_ref, wg_ref, wu_ref, wd_ref, o_ref, h_ref, acc_ref):
    f = pl.program_id(1)

    @pl.when(f == 0)
    def _():
        m = mod_ref[...]
        h = _norm_mod(x_ref[...], g_ref[...], m[:, 4 * D_MODEL:5 * D_MODEL], m[:, 3 * D_MODEL:4 * D_MODEL])
        h_ref[...] = h.astype(BF16)
        acc_ref[...] = jnp.zeros_like(acc_ref)

    h = h_ref[...]
    a = _silu(_dot(h, wg_ref[...])) * _dot(h, wu_ref[...])
    acc_ref[...] += _dot(a.astype(BF16), wd_ref[...])

    @pl.when(f == pl.num_programs(1) - 1)
    def _():
        o_ref[...] = x_ref[...] + mod_ref[:, 5 * D_MODEL:6 * D_MODEL] * acc_ref[...]


def _ffn(x, mod, g, wg, wu, wd):
    tm = 512
    tf = D_FF // 2
    cidx = _cond_index(tm)
    return pl.pallas_call(
        _ffn_kernel,
        grid=(T_ALL // tm, D_FF // tf),
        in_specs=[
            pl.BlockSpec((tm, D_MODEL), lambda i, f: (i, 0)),
            pl.BlockSpec((None, 1, N_MOD), lambda i, f: (cidx(i), 0, 0)),
            pl.BlockSpec((1, D_MODEL), lambda i, f: (0, 0)),
            pl.BlockSpec((D_MODEL, tf), lambda i, f: (0, f)),
            pl.BlockSpec((D_MODEL, tf), lambda i, f: (0, f)),
            pl.BlockSpec((tf, D_MODEL), lambda i, f: (f, 0)),
        ],
        out_specs=pl.BlockSpec((tm, D_MODEL), lambda i, f: (i, 0)),
        out_shape=jax.ShapeDtypeStruct((T_ALL, D_MODEL), F32),
        scratch_shapes=[pltpu.VMEM((tm, D_MODEL), BF16), pltpu.VMEM((tm, D_MODEL), F32)],
        compiler_params=_cparams("parallel", "arbitrary"),
        name="ffn_dense",
    )(x, mod, g, wg, wu, wd)


def _inproj_o_kernel(x_ref, mod_ref, g_ref, w_ref, mch_ref, mcl_ref, ar_ref, ai_ref, q_ref, k_ref, v_ref, gt_ref):
    m = mod_ref[...]
    h = _norm_mod(x_ref[...], g_ref[...], m[:, D_MODEL:2 * D_MODEL], m[:, 0:D_MODEL]).astype(BF16)
    u_hi, u_lo = _split(_dot(h, w_ref[:, 0:HALF]))
    for grp in range(FT_GROUPS):
        sl = slice(grp * FT_GROUP_CH, (grp + 1) * FT_GROUP_CH)
        a = _dot3(u_hi[:, sl], u_lo[:, sl], mch_ref[...], mcl_ref[...])
        ar_ref[:, sl] = a[:, 0:FT_GROUP_CH]
        ai_ref[:, sl] = a[:, FT_GROUP_CH:2 * FT_GROUP_CH]
    q_ref[...] = _dot(h, w_ref[:, HALF:2 * HALF]).astype(BF16)
    k_ref[...] = _dot(h, w_ref[:, 2 * HALF:3 * HALF]) * ATT_SCALE
    v_ref[...] = _dot(h, w_ref[:, 3 * HALF:4 * HALF]).astype(BF16)
    gt_ref[...] = _dot(h, w_ref[:, 4 * HALF:5 * HALF])


def _in_proj_o(x, mod, g, w, mc_hi, mc_lo):
    tm = 512
    cidx = _cond_index(tm)
    row = lambda i: (i, 0)
    half_out = pl.BlockSpec((tm, HALF), row)
    sds = jax.ShapeDtypeStruct
    return pl.pallas_call(
        _inproj_o_kernel,
        grid=(T_ALL // tm,),
        in_specs=[
            pl.BlockSpec((tm, D_MODEL), row),
            pl.BlockSpec((None, 1, N_MOD), lambda i: (cidx(i), 0, 0)),
            pl.BlockSpec((1, D_MODEL), lambda i: (0, 0)),
            pl.BlockSpec((D_MODEL, 5 * HALF), lambda i: (0, 0)),
            pl.BlockSpec((FT_GROUP_CH, 2 * FT_GROUP_CH), lambda i: (0, 0)),
            pl.BlockSpec((FT_GROUP_CH, 2 * FT_GROUP_CH), lambda i: (0, 0)),
        ],
        out_specs=[half_out] * 6,
        out_shape=[
            sds((T_ALL, HALF), F32),
            sds((T_ALL, HALF), F32),
            sds((T_ALL, HALF), BF16),
            sds((T_ALL, HALF), F32),
            sds((T_ALL, HALF), BF16),
            sds((T_ALL, HALF), F32),
        ],
        compiler_params=_cparams("parallel"),
        name="in_proj_odd",
    )(x, mod, g, w, mc_hi, mc_lo)


def _dft_cos_sin(n):
    k = np.arange(n)
    ang = 2.0 * np.pi * ((k[:, None] * k[None, :]) % n) / n
    return np.cos(ang), np.sin(ang)


def _hi_lo(a):
    a = jnp.asarray(a, F32)
    hi = a.astype(BF16)
    return hi, (a - hi.astype(F32)).astype(BF16)


def _channel_dft_table():
    c, s = _dft_cos_sin(FT_GROUP_CH)
    scale = FT_GROUP_CH ** -0.5
    return np.concatenate([c * scale, -s * scale], axis=1)


def _fourier_ctx_kernel(ar_ref, ai_ref, th_ref, tl_ref, o_ref):
    a_hi, a_lo = _split(jnp.concatenate([ar_ref[...], ai_ref[...]], axis=0))
    o_ref[...] = _dot3(th_ref[...], tl_ref[...], a_hi, a_lo).astype(BF16)


def _fourier_ctx(ar, ai):
    c, s = _dft_cos_sin(SEQ)
    t_hi, t_lo = _hi_lo(np.concatenate([c, s], axis=1) * SEQ ** -0.5)
    blk = pl.BlockSpec((SEQ, HALF), lambda b: (b, 0))
    tab = pl.BlockSpec((SEQ, 2 * SEQ), lambda b: (0, 0))
    return pl.pallas_call(
        _fourier_ctx_kernel,
        grid=(BATCH,),
        in_specs=[blk, blk, tab, tab],
        out_specs=blk,
        out_shape=jax.ShapeDtypeStruct((T_CTX, HALF), BF16),
        compiler_params=_cparams("parallel"),
        name="fourier_context",
    )(ar, ai, t_hi, t_lo)


def _fourier_s1_kernel(ar_ref, ai_ref, mh_ref, ml_ref, yr_ref, yi_ref):
    group = 16
    lanes = ar_ref.shape[-1]

    def p2_group(g, c):
        rows = [pl.ds(g * group + j, FT_N1, stride=FT_N1) for j in range(group)]
        a = jnp.concatenate([jnp.concatenate([ar_ref[r, :] for r in rows], axis=-1),
                             jnp.concatenate([ai_ref[r, :] for r in rows], axis=-1)], axis=0)
        a_hi, a_lo = _split(a)
        y = _dot3(mh_ref[...], ml_ref[...], a_hi, a_lo)
        for j, r in enumerate(rows):
            yr_ref[r, :] = y[0:FT_N1, j * lanes:(j + 1) * lanes]
            yi_ref[r, :] = y[FT_N1:2 * FT_N1, j * lanes:(j + 1) * lanes]
        return c

    lax.fori_loop(0, FT_N1 // group, p2_group, 0)


def _fourier_s3_kernel(yr_ref, yi_ref, tc_ref, ts_ref, mh_ref, ml_ref, o_ref, *, nk):
    lanes = HALF // tc_ref.shape[-1]
    for j in range(nk):
        tc = jnp.concatenate([tc_ref[j]] * lanes, axis=-1)
        ts = jnp.concatenate([ts_ref[j]] * lanes, axis=-1)
        yr = yr_ref[j * FT_N1:(j + 1) * FT_N1, :]
        yi = yi_ref[j * FT_N1:(j + 1) * FT_N1, :]
        z = jnp.concatenate([yr * tc + yi * ts, yi * tc - yr * ts], axis=0)
        z_hi, z_lo = _split(z)
        o_ref[:, j * HALF:(j + 1) * HALF] = _dot3(mh_ref[...], ml_ref[...], z_hi, z_lo).astype(BF16)


def _fourier_smp(ar, ai):
    n1 = FT_N1
    wide = n1 * HALF
    c, s = _dft_cos_sin(n1)
    m1_hi, m1_lo = _hi_lo(np.block([[c, s], [-s, c]]))
    m3_hi, m3_lo = _hi_lo(np.concatenate([c, s], axis=1) / n1)
    kk = np.arange(n1)
    ang = 2.0 * np.pi * (kk[:, None] * kk[None, :]) / DEC_SEQ
    tw_c = jnp.broadcast_to(jnp.asarray(np.cos(ang), F32)[:, :, None], (n1, n1, 128))
    tw_s = jnp.broadcast_to(jnp.asarray(np.sin(ang), F32)[:, :, None], (n1, n1, 128))

    ncol = 128
    ctx_tiles = T_CTX // DEC_SEQ
    a_blk = pl.BlockSpec((DEC_SEQ, ncol), lambda b, j: (ctx_tiles + b, j))
    y_blk = pl.BlockSpec((DEC_SEQ, ncol), lambda b, j: (b, j))
    m1_blk = pl.BlockSpec((2 * n1, 2 * n1), lambda b, j: (0, 0))
    y_sds = jax.ShapeDtypeStruct((T_SMP, HALF), F32)
    yr, yi = pl.pallas_call(
        _fourier_s1_kernel,
        grid=(DEC_BATCH, HALF // ncol),
        in_specs=[a_blk, a_blk, m1_blk, m1_blk],
        out_specs=[y_blk, y_blk],
        out_shape=[y_sds, y_sds],
        compiler_params=_cparams("parallel", "parallel"),
        name="fourier_latent_stage1",
    )(ar, ai, m1_hi, m1_lo)

    nk = 8
    z_blk = pl.BlockSpec((nk * n1, HALF), lambda b, k: (b * (n1 // nk) + k, 0))
    tw_blk = pl.BlockSpec((nk, n1, 128), lambda b, k: (k, 0, 0))
    m3_blk = pl.BlockSpec((n1, 2 * n1), lambda b, k: (0, 0))
    out = pl.pallas_call(
        functools.partial(_fourier_s3_kernel, nk=nk),
        grid=(DEC_BATCH, n1 // nk),
        in_specs=[z_blk, z_blk, tw_blk, tw_blk, m3_blk, m3_blk],
        out_specs=pl.BlockSpec((n1, nk * HALF), lambda b, k: (b, k)),
        out_shape=jax.ShapeDtypeStruct((DEC_BATCH * n1, wide), BF16),
        compiler_params=_cparams("parallel", "parallel"),
        name="fourier_latent_stage3",
    )(yr, yi, tw_c, tw_s, m3_hi, m3_lo)
    return out.reshape(T_SMP, HALF)


def _retention_kernel(lg_ref, q_ref, k_ref, v_ref, g_ref, *rest, seq, chunk, nh, has_s0):
    if has_s0:
        s0_ref, y_ref, of_ref, ob_ref, dec_ref, xi_ref, zeta_ref = rest
        st_ref = None
    else:
        y_ref, st_ref, of_ref, ob_ref, dec_ref, xi_ref, zeta_ref = rest
        s0_ref = None
    L = chunk
    nc = seq // L
    hb = pl.program_id(0)
    one = jnp.ones((1, 1), F32)

    @pl.when(pl.program_id(1) == 0)
    def _():
        diff = (lax.broadcasted_iota(jnp.int32, (L, L), 0)
                - lax.broadcasted_iota(jnp.int32, (L, L), 1)).astype(F32)
        li = lax.broadcasted_iota(jnp.int32, (L, HEAD_DIM), 0).astype(F32)
        for hh in range(nh):
            lgf = lg_ref[0, hb * nh + hh]
            lgb = lg_ref[1, hb * nh + hh]
            dec_ref[2 * hh] = jnp.where(diff >= 0, jnp.exp(lgf * jnp.maximum(diff, 0.0)), 0.0)
            dec_ref[2 * hh + 1] = jnp.where(diff <= 0, jnp.exp(lgb * jnp.maximum(-diff, 0.0)), 0.0)
            xi_ref[2 * hh] = jnp.exp(lgf * (li + 1.0))
            xi_ref[2 * hh + 1] = jnp.exp(lgb * (L - li))
            zeta_ref[2 * hh] = jnp.exp(lgf * (L - 1.0 - li))
            zeta_ref[2 * hh + 1] = jnp.exp(lgb * li)

    gcs = []
    for hh in range(nh):
        gcs.append(jnp.exp(one * (lg_ref[0, hb * nh + hh] * L)))
        gcs.append(jnp.exp(one * (lg_ref[1, hb * nh + hh] * L)))

    def chunk(c, s, t, sl, o_ref):
        r0 = pl.multiple_of(c * L, L)
        qc = q_ref[pl.ds(r0, L), sl]
        kc = k_ref[pl.ds(r0, L), sl]
        vc = v_ref[pl.ds(r0, L), sl]
        inner = _dot_nt(qc, kc.astype(BF16)) * dec_ref[t]
        o_ref[pl.ds(r0, L), sl] = _dot(inner.astype(BF16), vc) + _dot(qc, s.astype(BF16)) * xi_ref[t]
        return s * gcs[t] + _dot((kc * zeta_ref[t]).T.astype(BF16), vc)

    def scan_step(i, states):
        out = []
        for hh in range(nh):
            sl = slice(hh * HEAD_DIM, (hh + 1) * HEAD_DIM)
            out.append(chunk(i, states[2 * hh], 2 * hh, sl, of_ref))
            out.append(chunk(nc - 1 - i, states[2 * hh + 1], 2 * hh + 1, sl, ob_ref))
        return tuple(out)

    if has_s0:
        init = tuple(s0_ref[t % 2, t // 2] for t in range(2 * nh))
    else:
        init = tuple(jnp.zeros((HEAD_DIM, HEAD_DIM), F32) for _ in range(2 * nh))
    final = lax.fori_loop(0, nc, scan_step, init)
    if st_ref is not None:
        for t in range(2 * nh):
            st_ref[t % 2, t // 2] = final[t]

    def finish(c, carry):
        r0 = pl.multiple_of(c * L, L)
        o_all = of_ref[pl.ds(r0, L), :] + ob_ref[pl.ds(r0, L), :]
        gate = _silu(g_ref[pl.ds(r0, L), :])
        for hh in range(nh):
            sl = slice(hh * HEAD_DIM, (hh + 1) * HEAD_DIM)
            o = o_all[:, sl]
            mu = jnp.mean(o, axis=-1, keepdims=True)
            var = jnp.mean(jnp.square(o - mu), axis=-1, keepdims=True)
            y_ref[pl.ds(r0, L), sl] = (gate[:, sl] * ((o - mu) * lax.rsqrt(var + EPS))).astype(BF16)
        return carry

    lax.fori_loop(0, nc, finish, 0)


def _retention(lg, q, k, v, g, s0, *, seq, chunk, nbatch, row0, nh):
    has_s0 = s0 is not None
    tile0 = row0 // seq
    width = nh * HEAD_DIM
    blk = pl.BlockSpec((seq, width), lambda hb, b: (tile0 + b, hb))
    st_blk = pl.BlockSpec((None, 2, nh, HEAD_DIM, HEAD_DIM), lambda hb, b: (b, 0, hb, 0, 0))
    in_specs = [pl.BlockSpec(memory_space=pltpu.SMEM), blk, blk, blk, blk]
    args = [lg, q, k, v, g]
    y_spec = pl.BlockSpec((seq, width), lambda hb, b: (b, hb))
    y_sds = jax.ShapeDtypeStruct((nbatch * seq, HALF), BF16)
    if has_s0:
        in_specs.append(st_blk)
        args.append(s0)
        out_specs, out_shape = y_spec, y_sds
    else:
        out_specs = [y_spec, st_blk]
        out_shape = [y_sds, jax.ShapeDtypeStruct((nbatch, 2, N_HEADS, HEAD_DIM, HEAD_DIM), F32)]
    return pl.pallas_call(
        functools.partial(_retention_kernel, seq=seq, chunk=chunk, nh=nh, has_s0=has_s0),
        grid=(N_HEADS // nh, nbatch),
        in_specs=in_specs,
        out_specs=out_specs,
        out_shape=out_shape,
        scratch_shapes=[
            pltpu.VMEM((seq, width), F32),
            pltpu.VMEM((seq, width), F32),
            pltpu.VMEM((2 * nh, chunk, chunk), F32),
            pltpu.VMEM((2 * nh, chunk, HEAD_DIM), F32),
            pltpu.VMEM((2 * nh, chunk, HEAD_DIM), F32),
        ],
        compiler_params=_cparams("arbitrary", "arbitrary"),
        name="retention_%d" % seq,
    )(*args)


def _outproj_o_kernel(ycc_ref, ycs_ref, ydc_ref, yds_ref, x_ref, mod_ref, w_ref, g_ref, rh_ref, rl_ref,
                      x3_ref, h_ref, e1_ref, e2_ref, ga_ref, gb_ref, *, tm):
    is_ctx = pl.program_id(0) < T_CTX // tm
    y_c = jnp.where(is_ctx, ycc_ref[...], ycs_ref[...])
    y_d = jnp.where(is_ctx, ydc_ref[...], yds_ref[...])
    y = _dot(y_c, w_ref[0:HALF, :]) + _dot(y_d, w_ref[HALF:D_MODEL, :])
    m = mod_ref[...]
    x3 = x_ref[...] + m[:, 2 * D_MODEL:3 * D_MODEL] * y
    x3_ref[...] = x3
    h = _norm_mod(x3, g_ref[...], m[:, 4 * D_MODEL:5 * D_MODEL], m[:, 3 * D_MODEL:4 * D_MODEL])
    h_ref[...] = h
    h_hi, h_lo = _split(h)
    logits = _dot3(h_hi, h_lo, rh_ref[...], rl_ref[...])
    idx = lax.broadcasted_iota(jnp.int32, logits.shape, 1).astype(F32)
    logits = jnp.where(idx < float(N_EXPERTS), logits, -jnp.inf)
    m1 = logits.max(axis=-1, keepdims=True)
    e1 = jnp.where(logits == m1, idx, float(N_EXPERTS)).min(axis=-1, keepdims=True)
    rest = jnp.where(idx == e1, -jnp.inf, logits)
    m2 = rest.max(axis=-1, keepdims=True)
    e2 = jnp.where(rest == m2, idx, float(N_EXPERTS)).min(axis=-1, keepdims=True)
    ex = jnp.exp(m2 - m1)
    den = 1.0 + ex
    wide = (tm, 128)
    e1_ref[...] = jnp.broadcast_to(e1, wide).astype(jnp.int32)
    e2_ref[...] = jnp.broadcast_to(e2, wide).astype(jnp.int32)
    ga_ref[...] = jnp.broadcast_to(1.0 / den, wide)
    gb_ref[...] = jnp.broadcast_to(ex / den, wide)


def _out_proj_o(yc_ctx, yc_smp, yd_ctx, yd_smp, x, mod, w, g, r_hi, r_lo):
    tm = 256
    n_ctx = T_CTX // tm
    cidx = _cond_index(tm)
    row = lambda i: (i, 0)
    ctx_blk = pl.BlockSpec((tm, HALF), lambda i: (jnp.minimum(i, n_ctx - 1), 0))
    smp_blk = pl.BlockSpec((tm, HALF), lambda i: (jnp.maximum(i - n_ctx, 0), 0))
    sds = jax.ShapeDtypeStruct
    rep = pl.BlockSpec((tm, 128), row)
    return pl.pallas_call(
        functools.partial(_outproj_o_kernel, tm=tm),
        grid=(T_ALL // tm,),
        in_specs=[
            ctx_blk, smp_blk, ctx_blk, smp_blk,
            pl.BlockSpec((tm, D_MODEL), row),
            pl.BlockSpec((None, 1, N_MOD), lambda i: (cidx(i), 0, 0)),
            pl.BlockSpec((D_MODEL, D_MODEL), lambda i: (0, 0)),
            pl.BlockSpec((1, D_MODEL), lambda i: (0, 0)),
            pl.BlockSpec((D_MODEL, 128), lambda i: (0, 0)),
            pl.BlockSpec((D_MODEL, 128), lambda i: (0, 0)),
        ],
        out_specs=[pl.BlockSpec((tm, D_MODEL), row), pl.BlockSpec((tm, D_MODEL), row), rep, rep, rep, rep],
        out_shape=[
            sds((T_ALL, D_MODEL), F32), sds((T_ALL, D_MODEL), F32),
            sds((T_ALL, 128), jnp.int32), sds((T_ALL, 128), jnp.int32),
            sds((T_ALL, 128), F32), sds((T_ALL, 128), F32),
        ],
        compiler_params=_cparams("parallel"),
        name="out_proj_odd_route",
    )(yc_ctx, yc_smp, yd_ctx, yd_smp, x, mod, w, g, r_hi, r_lo)


def _routing_tables(e1, e2):
    flat_e = jnp.concatenate([e1, e2])
    onehot = (flat_e[:, None] == jnp.arange(N_EXPERTS, dtype=jnp.int32)[None, :]).astype(jnp.int32)
    csum = jnp.cumsum(onehot, axis=0)
    counts = csum[-1]
    padded = (counts + MOE_TM - 1) // MOE_TM * MOE_TM
    ends = jnp.cumsum(padded)
    pos = jnp.sum(onehot * (csum - 1 + (ends - padded)[None, :]), axis=1)
    tok = jnp.tile(jnp.arange(T_ALL, dtype=jnp.int32), 2)
    slot_t = jnp.zeros((MOE_P,), jnp.int32).at[pos].set(tok)
    block_row0 = jnp.arange(MOE_NB, dtype=jnp.int32) * MOE_TM
    block_e = jnp.minimum(
        jnp.sum((ends[None, :] <= block_row0[:, None]).astype(jnp.int32), axis=1), N_EXPERTS - 1
    ).astype(jnp.int32)
    n_valid = (ends[-1] // MOE_TM).astype(jnp.int32).reshape(1)
    return slot_t, pos[:T_ALL].astype(jnp.int32), pos[T_ALL:].astype(jnp.int32), block_e, n_valid


def _row_copy(src_ref, dst_ref, sem, src_row, dst_row):
    return pltpu.make_async_copy(src_ref.at[pl.ds(src_row, 1), :], dst_ref.at[pl.ds(dst_row, 1), :], sem)


def _start_rows(src_ref, dst_ref, sem, idx_ref, tm):
    def start(r, c):
        _row_copy(src_ref, dst_ref, sem, idx_ref[0, r], r).start()
        return c

    lax.fori_loop(0, tm, start, 0, unroll=8)


def _wait_rows(src_ref, dst_ref, sem, tm):
    pltpu.make_async_copy(src_ref.at[pl.ds(0, tm), :], dst_ref, sem).wait()


def _gather_kernel(idx_ref, nxt_ref, src_ref, o_ref, buf_ref, sems, *, tm):
    i = pl.program_id(0)
    slot = i % 2

    @pl.when(i == 0)
    def _():
        _start_rows(src_ref, buf_ref.at[0], sems.at[0], idx_ref, tm)

    @pl.when(i + 1 < pl.num_programs(0))
    def _():
        _start_rows(src_ref, buf_ref.at[1 - slot], sems.at[1 - slot], nxt_ref, tm)

    _wait_rows(src_ref, buf_ref.at[slot], sems.at[slot], tm)
    o_ref[...] = buf_ref[slot]


def _next_idx_spec(tm, nblk):
    return pl.BlockSpec((None, 1, tm), lambda i: (jnp.minimum(i + 1, nblk - 1), 0, 0), memory_space=pltpu.SMEM)


def _gather_rows(src, slot_t):
    tm = GATHER_TM
    nblk = MOE_P // tm
    idx = slot_t.reshape(nblk, 1, tm)
    return pl.pallas_call(
        functools.partial(_gather_kernel, tm=tm),
        grid=(nblk,),
        in_specs=[
            pl.BlockSpec((None, 1, tm), lambda i: (i, 0, 0), memory_space=pltpu.SMEM),
            _next_idx_spec(tm, nblk),
            pl.BlockSpec(memory_space=pl.ANY),
        ],
        out_specs=pl.BlockSpec((tm, D_MODEL), lambda i: (i, 0)),
        out_shape=jax.ShapeDtypeStruct((MOE_P, D_MODEL), F32),
        scratch_shapes=[pltpu.VMEM((2, tm, D_MODEL), F32), pltpu.SemaphoreType.DMA((2,))],
        compiler_params=_cparams("arbitrary"),
        name="moe_gather",
    )(idx, idx, src)


def _experts_kernel(be_ref, nv_ref, x_ref, wg_ref, wu_ref, wd_ref, o_ref, xb_ref, acc_ref):
    i = pl.program_id(0)
    f = pl.program_id(1)
    valid = i < nv_ref[0]

    @pl.when(jnp.logical_and(valid, f == 0))
    def _():
        xb_ref[...] = x_ref[...].astype(BF16)
        acc_ref[...] = jnp.zeros_like(acc_ref)

    @pl.when(valid)
    def _():
        x = xb_ref[...]
        a = _silu(_dot(x, wg_ref[...])) * _dot(x, wu_ref[...])
        acc_ref[...] += _dot(a.astype(BF16), wd_ref[...])

    last = f == pl.num_programs(1) - 1

    @pl.when(jnp.logical_and(valid, last))
    def _():
        o_ref[...] = acc_ref[...]

    @pl.when(jnp.logical_and(jnp.logical_not(valid), last))
    def _():
        o_ref[...] = jnp.zeros_like(o_ref)


def _experts(xs, block_e, n_valid, wg, wu, wd):
    nf = D_FF_EXPERT // MOE_TF

    def f_eff(i, f, nv):
        return jnp.where(i < nv[0], f, nf - 1)

    return pl.pallas_call(
        _experts_kernel,
        grid_spec=pltpu.PrefetchScalarGridSpec(
            num_scalar_prefetch=2,
            grid=(MOE_NB, nf),
            in_specs=[
                pl.BlockSpec((MOE_TM, D_MODEL), lambda i, f, be, nv: (i, 0)),
                pl.BlockSpec((None, D_MODEL, MOE_TF), lambda i, f, be, nv: (be[i], 0, f_eff(i, f, nv))),
                pl.BlockSpec((None, D_MODEL, MOE_TF), lambda i, f, be, nv: (be[i], 0, f_eff(i, f, nv))),
                pl.BlockSpec((None, MOE_TF, D_MODEL), lambda i, f, be, nv: (be[i], f_eff(i, f, nv), 0)),
            ],
            out_specs=pl.BlockSpec((MOE_TM, D_MODEL), lambda i, f, be, nv: (i, 0)),
            scratch_shapes=[pltpu.VMEM((MOE_TM, D_MODEL), BF16), pltpu.VMEM((MOE_TM, D_MODEL), F32)],
        ),
        out_shape=jax.ShapeDtypeStruct((MOE_P, D_MODEL), F32),
        compiler_params=_cparams("arbitrary", "arbitrary"),
        name="moe_experts",
    )(block_e, n_valid, xs, wg, wu, wd)


def _combine_kernel(p1_ref, p2_ref, n1_ref, n2_ref, ys_ref, x_ref, ga_ref, gb_ref, mod_ref, g_ref, oc_ref, os_ref,
                    a_ref, b_ref, sems, *, tm):
    i = pl.program_id(0)
    slot = i % 2

    def start_tile(s, i1_ref, i2_ref):
        _start_rows(ys_ref, a_ref.at[s], sems.at[0, s], i1_ref, tm)
        _start_rows(ys_ref, b_ref.at[s], sems.at[1, s], i2_ref, tm)

    @pl.when(i == 0)
    def _():
        start_tile(0, p1_ref, p2_ref)

    @pl.when(i + 1 < pl.num_programs(0))
    def _():
        start_tile(1 - slot, n1_ref, n2_ref)

    _wait_rows(ys_ref, a_ref.at[slot], sems.at[0, slot], tm)
    _wait_rows(ys_ref, b_ref.at[slot], sems.at[1, slot], tm)
    reps = D_MODEL // 128
    ga = jnp.concatenate([ga_ref[...]] * reps, axis=-1)
    gb = jnp.concatenate([gb_ref[...]] * reps, axis=-1)
    f = a_ref[slot] * ga + b_ref[slot] * gb
    x = x_ref[...] + mod_ref[:, 5 * D_MODEL:6 * D_MODEL] * f
    y = (x * lax.rsqrt(jnp.mean(x * x, axis=-1, keepdims=True) + EPS)) * g_ref[...]
    is_ctx = pl.program_id(0) < T_CTX // tm

    @pl.when(is_ctx)
    def _():
        oc_ref[...] = y

    @pl.when(jnp.logical_not(is_ctx))
    def _():
        os_ref[...] = y


def _combine(ys, pos1, pos2, x, ga, gb, mod, final_g):
    tm = GATHER_TM
    nblk = T_ALL // tm
    cidx = _cond_index(tm)
    row = lambda i: (i, 0)
    idx_blk = pl.BlockSpec((None, 1, tm), lambda i: (i, 0, 0), memory_space=pltpu.SMEM)
    idx1 = pos1.reshape(nblk, 1, tm)
    idx2 = pos2.reshape(nblk, 1, tm)
    return pl.pallas_call(
        functools.partial(_combine_kernel, tm=tm),
        grid=(nblk,),
        in_specs=[
            idx_blk, idx_blk, _next_idx_spec(tm, nblk), _next_idx_spec(tm, nblk),
            pl.BlockSpec(memory_space=pl.ANY),
            pl.BlockSpec((tm, D_MODEL), row),
            pl.BlockSpec((tm, 128), row),
            pl.BlockSpec((tm, 128), row),
            pl.BlockSpec((None, 1, N_MOD), lambda i: (cidx(i), 0, 0)),
            pl.BlockSpec((1, D_MODEL), lambda i: (0, 0)),
        ],
        out_specs=list(_ctx_smp_specs(tm, D_MODEL)),
        out_shape=[jax.ShapeDtypeStruct((T_CTX, D_MODEL), F32), jax.ShapeDtypeStruct((T_SMP, D_MODEL), F32)],
        scratch_shapes=[
            pltpu.VMEM((2, tm, D_MODEL), F32),
            pltpu.VMEM((2, tm, D_MODEL), F32),
            pltpu.SemaphoreType.DMA((2, 2)),
        ],
        compiler_params=_cparams("arbitrary"),
        name="moe_combine_final_norm",
    )(idx1, idx2, idx1, idx2, ys, x, ga, gb, mod, final_g)


def kernel(x_prompt, x_sample, cache_na_k, cache_na_v, state_ret, c, c_ctx, w_ada, b_ada, norm_g, final_g, w_in_e, w_conv_e, na_rpb_e, w_out_e, ff_gate_e, ff_up_e, ff_down_e, w_in_o, ret_decay_o, w_out_o, router_o, ex_gate_o, ex_up_o, ex_down_o):
    x_ctx = x_prompt.reshape(T_CTX, D_MODEL)
    x_smp = x_sample.reshape(T_SMP, D_MODEL)
    cond = jnp.concatenate([c_ctx[None, :], c, jnp.zeros((N_COND - 1 - DEC_BATCH, D_MODEL), F32)], axis=0)
    mod = _ada(cond, w_ada, b_ada)
    mod0 = mod[0].reshape(N_COND, 1, N_MOD)
    mod1 = mod[1].reshape(N_COND, 1, N_MOD)

    bcu, q, k, v, kb, vb = _in_proj_e(x_ctx, x_smp, mod0, norm_g[0, 0][None, :], w_in_e[0].astype(BF16))
    o_ctx = _attn_ctx(q, kb, vb)
    o_smp = _attn_na(q, kb, vb,
                     cache_na_k[:, 0].reshape(DEC_BATCH, PAST_LEN, HALF),
                     cache_na_v[:, 0].reshape(DEC_BATCH, PAST_LEN, HALF),
                     _na_bias_table(na_rpb_e[0]))
    x1 = _out_proj_e(bcu, o_ctx, o_smp, x_ctx, x_smp, mod0, w_conv_e[0].T, w_out_e[0].astype(BF16))
    x2 = _ffn(x1, mod0, norm_g[0, 1][None, :], ff_gate_e[0].astype(BF16), ff_up_e[0].astype(BF16),
              ff_down_e[0].astype(BF16))

    mc_hi, mc_lo = _hi_lo(_channel_dft_table())
    ar, ai, q1, k1, v1, g1 = _in_proj_o(x2, mod1, norm_g[1, 0][None, :], w_in_o[0].astype(BF16), mc_hi, mc_lo)
    yc_ctx = _fourier_ctx(ar, ai)
    yc_smp = _fourier_smp(ar, ai)
    lg = jax.nn.log_sigmoid(ret_decay_o[0].astype(F32))
    yd_ctx, new_state = _retention(lg, q1, k1, v1, g1, None, seq=SEQ, chunk=RET_CHUNK_CTX, nbatch=BATCH, row0=0,
                                   nh=8)
    yd_smp = _retention(lg, q1, k1, v1, g1, state_ret[:, 0], seq=DEC_SEQ, chunk=RET_CHUNK_SMP, nbatch=DEC_BATCH,
                        row0=T_CTX, nh=4)
    r_hi, r_lo = _hi_lo(jnp.pad(router_o[0], ((0, 0), (0, 128 - N_EXPERTS))))
    x3, hm, e1, e2, ga, gb = _out_proj_o(yc_ctx, yc_smp, yd_ctx, yd_smp, x2, mod1, w_out_o[0].astype(BF16),
                                         norm_g[1, 1][None, :], r_hi, r_lo)
    slot_t, pos1, pos2, block_e, n_valid = _routing_tables(e1[:, 0], e2[:, 0])
    xs = _gather_rows(hm, slot_t)
    ys = _experts(xs, block_e, n_valid, ex_gate_o[0].astype(BF16), ex_up_o[0].astype(BF16),
                  ex_down_o[0].astype(BF16))
    y_ctx, y_smp = _combine(ys, pos1, pos2, x3, ga, gb, mod1, final_g[None, :])

    y_prompt = y_ctx.reshape(BATCH, SEQ, D_MODEL)
    y_sample = y_smp.reshape(DEC_BATCH, DEC_SEQ, D_MODEL)
    new_na_k = k.reshape(BATCH, 1, SEQ, N_HEADS, HEAD_DIM)
    new_na_v = v.reshape(BATCH, 1, SEQ, N_HEADS, HEAD_DIM)
    new_state_ret = new_state.reshape(BATCH, 1, 2, N_HEADS, HEAD_DIM, HEAD_DIM)
    return (y_prompt, y_sample, new_na_k, new_na_v, new_state_ret)
```

```python
import functools

import numpy as np
import jax
import jax.numpy as jnp
from jax import lax
from jax.experimental import pallas as pl
from jax.experimental.pallas import tpu as pltpu

D_MODEL = 1024
BATCH = 32
SEQ = 256
DEC_BATCH = 4
DEC_SEQ = 4096
PAST_LEN = 256
GRID_W = 64
HEAD_DIM = 64
HALF = D_MODEL // 2
N_HEADS = HALF // HEAD_DIM
NA_WIN_R = 8
NA_WIN_C = 16
FT_GROUPS = 4
FT_GROUP_CH = HALF // FT_GROUPS
RET_CHUNK_CTX = 256
RET_CHUNK_SMP = 512
D_FF = 2816
N_EXPERTS = 8
D_FF_EXPERT = 3584
EPS = 1e-6
NEG_INF = -1e30
ATT_SCALE = HEAD_DIM ** -0.5

T_CTX = BATCH * SEQ
T_SMP = DEC_BATCH * DEC_SEQ
T_ALL = T_CTX + T_SMP
N_COND = 8
N_MOD = 6 * D_MODEL
GRID_H = DEC_SEQ // GRID_W
FT_N1 = 64

F32 = jnp.float32
BF16 = jnp.bfloat16
VMEM_LIMIT = 56 * 1024 * 1024

MOE_TM = 512
MOE_TF = 1792
MOE_A = 2 * T_ALL
MOE_NB = MOE_A // MOE_TM + N_EXPERTS
MOE_P = MOE_NB * MOE_TM
GATHER_TM = 256


def _cparams(*sem):
    return pltpu.CompilerParams(dimension_semantics=sem, vmem_limit_bytes=VMEM_LIMIT)


def _cond_index(tm):
    n_ctx = T_CTX // tm
    per_b = DEC_SEQ // tm

    def f(i):
        return jnp.where(i < n_ctx, 0, 1 + (i - n_ctx) // per_b)

    return f


def _ctx_smp_specs(tm, width):
    n_ctx = T_CTX // tm
    return (pl.BlockSpec((tm, width), lambda i: (jnp.minimum(i, n_ctx - 1), 0)),
            pl.BlockSpec((tm, width), lambda i: (jnp.maximum(i - n_ctx, 0), 0)))


def _pick_ctx_smp(tm, c_ref, s_ref):
    return jnp.where(pl.program_id(0) < T_CTX // tm, c_ref[...], s_ref[...])


def _silu(x):
    return x * (1.0 / (1.0 + jnp.exp(-x)))


def _norm_mod(x, g, scale, shift):
    y = x * lax.rsqrt(jnp.mean(x * x, axis=-1, keepdims=True) + EPS)
    return (y * g) * (1.0 + scale) + shift


def _split(a):
    hi = a.astype(BF16)
    lo = (a - hi.astype(F32)).astype(BF16)
    return hi, lo


def _dot(a, b):
    return jnp.dot(a, b, preferred_element_type=F32)


def _dot_nt(a, b):
    return lax.dot_general(a, b, (((1,), (1,)), ((), ())), preferred_element_type=F32)


def _dot3(a_hi, a_lo, b_hi, b_lo):
    return _dot(a_hi, b_hi) + (_dot(a_hi, b_lo) + _dot(a_lo, b_hi))


def _ada_kernel(c_ref, w_ref, b_ref, o_ref):
    s = _silu(c_ref[...]).astype(BF16)
    o_ref[...] = _dot(s, w_ref[...].astype(BF16)) + b_ref[...]


def _ada(cond, w_ada, b_ada):
    depth = w_ada.shape[0]
    tn = 1536
    return pl.pallas_call(
        _ada_kernel,
        grid=(depth, N_MOD // tn),
        in_specs=[
            pl.BlockSpec((N_COND, D_MODEL), lambda l, j: (0, 0)),
            pl.BlockSpec((None, D_MODEL, tn), lambda l, j: (l, 0, j)),
            pl.BlockSpec((None, 1, tn), lambda l, j: (l, 0, j)),
        ],
        out_specs=pl.BlockSpec((None, N_COND, tn), lambda l, j: (l, 0, j)),
        out_shape=jax.ShapeDtypeStruct((depth, N_COND, N_MOD), F32),
        compiler_params=_cparams("parallel", "parallel"),
        name="ada_mod",
    )(cond, w_ada, b_ada.reshape(depth, 1, N_MOD))


def _inproj_e_kernel(xc_ref, xs_ref, mod_ref, g_ref, w_ref, bcu_ref, q_ref, k_ref, v_ref, kb_ref, vb_ref, *, tm):
    m = mod_ref[...]
    x = _pick_ctx_smp(tm, xc_ref, xs_ref)
    h = _norm_mod(x, g_ref[...], m[:, D_MODEL:2 * D_MODEL], m[:, 0:D_MODEL]).astype(BF16)
    bcu_ref[...] = _dot(h, w_ref[:, 0:3 * HALF])
    q_ref[...] = _dot(h, w_ref[:, 3 * HALF:4 * HALF]).astype(BF16)
    k = _dot(h, w_ref[:, 4 * HALF:5 * HALF])
    kb_ref[...] = k.astype(BF16)
    v = _dot(h, w_ref[:, 5 * HALF:6 * HALF])
    vb_ref[...] = v.astype(BF16)

    @pl.when(pl.program_id(0) < T_CTX // tm)
    def _():
        k_ref[...] = k
        v_ref[...] = v


def _in_proj_e(x_ctx, x_smp, mod, g, w):
    tm = 512
    cidx = _cond_index(tm)
    row = lambda i: (i, 0)
    sds = jax.ShapeDtypeStruct
    return pl.pallas_call(
        functools.partial(_inproj_e_kernel, tm=tm),
        grid=(T_ALL // tm,),
        in_specs=[
            *_ctx_smp_specs(tm, D_MODEL),
            pl.BlockSpec((None, 1, N_MOD), lambda i: (cidx(i), 0, 0)),
            pl.BlockSpec((1, D_MODEL), lambda i: (0, 0)),
            pl.BlockSpec((D_MODEL, 6 * HALF), lambda i: (0, 0)),
        ],
        out_specs=[
            pl.BlockSpec((tm, 3 * HALF), row),
            pl.BlockSpec((tm, HALF), row),
            _ctx_smp_specs(tm, HALF)[0],
            _ctx_smp_specs(tm, HALF)[0],
            pl.BlockSpec((tm, HALF), row),
            pl.BlockSpec((tm, HALF), row),
        ],
        out_shape=[
            sds((T_ALL, 3 * HALF), F32),
            sds((T_ALL, HALF), BF16),
            sds((T_CTX, HALF), F32),
            sds((T_CTX, HALF), F32),
            sds((T_ALL, HALF), BF16),
            sds((T_ALL, HALF), BF16),
        ],
        compiler_params=_cparams("parallel"),
        name="in_proj_even",
    )(x_ctx, x_smp, mod, g, w)


def _softmax_parts(parts):
    m = parts[0].max(axis=-1, keepdims=True)
    for s in parts[1:]:
        m = jnp.maximum(m, s.max(axis=-1, keepdims=True))
    es = [jnp.exp(s - m) for s in parts]
    l = es[0].sum(axis=-1, keepdims=True)
    for e in es[1:]:
        l = l + e.sum(axis=-1, keepdims=True)
    inv = 1.0 / l
    return [e * inv for e in es]


def _attn_ctx_kernel(q_ref, k_ref, v_ref, o_ref):
    for h in range(N_HEADS):
        sl = slice(h * HEAD_DIM, (h + 1) * HEAD_DIM)
        s = _dot_nt(q_ref[:, sl], k_ref[:, sl]) * ATT_SCALE
        (p,) = _softmax_parts([s])
        o_ref[:, sl] = _dot(p.astype(BF16), v_ref[:, sl]).astype(BF16)


def _attn_ctx(q, kb, vb):
    blk = pl.BlockSpec((SEQ, HALF), lambda b: (b, 0))
    return pl.pallas_call(
        _attn_ctx_kernel,
        grid=(BATCH,),
        in_specs=[blk, blk, blk],
        out_specs=blk,
        out_shape=jax.ShapeDtypeStruct((T_CTX, HALF), BF16),
        compiler_params=_cparams("parallel"),
        name="attn_context",
    )(q, kb, vb)


NA_Q_ROWS = 4
NA_K_ROWS = NA_WIN_R + NA_Q_ROWS


def _na_key_row0(rb):
    return jnp.clip(rb * NA_Q_ROWS - NA_WIN_R // 2, 0, GRID_H - NA_K_ROWS)


def _attn_na_kernel(q_ref, k_ref, v_ref, kc_ref, vc_ref, bias_ref, o_ref):
    rb = pl.program_id(1)
    start = pl.multiple_of(_na_key_row0(rb) * GRID_W, GRID_W)
    n_loc = NA_K_ROWS * GRID_W
    for h in range(N_HEADS):
        sl = slice(h * HEAD_DIM, (h + 1) * HEAD_DIM)
        q = q_ref[:, sl]
        s_loc = _dot_nt(q, k_ref[pl.ds(start, n_loc), sl]) * ATT_SCALE + bias_ref[h]
        s_ctx = _dot_nt(q, kc_ref[:, sl].astype(BF16)) * ATT_SCALE
        p_loc, p_ctx = _softmax_parts([s_loc, s_ctx])
        o = _dot(p_loc.astype(BF16), v_ref[pl.ds(start, n_loc), sl])
        o = o + _dot(p_ctx.astype(BF16), vc_ref[:, sl].astype(BF16))
        o_ref[:, sl] = o.astype(BF16)


def _na_bias_table(rpb):
    cols = np.arange(GRID_W)
    col_start = np.clip(cols - NA_WIN_C // 2, 0, GRID_W - NA_WIN_C)
    col_in = (cols[None, :] >= col_start[:, None]) & (cols[None, :] < col_start[:, None] + NA_WIN_C)
    col_off = np.clip(cols[None, :] - cols[:, None], 1 - NA_WIN_C, NA_WIN_C - 1) + (NA_WIN_C - 1)
    n_off = 2 * NA_WIN_C - 1
    pick = jnp.asarray(col_off[None] == np.arange(n_off)[:, None, None], F32)
    by_col = (rpb.astype(F32)[:, :, :, None, None] * pick[None, None]).sum(axis=2)
    by_col = jnp.pad(by_col, ((0, 0), (NA_K_ROWS, NA_K_ROWS), (0, 0), (0, 0)))
    n_blocks = GRID_H // NA_Q_ROWS
    variants = []
    for rb in (0, 1, n_blocks - 1):
        k_row0 = int(np.clip(rb * NA_Q_ROWS - NA_WIN_R // 2, 0, GRID_H - NA_K_ROWS))
        k_row = k_row0 + np.arange(NA_K_ROWS)
        per_q_row = []
        for qr in range(NA_Q_ROWS):
            q_row = rb * NA_Q_ROWS + qr
            win0 = int(np.clip(q_row - NA_WIN_R // 2, 0, GRID_H - NA_WIN_R))
            row_in = (k_row >= win0) & (k_row < win0 + NA_WIN_R)
            ro0 = k_row0 - q_row + (NA_WIN_R - 1) + NA_K_ROWS
            b = by_col[:, ro0:ro0 + NA_K_ROWS]
            keep = row_in[:, None, None] & col_in[None]
            per_q_row.append(jnp.where(keep[None], b, NEG_INF))
        b = jnp.stack(per_q_row, axis=1)
        variants.append(b.transpose(0, 1, 3, 2, 4).reshape(N_HEADS, NA_Q_ROWS * GRID_W, NA_K_ROWS * GRID_W))
    return jnp.stack(variants)


def _attn_na(q, kb, vb, cache_k, cache_v, bias):
    tq = NA_Q_ROWS * GRID_W
    n_blocks = GRID_H // NA_Q_ROWS
    ctx_q_tiles = T_CTX // tq
    ctx_b_tiles = T_CTX // DEC_SEQ

    def bias_idx(b, rb):
        return (jnp.where(rb == 0, 0, jnp.where(rb == n_blocks - 1, 2, 1)), 0, 0, 0)

    full = pl.BlockSpec((DEC_SEQ, HALF), lambda b, r: (ctx_b_tiles + b, 0))
    cache = pl.BlockSpec((None, PAST_LEN, HALF), lambda b, r: (b, 0, 0))
    return pl.pallas_call(
        _attn_na_kernel,
        grid=(DEC_BATCH, n_blocks),
        in_specs=[
            pl.BlockSpec((tq, HALF), lambda b, r: (ctx_q_tiles + b * n_blocks + r, 0)),
            full, full, cache, cache,
            pl.BlockSpec((None, N_HEADS, tq, NA_K_ROWS * GRID_W), bias_idx),
        ],
        out_specs=pl.BlockSpec((tq, HALF), lambda b, r: (b * n_blocks + r, 0)),
        out_shape=jax.ShapeDtypeStruct((T_SMP, HALF), BF16),
        compiler_params=_cparams("parallel", "arbitrary"),
        name="attn_neighbourhood",
    )(q, kb, vb, cache_k, cache_v, bias)


def _seq_edges(i, tm):
    n_ctx = T_CTX // tm
    ctx_per = SEQ // tm
    smp_per = DEC_SEQ // tm
    j = i - n_ctx
    first = jnp.where(i < n_ctx, i % ctx_per == 0, j % smp_per == 0)
    last = jnp.where(i < n_ctx, i % ctx_per == ctx_per - 1, j % smp_per == smp_per - 1)
    return first, last


def _outproj_e_kernel(bcu_ref, prev_ref, next_ref, oc_ref, os_ref, xc_ref, xs_ref, mod_ref, wc_ref, w_ref, o_ref,
                      *, tm):
    i = pl.program_id(0)
    first, last = _seq_edges(i, tm)
    bcu = bcu_ref[...]
    cu = bcu[:, HALF:2 * HALF] * bcu[:, 2 * HALF:3 * HALF]
    pv = prev_ref[7:8, :]
    nx = next_ref[0:1, :]
    cu_prev_row = jnp.where(first, 0.0, pv[:, HALF:2 * HALF] * pv[:, 2 * HALF:3 * HALF])
    cu_next_row = jnp.where(last, 0.0, nx[:, HALF:2 * HALF] * nx[:, 2 * HALF:3 * HALF])
    rows = lax.broadcasted_iota(jnp.int32, (tm, HALF), 0)
    cu_prev = jnp.where(rows == 0, cu_prev_row, pltpu.roll(cu, 1, axis=0))
    cu_next = jnp.where(rows == tm - 1, cu_next_row, pltpu.roll(cu, tm - 1, axis=0))
    wc = wc_ref[...]
    y_a = bcu[:, 0:HALF] * (cu_prev * wc[0:1] + cu * wc[1:2] + cu_next * wc[2:3])
    y_b = _pick_ctx_smp(tm, oc_ref, os_ref)
    y = _dot(y_a.astype(BF16), w_ref[0:HALF, :]) + _dot(y_b, w_ref[HALF:D_MODEL, :])
    o_ref[...] = _pick_ctx_smp(tm, xc_ref, xs_ref) + mod_ref[:, 2 * D_MODEL:3 * D_MODEL] * y


def _out_proj_e(bcu, o_ctx, o_smp, x_ctx, x_smp, mod, w_conv_t, w):
    tm = 256
    cidx = _cond_index(tm)
    row = lambda i: (i, 0)
    nb8 = T_ALL // 8
    return pl.pallas_call(
        functools.partial(_outproj_e_kernel, tm=tm),
        grid=(T_ALL // tm,),
        in_specs=[
            pl.BlockSpec((tm, 3 * HALF), row),
            pl.BlockSpec((8, 3 * HALF), lambda i: (jnp.maximum(i * (tm // 8) - 1, 0), 0)),
            pl.BlockSpec((8, 3 * HALF), lambda i: (jnp.minimum((i + 1) * (tm // 8), nb8 - 1), 0)),
            *_ctx_smp_specs(tm, HALF),
            *_ctx_smp_specs(tm, D_MODEL),
            pl.BlockSpec((None, 1, N_MOD), lambda i: (cidx(i), 0, 0)),
            pl.BlockSpec((3, HALF), lambda i: (0, 0)),
            pl.BlockSpec((D_MODEL, D_MODEL), lambda i: (0, 0)),
        ],
        out_specs=pl.BlockSpec((tm, D_MODEL), row),
        out_shape=jax.ShapeDtypeStruct((T_ALL, D_MODEL), F32),
        compiler_params=_cparams("parallel"),
        name="out_proj_even",
    )(bcu, bcu, bcu, o_ctx, o_smp, x_ctx, x_smp, mod, w_conv_t, w)


def _ffn_kernel(x_ref, mod_ref, g_ref, wg_ref, wu_ref, wd_ref, o_ref, h_ref, acc_ref):
    f = pl.program_id(1)

    @pl.when(f == 0)
    def _():
        m = mod_ref[...]
        h = _norm_mod(x_ref[...], g_ref[...], m[:, 4 * D_MODEL:5 * D_MODEL], m[:, 3 * D_MODEL:4 * D_MODEL])
        h_ref[...] = h.astype(BF16)
        acc_ref[...] = jnp.zeros_like(acc_ref)

    h = h_ref[...]
    a = _silu(_dot(h, wg_ref[...])) * _dot(h, wu_ref[...])
    acc_ref[...] += _dot(a.astype(BF16), wd_ref[...])

    @pl.when(f == pl.num_programs(1) - 1)
    def _():
        o_ref[...] = x_ref[...] + mod_ref[:, 5 * D_MODEL:6 * D_MODEL] * acc_ref[...]


def _ffn(x, mod, g, wg, wu, wd):
    tm = 1024
    tf = D_FF // 2
    cidx = _cond_index(tm)
    return pl.pallas_call(
        _ffn_kernel,
        grid=(T_ALL // tm, D_FF // tf),
        in_specs=[
            pl.BlockSpec((tm, D_MODEL), lambda i, f: (i, 0)),
            pl.BlockSpec((None, 1, N_MOD), lambda i, f: (cidx(i), 0, 0)),
            pl.BlockSpec((1, D_MODEL), lambda i, f: (0, 0)),
            pl.BlockSpec((D_MODEL, tf), lambda i, f: (0, f)),
            pl.BlockSpec((D_MODEL, tf), lambda i, f: (0, f)),
            pl.BlockSpec((tf, D_MODEL), lambda i, f: (f, 0)),
        ],
        out_specs=pl.BlockSpec((tm, D_MODEL), lambda i, f: (i, 0)),
        out_shape=jax.ShapeDtypeStruct((T_ALL, D_MODEL), F32),
        scratch_shapes=[pltpu.VMEM((tm, D_MODEL), BF16), pltpu.VMEM((tm, D_MODEL), F32)],
        compiler_params=_cparams("parallel", "arbitrary"),
        name="ffn_dense",
    )(x, mod, g, wg, wu, wd)


def _inproj_o_kernel(x_ref, mod_ref, g_ref, w_ref, mch_ref, mcl_ref, ar_ref, ai_ref, q_ref, k_ref, v_ref, gt_ref):
    m = mod_ref[...]
    h = _norm_mod(x_ref[...], g_ref[...], m[:, D_MODEL:2 * D_MODEL], m[:, 0:D_MODEL]).astype(BF16)
    u_hi, u_lo = _split(_dot(h, w_ref[:, 0:HALF]))
    for grp in range(FT_GROUPS):
        sl = slice(grp * FT_GROUP_CH, (grp + 1) * FT_GROUP_CH)
        a = _dot3(u_hi[:, sl], u_lo[:, sl], mch_ref[...], mcl_ref[...])
        ar_ref[:, sl] = a[:, 0:FT_GROUP_CH]
        ai_ref[:, sl] = a[:, FT_GROUP_CH:2 * FT_GROUP_CH]
    q_ref[...] = _dot(h, w_ref[:, HALF:2 * HALF]).astype(BF16)
    k_ref[...] = _dot(h, w_ref[:, 2 * HALF:3 * HALF]) * ATT_SCALE
    v_ref[...] = _dot(h, w_ref[:, 3 * HALF:4 * HALF]).astype(BF16)
    gt_ref[...] = _dot(h, w_ref[:, 4 * HALF:5 * HALF])


def _in_proj_o(x, mod, g, w, mc_hi, mc_lo):
    tm = 512
    cidx = _cond_index(tm)
    row = lambda i: (i, 0)
    half_out = pl.BlockSpec((tm, HALF), row)
    sds = jax.ShapeDtypeStruct
    return pl.pallas_call(
        _inproj_o_kernel,
        grid=(T_ALL // tm,),
        in_specs=[
            pl.BlockSpec((tm, D_MODEL), row),
            pl.BlockSpec((None, 1, N_MOD), lambda i: (cidx(i), 0, 0)),
            pl.BlockSpec((1, D_MODEL), lambda i: (0, 0)),
            pl.BlockSpec((D_MODEL, 5 * HALF), lambda i: (0, 0)),
            pl.BlockSpec((FT_GROUP_CH, 2 * FT_GROUP_CH), lambda i: (0, 0)),
            pl.BlockSpec((FT_GROUP_CH, 2 * FT_GROUP_CH), lambda i: (0, 0)),
        ],
        out_specs=[half_out] * 6,
        out_shape=[
            sds((T_ALL, HALF), F32),
            sds((T_ALL, HALF), F32),
            sds((T_ALL, HALF), BF16),
            sds((T_ALL, HALF), F32),
            sds((T_ALL, HALF), BF16),
            sds((T_ALL, HALF), F32),
        ],
        compiler_params=_cparams("parallel"),
        name="in_proj_odd",
    )(x, mod, g, w, mc_hi, mc_lo)


def _dft_cos_sin(n):
    k = np.arange(n)
    ang = 2.0 * np.pi * ((k[:, None] * k[None, :]) % n) / n
    return np.cos(ang), np.sin(ang)


def _hi_lo(a):
    a = jnp.asarray(a, F32)
    hi = a.astype(BF16)
    return hi, (a - hi.astype(F32)).astype(BF16)


def _channel_dft_table():
    c, s = _dft_cos_sin(FT_GROUP_CH)
    scale = FT_GROUP_CH ** -0.5
    return np.concatenate([c * scale, -s * scale], axis=1)


def _fourier_ctx_kernel(ar_ref, ai_ref, th_ref, tl_ref, o_ref):
    a_hi, a_lo = _split(jnp.concatenate([ar_ref[...], ai_ref[...]], axis=0))
    o_ref[...] = _dot3(th_ref[...], tl_ref[...], a_hi, a_lo).astype(BF16)


def _fourier_ctx(ar, ai):
    c, s = _dft_cos_sin(SEQ)
    t_hi, t_lo = _hi_lo(np.concatenate([c, s], axis=1) * SEQ ** -0.5)
    blk = pl.BlockSpec((SEQ, HALF), lambda b: (b, 0))
    tab = pl.BlockSpec((SEQ, 2 * SEQ), lambda b: (0, 0))
    return pl.pallas_call(
        _fourier_ctx_kernel,
        grid=(BATCH,),
        in_specs=[blk, blk, tab, tab],
        out_specs=blk,
        out_shape=jax.ShapeDtypeStruct((T_CTX, HALF), BF16),
        compiler_params=_cparams("parallel"),
        name="fourier_context",
    )(ar, ai, t_hi, t_lo)


def _fourier_s1_kernel(ar_ref, ai_ref, mh_ref, ml_ref, yr_ref, yi_ref):
    group = 16
    lanes = ar_ref.shape[-1]

    def p2_group(g, c):
        rows = [pl.ds(g * group + j, FT_N1, stride=FT_N1) for j in range(group)]
        a = jnp.concatenate([jnp.concatenate([ar_ref[r, :] for r in rows], axis=-1),
                             jnp.concatenate([ai_ref[r, :] for r in rows], axis=-1)], axis=0)
        a_hi, a_lo = _split(a)
        y = _dot3(mh_ref[...], ml_ref[...], a_hi, a_lo)
        for j, r in enumerate(rows):
            yr_ref[r, :] = y[0:FT_N1, j * lanes:(j + 1) * lanes]
            yi_ref[r, :] = y[FT_N1:2 * FT_N1, j * lanes:(j + 1) * lanes]
        return c

    lax.fori_loop(0, FT_N1 // group, p2_group, 0)


def _fourier_s3_kernel(yr_ref, yi_ref, tc_ref, ts_ref, mh_ref, ml_ref, o_ref, *, nk):
    lanes = HALF // tc_ref.shape[-1]
    for j in range(nk):
        tc = jnp.concatenate([tc_ref[j]] * lanes, axis=-1)
        ts = jnp.concatenate([ts_ref[j]] * lanes, axis=-1)
        yr = yr_ref[j * FT_N1:(j + 1) * FT_N1, :]
        yi = yi_ref[j * FT_N1:(j + 1) * FT_N1, :]
        z = jnp.concatenate([yr * tc + yi * ts, yi * tc - yr * ts], axis=0)
        z_hi, z_lo = _split(z)
        o_ref[:, j * HALF:(j + 1) * HALF] = _dot3(mh_ref[...], ml_ref[...], z_hi, z_lo).astype(BF16)


def _fourier_smp(ar, ai):
    n1 = FT_N1
    wide = n1 * HALF
    c, s = _dft_cos_sin(n1)
    m1_hi, m1_lo = _hi_lo(np.block([[c, s], [-s, c]]))
    m3_hi, m3_lo = _hi_lo(np.concatenate([c, s], axis=1) / n1)
    kk = np.arange(n1)
    ang = 2.0 * np.pi * (kk[:, None] * kk[None, :]) / DEC_SEQ
    tw_c = jnp.broadcast_to(jnp.asarray(np.cos(ang), F32)[:, :, None], (n1, n1, 128))
    tw_s = jnp.broadcast_to(jnp.asarray(np.sin(ang), F32)[:, :, None], (n1, n1, 128))

    ncol = 128
    ctx_tiles = T_CTX // DEC_SEQ
    a_blk = pl.BlockSpec((DEC_SEQ, ncol), lambda b, j: (ctx_tiles + b, j))
    y_blk = pl.BlockSpec((DEC_SEQ, ncol), lambda b, j: (b, j))
    m1_blk = pl.BlockSpec((2 * n1, 2 * n1), lambda b, j: (0, 0))
    y_sds = jax.ShapeDtypeStruct((T_SMP, HALF), F32)
    yr, yi = pl.pallas_call(
        _fourier_s1_kernel,
        grid=(DEC_BATCH, HALF // ncol),
        in_specs=[a_blk, a_blk, m1_blk, m1_blk],
        out_specs=[y_blk, y_blk],
        out_shape=[y_sds, y_sds],
        compiler_params=_cparams("parallel", "parallel"),
        name="fourier_latent_stage1",
    )(ar, ai, m1_hi, m1_lo)

    nk = 8
    z_blk = pl.BlockSpec((nk * n1, HALF), lambda b, k: (b * (n1 // nk) + k, 0))
    tw_blk = pl.BlockSpec((nk, n1, 128), lambda b, k: (k, 0, 0))
    m3_blk = pl.BlockSpec((n1, 2 * n1), lambda b, k: (0, 0))
    out = pl.pallas_call(
        functools.partial(_fourier_s3_kernel, nk=nk),
        grid=(DEC_BATCH, n1 // nk),
        in_specs=[z_blk, z_blk, tw_blk, tw_blk, m3_blk, m3_blk],
        out_specs=pl.BlockSpec((n1, nk * HALF), lambda b, k: (b, k)),
        out_shape=jax.ShapeDtypeStruct((DEC_BATCH * n1, wide), BF16),
        compiler_params=_cparams("parallel", "parallel"),
        name="fourier_latent_stage3",
    )(yr, yi, tw_c, tw_s, m3_hi, m3_lo)
    return out.reshape(T_SMP, HALF)


def _retention_kernel(lg_ref, q_ref, k_ref, v_ref, g_ref, *rest, seq, chunk, nh, has_s0):
    if has_s0:
        s0_ref, y_ref, of_ref, ob_ref, dec_ref, xi_ref, zeta_ref = rest
        st_ref = None
    else:
        y_ref, st_ref, of_ref, ob_ref, dec_ref, xi_ref, zeta_ref = rest
        s0_ref = None
    L = chunk
    nc = seq // L
    hb = pl.program_id(0)
    one = jnp.ones((1, 1), F32)

    @pl.when(pl.program_id(1) == 0)
    def _():
        diff = (lax.broadcasted_iota(jnp.int32, (L, L), 0)
                - lax.broadcasted_iota(jnp.int32, (L, L), 1)).astype(F32)
        li = lax.broadcasted_iota(jnp.int32, (L, HEAD_DIM), 0).astype(F32)
        for hh in range(nh):
            lgf = lg_ref[0, hb * nh + hh]
            lgb = lg_ref[1, hb * nh + hh]
            dec_ref[2 * hh] = jnp.where(diff >= 0, jnp.exp(lgf * jnp.maximum(diff, 0.0)), 0.0)
            dec_ref[2 * hh + 1] = jnp.where(diff <= 0, jnp.exp(lgb * jnp.maximum(-diff, 0.0)), 0.0)
            xi_ref[2 * hh] = jnp.exp(lgf * (li + 1.0))
            xi_ref[2 * hh + 1] = jnp.exp(lgb * (L - li))
            zeta_ref[2 * hh] = jnp.exp(lgf * (L - 1.0 - li))
            zeta_ref[2 * hh + 1] = jnp.exp(lgb * li)

    gcs = []
    for hh in range(nh):
        gcs.append(jnp.exp(one * (lg_ref[0, hb * nh + hh] * L)))
        gcs.append(jnp.exp(one * (lg_ref[1, hb * nh + hh] * L)))

    def chunk(c, s, t, sl, o_ref):
        r0 = pl.multiple_of(c * L, L)
        qc = q_ref[pl.ds(r0, L), sl]
        kc = k_ref[pl.ds(r0, L), sl]
        vc = v_ref[pl.ds(r0, L), sl]
        inner = _dot_nt(qc, kc.astype(BF16)) * dec_ref[t]
        o_ref[pl.ds(r0, L), sl] = _dot(inner.astype(BF16), vc) + _dot(qc, s.astype(BF16)) * xi_ref[t]
        return s * gcs[t] + _dot((kc * zeta_ref[t]).T.astype(BF16), vc)

    def scan_step(i, states):
        out = []
        for hh in range(nh):
            sl = slice(hh * HEAD_DIM, (hh + 1) * HEAD_DIM)
            out.append(chunk(i, states[2 * hh], 2 * hh, sl, of_ref))
            out.append(chunk(nc - 1 - i, states[2 * hh + 1], 2 * hh + 1, sl, ob_ref))
        return tuple(out)

    if has_s0:
        init = tuple(s0_ref[t % 2, t // 2] for t in range(2 * nh))
    else:
        init = tuple(jnp.zeros((HEAD_DIM, HEAD_DIM), F32) for _ in range(2 * nh))
    final = lax.fori_loop(0, nc, scan_step, init)
    if st_ref is not None:
        for t in range(2 * nh):
            st_ref[t % 2, t // 2] = final[t]

    def finish(c, carry):
        r0 = pl.multiple_of(c * L, L)
        o_all = of_ref[pl.ds(r0, L), :] + ob_ref[pl.ds(r0, L), :]
        gate = _silu(g_ref[pl.ds(r0, L), :])
        for hh in range(nh):
            sl = slice(hh * HEAD_DIM, (hh + 1) * HEAD_DIM)
            o = o_all[:, sl]
            mu = jnp.mean(o, axis=-1, keepdims=True)
            var = jnp.mean(jnp.square(o - mu), axis=-1, keepdims=True)
            y_ref[pl.ds(r0, L), sl] = (gate[:, sl] * ((o - mu) * lax.rsqrt(var + EPS))).astype(BF16)
        return carry

    lax.fori_loop(0, nc, finish, 0)


def _retention(lg, q, k, v, g, s0, *, seq, chunk, nbatch, row0, nh):
    has_s0 = s0 is not None
    tile0 = row0 // seq
    width = nh * HEAD_DIM
    blk = pl.BlockSpec((seq, width), lambda hb, b: (tile0 + b, hb))
    st_blk = pl.BlockSpec((None, 2, nh, HEAD_DIM, HEAD_DIM), lambda hb, b: (b, 0, hb, 0, 0))
    in_specs = [pl.BlockSpec(memory_space=pltpu.SMEM), blk, blk, blk, blk]
    args = [lg, q, k, v, g]
    y_spec = pl.BlockSpec((seq, width), lambda hb, b: (b, hb))
    y_sds = jax.ShapeDtypeStruct((nbatch * seq, HALF), BF16)
    if has_s0:
        in_specs.append(st_blk)
        args.append(s0)
        out_specs, out_shape = y_spec, y_sds
    else:
        out_specs = [y_spec, st_blk]
        out_shape = [y_sds, jax.ShapeDtypeStruct((nbatch, 2, N_HEADS, HEAD_DIM, HEAD_DIM), F32)]
    return pl.pallas_call(
        functools.partial(_retention_kernel, seq=seq, chunk=chunk, nh=nh, has_s0=has_s0),
        grid=(N_HEADS // nh, nbatch),
        in_specs=in_specs,
        out_specs=out_specs,
        out_shape=out_shape,
        scratch_shapes=[
            pltpu.VMEM((seq, width), F32),
            pltpu.VMEM((seq, width), F32),
            pltpu.VMEM((2 * nh, chunk, chunk), F32),
            pltpu.VMEM((2 * nh, chunk, HEAD_DIM), F32),
            pltpu.VMEM((2 * nh, chunk, HEAD_DIM), F32),
        ],
        compiler_params=_cparams("arbitrary", "arbitrary"),
        name="retention_%d" % seq,
    )(*args)


def _outproj_o_kernel(ycc_ref, ycs_ref, ydc_ref, yds_ref, x_ref, mod_ref, w_ref, g_ref, rh_ref, rl_ref,
                      x3_ref, h_ref, e1_ref, e2_ref, ga_ref, gb_ref, *, tm):
    is_ctx = pl.program_id(0) < T_CTX // tm
    y_c = jnp.where(is_ctx, ycc_ref[...], ycs_ref[...])
    y_d = jnp.where(is_ctx, ydc_ref[...], yds_ref[...])
    y = _dot(y_c, w_ref[0:HALF, :]) + _dot(y_d, w_ref[HALF:D_MODEL, :])
    m = mod_ref[...]
    x3 = x_ref[...] + m[:, 2 * D_MODEL:3 * D_MODEL] * y
    x3_ref[...] = x3
    h = _norm_mod(x3, g_ref[...], m[:, 4 * D_MODEL:5 * D_MODEL], m[:, 3 * D_MODEL:4 * D_MODEL])
    h_ref[...] = h
    h_hi, h_lo = _split(h)
    logits = _dot3(h_hi, h_lo, rh_ref[...], rl_ref[...])
    idx = lax.broadcasted_iota(jnp.int32, logits.shape, 1).astype(F32)
    logits = jnp.where(idx < float(N_EXPERTS), logits, -jnp.inf)
    m1 = logits.max(axis=-1, keepdims=True)
    e1 = jnp.where(logits == m1, idx, float(N_EXPERTS)).min(axis=-1, keepdims=True)
    rest = jnp.where(idx == e1, -jnp.inf, logits)
    m2 = rest.max(axis=-1, keepdims=True)
    e2 = jnp.where(rest == m2, idx, float(N_EXPERTS)).min(axis=-1, keepdims=True)
    ex = jnp.exp(m2 - m1)
    den = 1.0 + ex
    wide = (tm, 128)
    e1_ref[...] = jnp.broadcast_to(e1, wide).astype(jnp.int32)
    e2_ref[...] = jnp.broadcast_to(e2, wide).astype(jnp.int32)
    ga_ref[...] = jnp.broadcast_to(1.0 / den, wide)
    gb_ref[...] = jnp.broadcast_to(ex / den, wide)


def _out_proj_o(yc_ctx, yc_smp, yd_ctx, yd_smp, x, mod, w, g, r_hi, r_lo):
    tm = 256
    n_ctx = T_CTX // tm
    cidx = _cond_index(tm)
    row = lambda i: (i, 0)
    ctx_blk = pl.BlockSpec((tm, HALF), lambda i: (jnp.minimum(i, n_ctx - 1), 0))
    smp_blk = pl.BlockSpec((tm, HALF), lambda i: (jnp.maximum(i - n_ctx, 0), 0))
    sds = jax.ShapeDtypeStruct
    rep = pl.BlockSpec((tm, 128), row)
    return pl.pallas_call(
        functools.partial(_outproj_o_kernel, tm=tm),
        grid=(T_ALL // tm,),
        in_specs=[
            ctx_blk, smp_blk, ctx_blk, smp_blk,
            pl.BlockSpec((tm, D_MODEL), row),
            pl.BlockSpec((None, 1, N_MOD), lambda i: (cidx(i), 0, 0)),
            pl.BlockSpec((D_MODEL, D_MODEL), lambda i: (0, 0)),
            pl.BlockSpec((1, D_MODEL), lambda i: (0, 0)),
            pl.BlockSpec((D_MODEL, 128), lambda i: (0, 0)),
            pl.BlockSpec((D_MODEL, 128), lambda i: (0, 0)),
        ],
        out_specs=[pl.BlockSpec((tm, D_MODEL), row), pl.BlockSpec((tm, D_MODEL), row), rep, rep, rep, rep],
        out_shape=[
            sds((T_ALL, D_MODEL), F32), sds((T_ALL, D_MODEL), F32),
            sds((T_ALL, 128), jnp.int32), sds((T_ALL, 128), jnp.int32),
            sds((T_ALL, 128), F32), sds((T_ALL, 128), F32),
        ],
        compiler_params=_cparams("parallel"),
        name="out_proj_odd_route",
    )(yc_ctx, yc_smp, yd_ctx, yd_smp, x, mod, w, g, r_hi, r_lo)


def _routing_tables(e1, e2):
    flat_e = jnp.concatenate([e1, e2])
    onehot = (flat_e[:, None] == jnp.arange(N_EXPERTS, dtype=jnp.int32)[None, :]).astype(jnp.int32)
    csum = jnp.cumsum(onehot, axis=0)
    counts = csum[-1]
    padded = (counts + MOE_TM - 1) // MOE_TM * MOE_TM
    ends = jnp.cumsum(padded)
    pos = jnp.sum(onehot * (csum - 1 + (ends - padded)[None, :]), axis=1)
    tok = jnp.tile(jnp.arange(T_ALL, dtype=jnp.int32), 2)
    slot_t = jnp.zeros((MOE_P,), jnp.int32).at[pos].set(tok)
    block_row0 = jnp.arange(MOE_NB, dtype=jnp.int32) * MOE_TM
    block_e = jnp.minimum(
        jnp.sum((ends[None, :] <= block_row0[:, None]).astype(jnp.int32), axis=1), N_EXPERTS - 1
    ).astype(jnp.int32)
    n_valid = (ends[-1] // MOE_TM).astype(jnp.int32).reshape(1)
    return slot_t, pos[:T_ALL].astype(jnp.int32), pos[T_ALL:].astype(jnp.int32), block_e, n_valid


def _row_copy(src_ref, dst_ref, sem, src_row, dst_row):
    return pltpu.make_async_copy(src_ref.at[pl.ds(src_row, 1), :], dst_ref.at[pl.ds(dst_row, 1), :], sem)


def _start_rows(src_ref, dst_ref, sem, idx_ref, tm):
    def start(r, c):
        _row_copy(src_ref, dst_ref, sem, idx_ref[0, r], r).start()
        return c

    lax.fori_loop(0, tm, start, 0, unroll=8)


def _wait_rows(src_ref, dst_ref, sem, tm):
    pltpu.make_async_copy(src_ref.at[pl.ds(0, tm), :], dst_ref, sem).wait()


def _gather_kernel(idx_ref, nxt_ref, src_ref, o_ref, buf_ref, sems, *, tm):
    i = pl.program_id(0)
    slot = i % 2

    @pl.when(i == 0)
    def _():
        _start_rows(src_ref, buf_ref.at[0], sems.at[0], idx_ref, tm)

    @pl.when(i + 1 < pl.num_programs(0))
    def _():
        _start_rows(src_ref, buf_ref.at[1 - slot], sems.at[1 - slot], nxt_ref, tm)

    _wait_rows(src_ref, buf_ref.at[slot], sems.at[slot], tm)
    o_ref[...] = buf_ref[slot]


def _next_idx_spec(tm, nblk):
    return pl.BlockSpec((None, 1, tm), lambda i: (jnp.minimum(i + 1, nblk - 1), 0, 0), memory_space=pltpu.SMEM)


def _gather_rows(src, slot_t):
    tm = GATHER_TM
    nblk = MOE_P // tm
    idx = slot_t.reshape(nblk, 1, tm)
    return pl.pallas_call(
        functools.partial(_gather_kernel, tm=tm),
        grid=(nblk,),
        in_specs=[
            pl.BlockSpec((None, 1, tm), lambda i: (i, 0, 0), memory_space=pltpu.SMEM),
            _next_idx_spec(tm, nblk),
            pl.BlockSpec(memory_space=pl.ANY),
        ],
        out_specs=pl.BlockSpec((tm, D_MODEL), lambda i: (i, 0)),
        out_shape=jax.ShapeDtypeStruct((MOE_P, D_MODEL), F32),
        scratch_shapes=[pltpu.VMEM((2, tm, D_MODEL), F32), pltpu.SemaphoreType.DMA((2,))],
        compiler_params=_cparams("arbitrary"),
        name="moe_gather",
    )(idx, idx, src)


def _experts_kernel(be_ref, nv_ref, x_ref, wg_ref, wu_ref, wd_ref, o_ref, xb_ref, acc_ref):
    i = pl.program_id(0)
    f = pl.program_id(1)
    valid = i < nv_ref[0]

    @pl.when(jnp.logical_and(valid, f == 0))
    def _():
        xb_ref[...] = x_ref[...].astype(BF16)
        acc_ref[...] = jnp.zeros_like(acc_ref)

    @pl.when(valid)
    def _():
        x = xb_ref[...]
        a = _silu(_dot(x, wg_ref[...])) * _dot(x, wu_ref[...])
        acc_ref[...] += _dot(a.astype(BF16), wd_ref[...])

    last = f == pl.num_programs(1) - 1

    @pl.when(jnp.logical_and(valid, last))
    def _():
        o_ref[...] = acc_ref[...]

    @pl.when(jnp.logical_and(jnp.logical_not(valid), last))
    def _():
        o_ref[...] = jnp.zeros_like(o_ref)


def _experts(xs, block_e, n_valid, wg, wu, wd):
    nf = D_FF_EXPERT // MOE_TF

    def f_eff(i, f, nv):
        return jnp.where(i < nv[0], f, nf - 1)

    return pl.pallas_call(
        _experts_kernel,
        grid_spec=pltpu.PrefetchScalarGridSpec(
            num_scalar_prefetch=2,
            grid=(MOE_NB, nf),
            in_specs=[
                pl.BlockSpec((MOE_TM, D_MODEL), lambda i, f, be, nv: (i, 0)),
                pl.BlockSpec((None, D_MODEL, MOE_TF), lambda i, f, be, nv: (be[i], 0, f_eff(i, f, nv))),
                pl.BlockSpec((None, D_MODEL, MOE_TF), lambda i, f, be, nv: (be[i], 0, f_eff(i, f, nv))),
                pl.BlockSpec((None, MOE_TF, D_MODEL), lambda i, f, be, nv: (be[i], f_eff(i, f, nv), 0)),
            ],
            out_specs=pl.BlockSpec((MOE_TM, D_MODEL), lambda i, f, be, nv: (i, 0)),
            scratch_shapes=[pltpu.VMEM((MOE_TM, D_MODEL), BF16), pltpu.VMEM((MOE_TM, D_MODEL), F32)],
        ),
        out_shape=jax.ShapeDtypeStruct((MOE_P, D_MODEL), F32),
        compiler_params=_cparams("arbitrary", "arbitrary"),
        name="moe_experts",
    )(block_e, n_valid, xs, wg, wu, wd)


def _combine_kernel(p1_ref, p2_ref, n1_ref, n2_ref, ys_ref, x_ref, ga_ref, gb_ref, mod_ref, g_ref, oc_ref, os_ref,
                    a_ref, b_ref, sems, *, tm):
    i = pl.program_id(0)
    slot = i % 2

    def start_tile(s, i1_ref, i2_ref):
        _start_rows(ys_ref, a_ref.at[s], sems.at[0, s], i1_ref, tm)
        _start_rows(ys_ref, b_ref.at[s], sems.at[1, s], i2_ref, tm)

    @pl.when(i == 0)
    def _():
        start_tile(0, p1_ref, p2_ref)

    @pl.when(i + 1 < pl.num_programs(0))
    def _():
        start_tile(1 - slot, n1_ref, n2_ref)

    _wait_rows(ys_ref, a_ref.at[slot], sems.at[0, slot], tm)
    _wait_rows(ys_ref, b_ref.at[slot], sems.at[1, slot], tm)
    reps = D_MODEL // 128
    ga = jnp.concatenate([ga_ref[...]] * reps, axis=-1)
    gb = jnp.concatenate([gb_ref[...]] * reps, axis=-1)
    f = a_ref[slot] * ga + b_ref[slot] * gb
    x = x_ref[...] + mod_ref[:, 5 * D_MODEL:6 * D_MODEL] * f
    y = (x * lax.rsqrt(jnp.mean(x * x, axis=-1, keepdims=True) + EPS)) * g_ref[...]
    is_ctx = pl.program_id(0) < T_CTX // tm

    @pl.when(is_ctx)
    def _():
        oc_ref[...] = y

    @pl.when(jnp.logical_not(is_ctx))
    def _():
        os_ref[...] = y


def _combine(ys, pos1, pos2, x, ga, gb, mod, final_g):
    tm = GATHER_TM
    nblk = T_ALL // tm
    cidx = _cond_index(tm)
    row = lambda i: (i, 0)
    idx_blk = pl.BlockSpec((None, 1, tm), lambda i: (i, 0, 0), memory_space=pltpu.SMEM)
    idx1 = pos1.reshape(nblk, 1, tm)
    idx2 = pos2.reshape(nblk, 1, tm)
    return pl.pallas_call(
        functools.partial(_combine_kernel, tm=tm),
        grid=(nblk,),
        in_specs=[
            idx_blk, idx_blk, _next_idx_spec(tm, nblk), _next_idx_spec(tm, nblk),
            pl.BlockSpec(memory_space=pl.ANY),
            pl.BlockSpec((tm, D_MODEL), row),
            pl.BlockSpec((tm, 128), row),
            pl.BlockSpec((tm, 128), row),
            pl.BlockSpec((None, 1, N_MOD), lambda i: (cidx(i), 0, 0)),
            pl.BlockSpec((1, D_MODEL), lambda i: (0, 0)),
        ],
        out_specs=list(_ctx_smp_specs(tm, D_MODEL)),
        out_shape=[jax.ShapeDtypeStruct((T_CTX, D_MODEL), F32), jax.ShapeDtypeStruct((T_SMP, D_MODEL), F32)],
        scratch_shapes=[
            pltpu.VMEM((2, tm, D_MODEL), F32),
            pltpu.VMEM((2, tm, D_MODEL), F32),
            pltpu.SemaphoreType.DMA((2, 2)),
        ],
        compiler_params=_cparams("arbitrary"),
        name="moe_combine_final_norm",
    )(idx1, idx2, idx1, idx2, ys, x, ga, gb, mod, final_g)


def kernel(x_prompt, x_sample, cache_na_k, cache_na_v, state_ret, c, c_ctx, w_ada, b_ada, norm_g, final_g, w_in_e, w_conv_e, na_rpb_e, w_out_e, ff_gate_e, ff_up_e, ff_down_e, w_in_o, ret_decay_o, w_out_o, router_o, ex_gate_o, ex_up_o, ex_down_o):
    x_ctx = x_prompt.reshape(T_CTX, D_MODEL)
    x_smp = x_sample.reshape(T_SMP, D_MODEL)
    cond = jnp.concatenate([c_ctx[None, :], c, jnp.zeros((N_COND - 1 - DEC_BATCH, D_MODEL), F32)], axis=0)
    mod = _ada(cond, w_ada, b_ada)
    mod0 = mod[0].reshape(N_COND, 1, N_MOD)
    mod1 = mod[1].reshape(N_COND, 1, N_MOD)

    bcu, q, k, v, kb, vb = _in_proj_e(x_ctx, x_smp, mod0, norm_g[0, 0][None, :], w_in_e[0].astype(BF16))
    o_ctx = _attn_ctx(q, kb, vb)
    o_smp = _attn_na(q, kb, vb,
                     cache_na_k[:, 0].reshape(DEC_BATCH, PAST_LEN, HALF),
                     cache_na_v[:, 0].reshape(DEC_BATCH, PAST_LEN, HALF),
                     _na_bias_table(na_rpb_e[0]))
    x1 = _out_proj_e(bcu, o_ctx, o_smp, x_ctx, x_smp, mod0, w_conv_e[0].T, w_out_e[0].astype(BF16))
    x2 = _ffn(x1, mod0, norm_g[0, 1][None, :], ff_gate_e[0].astype(BF16), ff_up_e[0].astype(BF16),
              ff_down_e[0].astype(BF16))

    mc_hi, mc_lo = _hi_lo(_channel_dft_table())
    ar, ai, q1, k1, v1, g1 = _in_proj_o(x2, mod1, norm_g[1, 0][None, :], w_in_o[0].astype(BF16), mc_hi, mc_lo)
    yc_ctx = _fourier_ctx(ar, ai)
    yc_smp = _fourier_smp(ar, ai)
    lg = jax.nn.log_sigmoid(ret_decay_o[0].astype(F32))
    yd_ctx, new_state = _retention(lg, q1, k1, v1, g1, None, seq=SEQ, chunk=RET_CHUNK_CTX, nbatch=BATCH, row0=0,
                                   nh=8)
    yd_smp = _retention(lg, q1, k1, v1, g1, state_ret[:, 0], seq=DEC_SEQ, chunk=RET_CHUNK_SMP, nbatch=DEC_BATCH,
                        row0=T_CTX, nh=4)
    r_hi, r_lo = _hi_lo(jnp.pad(router_o[0], ((0, 0), (0, 128 - N_EXPERTS))))
    x3, hm, e1, e2, ga, gb = _out_proj_o(yc_ctx, yc_smp, yd_ctx, yd_smp, x2, mod1, w_out_o[0].astype(BF16),
                                         norm_g[1, 1][None, :], r_hi, r_lo)
    slot_t, pos1, pos2, block_e, n_valid = _routing_tables(e1[:, 0], e2[:, 0])
    xs = _gather_rows(hm, slot_t)
    ys = _experts(xs, block_e, n_valid, ex_gate_o[0].astype(BF16), ex_up_o[0].astype(BF16),
                  ex_down_o[0].astype(BF16))
    y_ctx, y_smp = _combine(ys, pos1, pos2, x3, ga, gb, mod1, final_g[None, :])

    y_prompt = y_ctx.reshape(BATCH, SEQ, D_MODEL)
    y_sample = y_smp.reshape(DEC_BATCH, DEC_SEQ, D_MODEL)
    new_na_k = k.reshape(BATCH, 1, SEQ, N_HEADS, HEAD_DIM)
    new_na_v = v.reshape(BATCH, 1, SEQ, N_HEADS, HEAD_DIM)
    new_state_ret = new_state.reshape(BATCH, 1, 2, N_HEADS, HEAD_DIM, HEAD_DIM)
    return (y_prompt, y_sample, new_na_k, new_na_v, new_state_ret)
```
